```python
import jax, jax.numpy as jnp
from jax import lax
import numpy as np

D_MODEL = 1024
BATCH = 2
SEQ = 8192
DEPTH = 1

GRID_W = 64
CTX_LEN = 256
N_HEADS = 6
QK_NOPE_DIM = 128
QK_ROPE_DIM = 64
V_HEAD_DIM = 128
Q_LORA_RANK = 384
KV_LORA_RANK = 256
MLA_WIDTH = N_HEADS * V_HEAD_DIM
FOURIER_GROUPS = 4
FOURIER_GROUP_DIM = 64
FOURIER_WIDTH = FOURIER_GROUPS * FOURIER_GROUP_DIM
MIX_WIDTH = MLA_WIDTH + FOURIER_WIDTH
IN_PROJ_WIDTH = FOURIER_WIDTH + Q_LORA_RANK + KV_LORA_RANK + QK_ROPE_DIM
ROPE_BASE = 10000.0
Q_BLOCK = 128
N_EXPERTS = 32
TOP_K = 4
D_FF = 1024
SWIGLU_LIMIT = 7.0
SWIGLU_ALPHA = 1.702
EXPERT_BLOCK = 128
RMS_EPS = 1e-6
N_MOD = 6

kernel_name = "hybrid_mla_fnet_moe_dit_block"


def rms_norm(x, g):
    xf = x.astype(jnp.float32)
    y = xf * lax.rsqrt(jnp.mean(xf * xf, axis=-1, keepdims=True) + RMS_EPS)
    return (y * g.astype(jnp.float32)).astype(x.dtype)


def modulate(x, g, shift, scale):
    return rms_norm(x, g) * (1 + scale) + shift


def axial_rope_tables(n_tokens):
    rows = n_tokens // GRID_W
    row = jnp.repeat(jnp.arange(rows, dtype=jnp.float32), GRID_W)
    col = jnp.tile(jnp.arange(GRID_W, dtype=jnp.float32), rows)
    half = QK_ROPE_DIM // 2
    inv = 1.0 / (ROPE_BASE ** (jnp.arange(0, half, 2, dtype=jnp.float32) / half))
    ang = jnp.concatenate([row[:, None] * inv, col[:, None] * inv], axis=-1)
    return jnp.cos(ang), jnp.sin(ang)


def rotate(x, cos, sin):
    xf = x.astype(jnp.float32)
    xe, xo = xf[..., 0::2], xf[..., 1::2]
    out = jnp.stack([xe * cos - xo * sin, xe * sin + xo * cos], axis=-1)
    return out.reshape(x.shape).astype(x.dtype)


def mla_project(p, norm_q, norm_kv, w_q_up, w_kv_up, rope):
    B, N, _ = p.shape
    q_lat, kv_lat, k_rope = jnp.split(p, [Q_LORA_RANK, Q_LORA_RANK + KV_LORA_RANK], axis=-1)
    q = (rms_norm(q_lat, norm_q) @ w_q_up).reshape(B, N, N_HEADS, QK_NOPE_DIM + QK_ROPE_DIM)
    kv = (rms_norm(kv_lat, norm_kv) @ w_kv_up).reshape(B, N, N_HEADS, QK_NOPE_DIM + V_HEAD_DIM)
    q_nope, q_rope = jnp.split(q, [QK_NOPE_DIM], axis=-1)
    k_nope, v = jnp.split(kv, [QK_NOPE_DIM], axis=-1)
    if rope is not None:
        cos, sin = rope
        q_rope = rotate(q_rope, cos[:, None, :], sin[:, None, :])
        k_rope = rotate(k_rope, cos, sin)
    k_rope = jnp.broadcast_to(k_rope[:, :, None, :], (B, N, N_HEADS, QK_ROPE_DIM))
    q = jnp.concatenate([q_nope, q_rope], axis=-1)
    k = jnp.concatenate([k_nope, k_rope], axis=-1)
    return q, k, v


def attend(q, k, v):
    B, N, H, Dq = q.shape
    nb = N // Q_BLOCK
    scale = Dq ** -0.5
    qb = jnp.moveaxis(q.reshape(B, nb, Q_BLOCK, H, Dq), 1, 0)

    def one_block(q_blk):
        s = jnp.einsum('bqhd,bkhd->bhqk', q_blk, k, preferred_element_type=jnp.float32) * scale
        pr = jax.nn.softmax(s, axis=-1).astype(v.dtype)
        return jnp.einsum('bhqk,bkhd->bqhd', pr, v)

    o = lax.map(one_block, qb)
    return jnp.moveaxis(o, 0, 1).reshape(B, N, H * V_HEAD_DIM)


def fourier_mix(u, w):
    B, N, _ = u.shape
    ug = u.reshape(B, N, FOURIER_GROUPS, FOURIER_GROUP_DIM).astype(jnp.float32)
    f = jnp.fft.fft2(ug, axes=(1, 3), norm='ortho').real.astype(u.dtype)
    return jnp.einsum('bngc,gcd->bngd', f, w).reshape(B, N, FOURIER_WIDTH)


def moe(h, w_router, b_router, w_gate, b_gate, w_up, b_up, w_down, b_down):
    T, D = h.shape
    logits = jnp.dot(h, w_router, preferred_element_type=jnp.float32) + b_router.astype(jnp.float32)
    top_logit, top_idx = lax.top_k(logits, TOP_K)
    gates = jax.nn.softmax(top_logit, axis=-1)
    n_pairs = T * TOP_K
    flat_e = top_idx.reshape(n_pairs)
    order = jnp.argsort(flat_e)
    sorted_e = flat_e[order]
    pair_tok = order // TOP_K
    pair_gate = gates.reshape(n_pairs)[order].astype(h.dtype)
    counts = jnp.bincount(flat_e, length=N_EXPERTS)
    padded = (counts + EXPERT_BLOCK - 1) // EXPERT_BLOCK * EXPERT_BLOCK
    pad_end = jnp.cumsum(padded)
    pad_start = pad_end - padded
    start = jnp.cumsum(counts) - counts
    dest = pad_start[sorted_e] + jnp.arange(n_pairs) - start[sorted_e]
    n_blocks = -(-n_pairs // EXPERT_BLOCK) + N_EXPERTS
    slot_tok = jnp.zeros((n_blocks * EXPERT_BLOCK,), jnp.int32).at[dest].set(pair_tok.astype(jnp.int32))
    block_e = jnp.minimum(
        jnp.searchsorted(pad_end, jnp.arange(n_blocks) * EXPERT_BLOCK, side='right'), N_EXPERTS - 1)
    xs = h[slot_tok].reshape(n_blocks, EXPERT_BLOCK, D)

    def expert_block(args):
        xb, e = args
        g = jnp.minimum(xb @ w_gate[e] + b_gate[e], SWIGLU_LIMIT)
        u = jnp.clip(xb @ w_up[e] + b_up[e], -SWIGLU_LIMIT, SWIGLU_LIMIT)
        act = g * jax.nn.sigmoid(SWIGLU_ALPHA * g) * (u + 1)
        return act @ w_down[e] + b_down[e]

    ys = lax.map(expert_block, (xs, block_e)).reshape(n_blocks * EXPERT_BLOCK, D)
    y_pairs = ys[dest] * pair_gate[:, None]
    return jnp.zeros_like(h).at[pair_tok].add(y_pairs)


def setup_inputs(seed: int = 0) -> dict:
    key = jax.random.key(seed)
    ks = jax.random.split(key, 32)
    f32 = jnp.float32
    L, D = DEPTH, D_MODEL

    def nrm(k, shape, scale):
        return jax.random.normal(k, shape, f32) * scale

    def gain(k, shape):
        return 1.0 + 0.05 * jax.random.normal(k, shape, f32)

    return {
        'x': nrm(ks[0], (BATCH, SEQ, D), 1.0),
        'c': nrm(ks[1], (BATCH, D), 1.0),
        'ctx': nrm(ks[2], (BATCH, CTX_LEN, D), 1.0),
        'c_ctx': nrm(ks[3], (D,), 1.0),
        'w_mod': nrm(ks[4], (L, D, N_MOD * D), 0.5 * D ** -0.5),
        'b_mod': nrm(ks[5], (L, N_MOD * D), 0.01),
        'norm_attn_pre': gain(ks[6], (L, D)),
        'norm_attn_post': gain(ks[7], (L, D)),
        'norm_ffn_pre': gain(ks[8], (L, D)),
        'norm_ffn_post': gain(ks[9], (L, D)),
        'w_in': nrm(ks[10], (L, D, IN_PROJ_WIDTH), D ** -0.5),
        'norm_q_lat': gain(ks[11], (L, Q_LORA_RANK)),
        'norm_kv_lat': gain(ks[12], (L, KV_LORA_RANK)),
        'w_q_up': nrm(ks[13], (L, Q_LORA_RANK, N_HEADS * (QK_NOPE_DIM + QK_ROPE_DIM)), Q_LORA_RANK ** -0.5),
        'w_kv_up': nrm(ks[14], (L, KV_LORA_RANK, N_HEADS * (QK_NOPE_DIM + V_HEAD_DIM)), KV_LORA_RANK ** -0.5),
        'w_fourier': nrm(ks[15], (L, FOURIER_GROUPS, FOURIER_GROUP_DIM, FOURIER_GROUP_DIM), FOURIER_GROUP_DIM ** -0.5),
        'w_out': nrm(ks[16], (L, MIX_WIDTH, D), MIX_WIDTH ** -0.5),
        'w_router': nrm(ks[17], (L, D, N_EXPERTS), D ** -0.5),
        'b_router': nrm(ks[18], (L, N_EXPERTS), 0.01),
        'w_gate': nrm(ks[19], (L, N_EXPERTS, D, D_FF), D ** -0.5),
        'b_gate': nrm(ks[20], (L, N_EXPERTS, D_FF), 0.01),
        'w_up': nrm(ks[21], (L, N_EXPERTS, D, D_FF), D ** -0.5),
        'b_up': nrm(ks[22], (L, N_EXPERTS, D_FF), 0.01),
        'w_down': nrm(ks[23], (L, N_EXPERTS, D_FF, D), D_FF ** -0.5),
        'b_down': nrm(ks[24], (L, N_EXPERTS, D), 0.01),
    }


def reference(x, c, ctx, c_ctx, w_mod, b_mod, norm_attn_pre, norm_attn_post, norm_ffn_pre,
              norm_ffn_post, w_in, norm_q_lat, norm_kv_lat, w_q_up, w_kv_up, w_fourier, w_out,
              w_router, b_router, w_gate, b_gate, w_up, b_up, w_down, b_down):
    B, N, D = x.shape
    rope = axial_rope_tables(N)
    for l in range(DEPTH):
        last = l == DEPTH - 1
        m = jax.nn.silu(c) @ w_mod[l] + b_mod[l]
        sh1, sc1, g1, sh2, sc2, g2 = jnp.split(m[:, None, :], N_MOD, axis=-1)
        mc = jax.nn.silu(c_ctx) @ w_mod[l] + b_mod[l]
        csh1, csc1, cg1, csh2, csc2, cg2 = jnp.split(mc, N_MOD, axis=-1)

        hx = modulate(x, norm_attn_pre[l], sh1, sc1)
        hc = modulate(ctx, norm_attn_pre[l], csh1, csc1)
        px = hx @ w_in[l]
        pc = hc @ w_in[l]
        qx, kx, vx = mla_project(px[..., FOURIER_WIDTH:], norm_q_lat[l], norm_kv_lat[l],
                                 w_q_up[l], w_kv_up[l], rope)
        qc, kc, vc = mla_project(pc[..., FOURIER_WIDTH:], norm_q_lat[l], norm_kv_lat[l],
                                 w_q_up[l], w_kv_up[l], None)
        k_all = jnp.concatenate([kx, kc], axis=1)
        v_all = jnp.concatenate([vx, vc], axis=1)
        att_x = attend(qx, k_all, v_all)
        four_x = fourier_mix(px[..., :FOURIER_WIDTH], w_fourier[l])
        mix_x = jnp.concatenate([att_x, four_x], axis=-1) @ w_out[l]
        x = x + g1 * rms_norm(mix_x, norm_attn_post[l])
        if not last:
            att_c = attend(qc, kc, vc)
            four_c = fourier_mix(pc[..., :FOURIER_WIDTH], w_fourier[l])
            mix_c = jnp.concatenate([att_c, four_c], axis=-1) @ w_out[l]
            ctx = ctx + cg1 * rms_norm(mix_c, norm_attn_post[l])

        hx2 = modulate(x, norm_ffn_pre[l], sh2, sc2)
        yx = moe(hx2.reshape(B * N, D), w_router[l], b_router[l], w_gate[l], b_gate[l],
                 w_up[l], b_up[l], w_down[l], b_down[l]).reshape(B, N, D)
        x = x + g2 * rms_norm(yx, norm_ffn_post[l])
        if not last:
            hc2 = modulate(ctx, norm_ffn_pre[l], csh2, csc2)
            yc = moe(hc2.reshape(-1, D), w_router[l], b_router[l], w_gate[l], b_gate[l],
                     w_up[l], b_up[l], w_down[l], b_down[l]).reshape(ctx.shape)
            ctx = ctx + cg2 * rms_norm(yc, norm_ffn_post[l])
    return x
```

```python
import functools
import math

import numpy as np
import jax
import jax.numpy as jnp
from jax import lax
from jax.experimental import pallas as pl
from jax.experimental.pallas import tpu as pltpu

F32 = jnp.float32
BF16 = jnp.bfloat16
I32 = jnp.int32

N_HEADS = 6
QK_NOPE = 128
QK_ROPE = 64
V_DIM = 128
Q_LORA = 384
KV_LORA = 256
F_GROUPS = 4
F_GDIM = 64
F_WIDTH = F_GROUPS * F_GDIM
GRID_W = 64
ROPE_BASE = 10000.0
N_EXPERTS = 32
TOP_K = 4
SWIGLU_LIMIT = 7.0
SWIGLU_ALPHA = 1.702
RMS_EPS = 1e-6
N_MOD = 6

LANES = 128
SUBLANES = 8
HEAD_W = 256
ROPE_W = 128
FFT_N1 = 128
FFT_K1_TILE = 8
VMEM_LIMIT = 56 * 1024 * 1024

LOG2E = 1.4426950408889634


def _cp(sem, vmem=None):
    return pltpu.CompilerParams(dimension_semantics=sem, vmem_limit_bytes=vmem or VMEM_LIMIT)


def _rms(x):
    return x * lax.rsqrt(jnp.mean(x * x, axis=-1, keepdims=True) + RMS_EPS)


def _dot(a, b):
    return jnp.dot(a, b, preferred_element_type=F32)


def _stack_rows(rows, height):
    w = rows[0].shape[1]
    sub = lax.broadcasted_iota(I32, (height, w), 0)
    out = jnp.zeros((height, w), rows[0].dtype)
    for r, v in enumerate(rows):
        out = jnp.where(sub == r, v, out)
    return out


def _dot_nt(a, b):
    return lax.dot_general(a, b, (((1,), (1,)), ((), ())), preferred_element_type=F32)


def _mod_kernel(c_ref, w_ref, b_ref, o_ref):
    c = c_ref[...]
    a = c / (1.0 + jnp.exp(-c))
    a_hi = a.astype(BF16)
    a_lo = (a - a_hi.astype(F32)).astype(BF16)
    w = w_ref[...]
    w_hi = w.astype(BF16)
    w_lo = (w - w_hi.astype(F32)).astype(BF16)
    acc = _dot(a_hi, w_hi) + _dot(a_lo, w_hi) + _dot(a_hi, w_lo)
    o_ref[...] = acc + b_ref[...]


def _modulation(c8, w_mod, b_mod):
    d, n = w_mod.shape
    tn = 1024
    return pl.pallas_call(
        _mod_kernel,
        grid=(n // tn,),
        in_specs=[pl.BlockSpec((SUBLANES, d), lambda j: (0, 0)),
                  pl.BlockSpec((d, tn), lambda j: (0, j)),
                  pl.BlockSpec((1, tn), lambda j: (0, j))],
        out_specs=pl.BlockSpec((SUBLANES, tn), lambda j: (0, j)),
        out_shape=jax.ShapeDtypeStruct((SUBLANES, n), F32),
        name="modulation",
        compiler_params=_cp(("arbitrary",)),
    )(c8, w_mod, b_mod.reshape(1, n))


def _pro_kernel(x_ref, sh_ref, sc_ref, g_ref, win_ref, bdcs_ref, nq_ref, nkv_ref, wqa_ref, wqb_ref,
                wk_ref, wvt_ref, t1_ref, t2_ref, q_ref, k_ref, vt_ref, za_ref, zb_ref, *, scale):
    x = x_ref[0]
    hx = _rms(x) * g_ref[...] * (1.0 + sc_ref[0]) + sh_ref[0]
    px = _dot(hx.astype(BF16), win_ref[...])
    u = px[:, 0:F_WIDTH].astype(BF16)
    z = _dot(u, bdcs_ref[...])
    za_ref[0] = z[:, :F_WIDTH].astype(BF16)
    zb_ref[0] = z[:, F_WIDTH:].astype(BF16)
    o_q = F_WIDTH
    o_kv = o_q + Q_LORA
    o_ra = o_kv + KV_LORA
    o_rb = o_ra + ROPE_W
    t1 = t1_ref[...]
    t2 = t2_ref[...]
    qn = (_rms(px[:, o_q:o_kv]) * nq_ref[...]).astype(BF16)
    qa = _dot(qn, wqa_ref[...])
    qb = _dot(qn, wqb_ref[...])
    for h in range(N_HEADS):
        nope = qa[:, h * HEAD_W:h * HEAD_W + QK_NOPE] * scale
        rope = (qa[:, h * HEAD_W + QK_NOPE:(h + 1) * HEAD_W] * t1
                + qb[:, h * ROPE_W:(h + 1) * ROPE_W] * t2) * scale
        q_ref[0, h, :, 0:QK_NOPE] = nope.astype(BF16)
        q_ref[0, h, :, QK_NOPE:HEAD_W] = rope.astype(BF16)
    kvn = (_rms(px[:, o_kv:o_ra]) * nkv_ref[...]).astype(BF16)
    kn = _dot(kvn, wk_ref[...])
    kr = (px[:, o_ra:o_rb] * t1 + px[:, o_rb:o_rb + ROPE_W] * t2).astype(BF16)
    for h in range(N_HEADS):
        k_ref[0, h, :, 0:QK_NOPE] = kn[:, h * QK_NOPE:(h + 1) * QK_NOPE].astype(BF16)
        k_ref[0, h, :, QK_NOPE:HEAD_W] = kr
    vt = _dot_nt(wvt_ref[...], kvn)
    for h in range(N_HEADS):
        vt_ref[0, h, 0] = vt[h * V_DIM:(h + 1) * V_DIM, :].astype(BF16)


def _prologue(x, m3, mod_row, g_pre, wts, t1, t2, tm, scale):
    b, n, d = x.shape
    nt = n // tm
    win, bdcs, nq, nkv, wqa, wqb, wk, wvt = wts
    full = lambda a: pl.BlockSpec(a.shape, lambda bi, i: (0,) * a.ndim)
    kern = functools.partial(_pro_kernel, scale=scale)
    return pl.pallas_call(
        kern,
        grid=(b, nt),
        in_specs=[pl.BlockSpec((1, tm, d), lambda bi, i: (bi, i, 0)),
                  pl.BlockSpec((1, 1, d), lambda bi, i: (mod_row(bi) * N_MOD + 0, 0, 0)),
                  pl.BlockSpec((1, 1, d), lambda bi, i: (mod_row(bi) * N_MOD + 1, 0, 0)),
                  full(g_pre), full(win), full(bdcs), full(nq), full(nkv), full(wqa), full(wqb),
                  full(wk), full(wvt),
                  pl.BlockSpec((tm, ROPE_W), lambda bi, i: (i, 0)),
                  pl.BlockSpec((tm, ROPE_W), lambda bi, i: (i, 0))],
        out_specs=[pl.BlockSpec((1, N_HEADS, tm, HEAD_W), lambda bi, i: (bi, 0, i, 0)),
                   pl.BlockSpec((1, N_HEADS, tm, HEAD_W), lambda bi, i: (bi, 0, i, 0)),
                   pl.BlockSpec((1, N_HEADS, 1, V_DIM, tm), lambda bi, i: (bi, 0, i, 0, 0)),
                   pl.BlockSpec((1, tm, F_WIDTH), lambda bi, i: (bi, i, 0)),
                   pl.BlockSpec((1, tm, F_WIDTH), lambda bi, i: (bi, i, 0))],
        out_shape=[jax.ShapeDtypeStruct((b, N_HEADS, n, HEAD_W), BF16),
                   jax.ShapeDtypeStruct((b, N_HEADS, n, HEAD_W), BF16),
                   jax.ShapeDtypeStruct((b, N_HEADS, nt, V_DIM, tm), BF16),
                   jax.ShapeDtypeStruct((b, n, F_WIDTH), BF16),
                   jax.ShapeDtypeStruct((b, n, F_WIDTH), BF16)],
        name="prologue",
        compiler_params=_cp(("arbitrary", "arbitrary")),
    )(x, m3, m3, g_pre, win, bdcs, nq, nkv, wqa, wqb, wk, wvt, t1, t2)


def _attn_kernel(q_ref, k_ref, vt_ref, kc_ref, vtc_ref, o_ref, m_sc, l_sc, acc_sc, *, nkv, tk):
    q = q_ref[0, 0]
    s = _dot_nt(kc_ref[0, 0], q)
    m0 = jnp.max(s, axis=0, keepdims=True)
    p = jnp.exp2(s - m0)
    m_sc[...] = m0
    l_sc[...] = jnp.sum(p, axis=0, keepdims=True)
    acc_sc[...] = _dot(vtc_ref[0, 0, 0], p.astype(BF16))

    def body(j, carry):
        start = pl.multiple_of(j * tk, tk)
        s = _dot_nt(k_ref[0, 0, pl.ds(start, tk), :], q)
        m_prev = m_sc[...]
        m_new = jnp.maximum(m_prev, jnp.max(s, axis=0, keepdims=True))
        alpha = jnp.exp2(m_prev - m_new)
        p = jnp.exp2(s - m_new)
        l_sc[...] = alpha * l_sc[...] + jnp.sum(p, axis=0, keepdims=True)
        acc_sc[...] = alpha * acc_sc[...] + _dot(vt_ref[0, 0, j], p.astype(BF16))
        m_sc[...] = m_new
        return carry

    lax.fori_loop(0, nkv, body, 0)
    o = acc_sc[...] / l_sc[...]
    o_ref[0] = o.T.astype(BF16)


def _attention(q, k, vt, kc, vtc, tq):
    b, h, n, _ = q.shape
    nkv, tk = vt.shape[2], vt.shape[4]
    nctx = kc.shape[2]
    kern = functools.partial(_attn_kernel, nkv=nkv, tk=tk)
    return pl.pallas_call(
        kern,
        grid=(b, h, n // tq),
        in_specs=[pl.BlockSpec((1, 1, tq, HEAD_W), lambda bi, hi, i: (bi, hi, i, 0)),
                  pl.BlockSpec((1, 1, n, HEAD_W), lambda bi, hi, i: (bi, hi, 0, 0)),
                  pl.BlockSpec((1, 1, nkv, V_DIM, tk), lambda bi, hi, i: (bi, hi, 0, 0, 0)),
                  pl.BlockSpec((1, 1, nctx, HEAD_W), lambda bi, hi, i: (bi, hi, 0, 0)),
                  pl.BlockSpec((1, 1, 1, V_DIM, nctx), lambda bi, hi, i: (bi, hi, 0, 0, 0))],
        out_specs=pl.BlockSpec((1, tq, V_DIM), lambda bi, hi, i: (bi, i, hi)),
        out_shape=jax.ShapeDtypeStruct((b, n, h * V_DIM), BF16),
        scratch_shapes=[pltpu.VMEM((1, tq), F32), pltpu.VMEM((1, tq), F32),
                        pltpu.VMEM((V_DIM, tq), F32)],
        name="attention",
        compiler_params=_cp(("arbitrary", "arbitrary", "arbitrary")),
    )(q, k, vt, kc, vtc)


def _fft_a_kernel(za_ref, zb_ref, w1_ref, tc_ref, ts_ref, yr_ref, yi_ref):
    ab = jnp.concatenate([za_ref[0], zb_ref[0]], axis=0)
    y = _dot(w1_ref[...], ab)
    re = y[:FFT_N1]
    im = y[FFT_N1:]
    tc = tc_ref[...]
    ts = ts_ref[...]
    yr_ref[0] = (re * tc + im * ts).astype(BF16)
    yi_ref[0] = (im * tc - re * ts).astype(BF16)


def _fft_b_kernel(yr_ref, yi_ref, m_ref, bdw_ref, o_ref):
    ri = jnp.concatenate([yr_ref[0], yi_ref[0]], axis=0)
    f = _dot(m_ref[...], ri)
    four = _dot(f.astype(BF16), bdw_ref[...])
    o_ref[0] = four.reshape(o_ref.shape[1:])


def _fourier(za, zb, w1, tc, ts, mmat, bdw):
    b, n, c = za.shape
    n2 = n // FFT_N1
    wide = n2 * c
    tn = min(2048, wide)
    za2 = za.reshape(b, FFT_N1, wide)
    zb2 = zb.reshape(b, FFT_N1, wide)
    yr, yi = pl.pallas_call(
        _fft_a_kernel,
        grid=(wide // tn, b),
        in_specs=[pl.BlockSpec((1, FFT_N1, tn), lambda i, bi: (bi, 0, i)),
                  pl.BlockSpec((1, FFT_N1, tn), lambda i, bi: (bi, 0, i)),
                  pl.BlockSpec(w1.shape, lambda i, bi: (0, 0)),
                  pl.BlockSpec((FFT_N1, tn), lambda i, bi: (0, i)),
                  pl.BlockSpec((FFT_N1, tn), lambda i, bi: (0, i))],
        out_specs=[pl.BlockSpec((1, FFT_N1, tn), lambda i, bi: (bi, 0, i)),
                   pl.BlockSpec((1, FFT_N1, tn), lambda i, bi: (bi, 0, i))],
        out_shape=[jax.ShapeDtypeStruct((b, FFT_N1, wide), BF16)] * 2,
        name="fourier_a",
        compiler_params=_cp(("arbitrary", "arbitrary")),
    )(za2, zb2, w1, tc, ts)
    rows = FFT_K1_TILE * n2
    yr = yr.reshape(b, n, c)
    yi = yi.reshape(b, n, c)
    out = pl.pallas_call(
        _fft_b_kernel,
        grid=(b, FFT_N1 // FFT_K1_TILE),
        in_specs=[pl.BlockSpec((1, rows, c), lambda bi, i: (bi, i, 0)),
                  pl.BlockSpec((1, rows, c), lambda bi, i: (bi, i, 0)),
                  pl.BlockSpec(mmat.shape, lambda bi, i: (0, 0)),
                  pl.BlockSpec(bdw.shape, lambda bi, i: (0, 0))],
        out_specs=pl.BlockSpec((1, n2, FFT_K1_TILE, c), lambda bi, i: (bi, 0, i, 0)),
        out_shape=jax.ShapeDtypeStruct((b, n2, FFT_N1, c), F32),
        name="fourier_b",
        compiler_params=_cp(("arbitrary", "arbitrary")),
    )(yr, yi, mmat, bdw)
    return out.reshape(b, n, c)


def _post_kernel(att_ref, four_ref, x_ref, g1_ref, sh2_ref, sc2_ref, gpost_ref, gpre_ref, woa_ref, wof_ref,
                 wrt_ref, br_ref, tri_ref, x1_ref, hx_ref, ridx_ref, gates_ref, cnt_ref, carry_sc):
    first = (pl.program_id(0) == 0) & (pl.program_id(1) == 0)

    @pl.when(first)
    def _():
        carry_sc[...] = jnp.zeros_like(carry_sc)

    mix = _dot(att_ref[0], woa_ref[...]) + _dot(four_ref[0].astype(BF16), wof_ref[...])
    x1 = x_ref[0] + g1_ref[0] * (_rms(mix) * gpost_ref[...])
    x1_ref[0] = x1
    hx = _rms(x1) * gpre_ref[...] * (1.0 + sc2_ref[0]) + sh2_ref[0]
    tm = hx.shape[0]
    for c in range(hx.shape[1] // LANES):
        hx_ref[:, c, :] = hx[:, c * LANES:(c + 1) * LANES]
    lg = _dot_nt(wrt_ref[...], hx.astype(BF16)) + br_ref[...]
    iota_e = lax.broadcasted_iota(I32, lg.shape, 0)
    cur = lg
    tops, idxs, ohs = [], [], []
    for _ in range(TOP_K):
        mx = jnp.max(cur, axis=0, keepdims=True)
        idx = jnp.min(jnp.where(cur == mx, iota_e, N_EXPERTS), axis=0, keepdims=True)
        oh = iota_e == idx
        cur = jnp.where(oh, -jnp.inf, cur)
        tops.append(mx)
        idxs.append(idx)
        ohs.append(oh)
    es = [jnp.exp(t - tops[0]) for t in tops]
    den = es[0] + es[1] + es[2] + es[3]
    gates = [e / den for e in es]
    ohsum = ohs[0].astype(F32) + ohs[1].astype(F32) + ohs[2].astype(F32) + ohs[3].astype(F32)
    base = carry_sc[:, 0:1] + _dot(ohsum.astype(BF16), tri_ref[...])
    ranks = [jnp.sum(jnp.where(oh, base, 0.0), axis=0, keepdims=True).astype(I32) for oh in ohs]
    ridx_ref[...] = _stack_rows(idxs + ranks, 2 * TOP_K)
    gates_ref[...] = _stack_rows(gates, LANES).T
    carry_sc[...] = carry_sc[...] + jnp.sum(ohsum, axis=1, keepdims=True)
    cnt_ref[...] = carry_sc[...]


def _post_attention(att, four, x, m3, gpost, gpre, woa, wof, wrt, br, tri, tm):
    b, n, d = x.shape
    nt = n // tm
    t = b * n
    nch = d // LANES
    full = lambda a: pl.BlockSpec(a.shape, lambda bi, i: (0,) * a.ndim)
    mrow = lambda j: pl.BlockSpec((1, 1, d), lambda bi, i: (bi * N_MOD + j, 0, 0))
    return pl.pallas_call(
        _post_kernel,
        grid=(b, nt),
        in_specs=[pl.BlockSpec((1, tm, att.shape[2]), lambda bi, i: (bi, i, 0)),
                  pl.BlockSpec((1, tm, F_WIDTH), lambda bi, i: (bi, i, 0)),
                  pl.BlockSpec((1, tm, d), lambda bi, i: (bi, i, 0)),
                  mrow(2), mrow(3), mrow(4),
                  full(gpost), full(gpre), full(woa), full(wof), full(wrt), full(br), full(tri)],
        out_specs=[pl.BlockSpec((1, tm, d), lambda bi, i: (bi, i, 0)),
                   pl.BlockSpec((tm, nch, LANES), lambda bi, i: (bi * nt + i, 0, 0)),
                   pl.BlockSpec((2 * TOP_K, tm), lambda bi, i: (0, bi * nt + i)),
                   pl.BlockSpec((tm, LANES), lambda bi, i: (bi * nt + i, 0)),
                   pl.BlockSpec((N_EXPERTS, LANES), lambda bi, i: (0, 0))],
        out_shape=[jax.ShapeDtypeStruct((b, n, d), F32),
                   jax.ShapeDtypeStruct((t, nch, LANES), F32),
                   jax.ShapeDtypeStruct((2 * TOP_K, t), I32),
                   jax.ShapeDtypeStruct((t, LANES), F32),
                   jax.ShapeDtypeStruct((N_EXPERTS, LANES), F32)],
        scratch_shapes=[pltpu.VMEM((N_EXPERTS, LANES), F32)],
        name="post_attention",
        compiler_params=_cp(("arbitrary", "arbitrary")),
    )(att, four, x, m3, m3, m3, gpost, gpre, woa, wof, wrt, br, tri)


def _dispatch_kernel(fill_ref, dest_ref, hx_hbm, xs_hbm, zrow, sem, fsem, *, tm, bm):
    i = pl.program_id(0)

    @pl.when(i == 0)
    def _():
        zrow[...] = jnp.zeros_like(zrow)
        nblk = xs_hbm.shape[0] // bm

        def tail(blk, c):
            cp = pltpu.make_async_copy(zrow, xs_hbm.at[pl.ds(blk * bm, bm)], fsem)
            cp.start()
            cp.wait()
            return c

        lax.fori_loop(fill_ref[2 * N_EXPERTS], nblk, tail, 0)

        def per_expert(e, c):
            s = fill_ref[e]
            n = fill_ref[N_EXPERTS + e]

            def start(r, cc):
                pltpu.make_async_copy(zrow.at[0], xs_hbm.at[s + r], fsem).start()
                return cc

            def wait(r, cc):
                pltpu.make_async_copy(zrow.at[0], xs_hbm.at[s + r], fsem).wait()
                return cc

            lax.fori_loop(0, n, start, 0)
            lax.fori_loop(0, n, wait, 0)
            return c

        lax.fori_loop(0, N_EXPERTS, per_expert, 0)

    def start(r, c):
        for k in range(TOP_K):
            d = dest_ref[0, 0, k * tm + r]
            pltpu.make_async_copy(hx_hbm.at[i * tm + r], xs_hbm.at[d], sem).start()
        return c

    def wait(r, c):
        for k in range(TOP_K):
            d = dest_ref[0, 0, k * tm + r]
            pltpu.make_async_copy(hx_hbm.at[i * tm + r], xs_hbm.at[d], sem).wait()
        return c

    lax.fori_loop(0, tm, start, 0)
    lax.fori_loop(0, tm, wait, 0)


def _dispatch(fill, dest_tiles, hx3, n_slots, tm, bm):
    t, nch, _ = hx3.shape
    kern = functools.partial(_dispatch_kernel, tm=tm, bm=bm)
    return pl.pallas_call(
        kern,
        grid_spec=pltpu.PrefetchScalarGridSpec(
            num_scalar_prefetch=1,
            grid=(t // tm,),
            in_specs=[pl.BlockSpec((1, 1, TOP_K * tm), lambda i, f: (i, 0, 0), memory_space=pltpu.SMEM),
                      pl.BlockSpec(memory_space=pl.ANY)],
            out_specs=pl.BlockSpec(memory_space=pl.ANY),
            scratch_shapes=[pltpu.VMEM((bm, nch, LANES), F32),
                            pltpu.SemaphoreType.DMA(()), pltpu.SemaphoreType.DMA(())]),
        out_shape=jax.ShapeDtypeStruct((n_slots, nch, LANES), F32),
        name="dispatch",
        compiler_params=_cp(("arbitrary",)),
    )(fill, dest_tiles, hx3)


def _moe_kernel(be_ref, nv_ref, x_ref, wg_ref, bg_ref, wu_ref, bu_ref, wd_ref, bd_ref, y_ref, wgb, wub, wdb):
    i = pl.program_id(0)
    nch = x_ref.shape[1]

    @pl.when(i < nv_ref[0])
    def _():
        e = be_ref[i]
        prev = be_ref[jnp.maximum(i - 1, 0)]

        @pl.when((i == 0) | (e != prev))
        def _():
            wgb[...] = wg_ref[0].astype(BF16)
            wub[...] = wu_ref[0].astype(BF16)
            wdb[...] = wd_ref[0].astype(BF16)

        xb = jnp.concatenate([x_ref[:, c, :] for c in range(nch)], axis=1).astype(BF16)
        g = jnp.minimum(_dot(xb, wgb[...]) + bg_ref[0], SWIGLU_LIMIT)
        u = jnp.clip(_dot(xb, wub[...]) + bu_ref[0], -SWIGLU_LIMIT, SWIGLU_LIMIT)
        act = g * (1.0 / (1.0 + jnp.exp(-SWIGLU_ALPHA * g))) * (u + 1.0)
        y = _dot(act.astype(BF16), wdb[...]) + bd_ref[0]
        for c in range(nch):
            y_ref[:, c, :] = y[:, c * LANES:(c + 1) * LANES]

    @pl.when(i >= nv_ref[0])
    def _():
        y_ref[...] = jnp.zeros_like(y_ref)


def _experts(block_e, nvalid, xs, wg, bg, wu, bu, wd, bd, bm):
    n_slots, nch, _ = xs.shape
    e, d, f = wg.shape
    nb = n_slots // bm
    xmap = lambda i, be, nv: (jnp.minimum(i, nv[0] - 1), 0, 0)
    wmap = lambda i, be, nv: (be[i], 0, 0)
    return pl.pallas_call(
        _moe_kernel,
        grid_spec=pltpu.PrefetchScalarGridSpec(
            num_scalar_prefetch=2,
            grid=(nb,),
            in_specs=[pl.BlockSpec((bm, nch, LANES), xmap),
                      pl.BlockSpec((1, d, f), wmap), pl.BlockSpec((1, 1, f), wmap),
                      pl.BlockSpec((1, d, f), wmap), pl.BlockSpec((1, 1, f), wmap),
                      pl.BlockSpec((1, f, d), wmap), pl.BlockSpec((1, 1, d), wmap)],
            out_specs=pl.BlockSpec((bm, nch, LANES), lambda i, be, nv: (i, 0, 0)),
            scratch_shapes=[pltpu.VMEM((d, f), BF16), pltpu.VMEM((d, f), BF16), pltpu.VMEM((f, d), BF16)]),
        out_shape=jax.ShapeDtypeStruct((n_slots, nch, LANES), F32),
        name="experts",
        compiler_params=_cp(("arbitrary",)),
    )(block_e, nvalid, xs, wg, bg.reshape(e, 1, f), wu, bu.reshape(e, 1, f), wd, bd.reshape(e, 1, d))


def _combine_kernel(dest_ref, ys_hbm, gates_ref, x1_ref, g2_ref, gpost_ref, o_ref, buf, sem, *, tm):
    def start(r, c):
        for k in range(TOP_K):
            d = dest_ref[0, 0, k * tm + r]
            pltpu.make_async_copy(ys_hbm.at[d], buf.at[k, r], sem).start()
        return c

    def wait(r, c):
        for k in range(TOP_K):
            d = dest_ref[0, 0, k * tm + r]
            pltpu.make_async_copy(ys_hbm.at[d], buf.at[k, r], sem).wait()
        return c

    lax.fori_loop(0, tm, start, 0)
    lax.fori_loop(0, tm, wait, 0)
    gt = gates_ref[...]
    nch = buf.shape[2]
    cols = []
    for c in range(nch):
        acc = gt[:, 0:1] * buf[0, :, c, :]
        for k in range(1, TOP_K):
            acc = acc + gt[:, k:k + 1] * buf[k, :, c, :]
        cols.append(acc)
    y = jnp.concatenate(cols, axis=1)
    o_ref[0] = x1_ref[0] + g2_ref[0] * (_rms(y) * gpost_ref[...])


def _combine(dest_tiles, ys, gates_t, x1, m3, gpost, tm):
    b, n, d = x1.shape
    nt = n // tm
    nch = d // LANES
    kern = functools.partial(_combine_kernel, tm=tm)
    return pl.pallas_call(
        kern,
        grid=(b, nt),
        in_specs=[pl.BlockSpec((1, 1, TOP_K * tm), lambda bi, i: (bi * nt + i, 0, 0), memory_space=pltpu.SMEM),
                  pl.BlockSpec(memory_space=pl.ANY),
                  pl.BlockSpec((tm, LANES), lambda bi, i: (bi * nt + i, 0)),
                  pl.BlockSpec((1, tm, d), lambda bi, i: (bi, i, 0)),
                  pl.BlockSpec((1, 1, d), lambda bi, i: (bi * N_MOD + 5, 0, 0)),
                  pl.BlockSpec(gpost.shape, lambda bi, i: (0, 0))],
        out_specs=pl.BlockSpec((1, tm, d), lambda bi, i: (bi, i, 0)),
        out_shape=jax.ShapeDtypeStruct((b, n, d), F32),
        scratch_shapes=[pltpu.VMEM((TOP_K, tm, nch, LANES), F32), pltpu.SemaphoreType.DMA(())],
        name="combine",
        compiler_params=_cp(("arbitrary", "arbitrary")),
    )(dest_tiles, ys, gates_t, x1, m3, gpost)


def _rope_tables(n):
    rows = n // GRID_W
    row = jnp.repeat(jnp.arange(rows, dtype=F32), GRID_W)
    col = jnp.tile(jnp.arange(GRID_W, dtype=F32), rows)
    half = QK_ROPE // 2
    inv = 1.0 / (ROPE_BASE ** (jnp.arange(0, half, 2, dtype=F32) / half))
    ang = jnp.concatenate([row[:, None] * inv, col[:, None] * inv], axis=-1)
    cos, sin = jnp.cos(ang), jnp.sin(ang)
    zero = jnp.zeros((n, ROPE_W - QK_ROPE), F32)
    return jnp.concatenate([cos, cos, zero], axis=1), jnp.concatenate([sin, sin, zero], axis=1)


def _ctx_tables(n):
    half = QK_ROPE // 2
    t1 = np.zeros((n, ROPE_W), np.float32)
    t1[:, :2 * half] = 1.0
    return jnp.asarray(t1), jnp.zeros((n, ROPE_W), F32)


def _prep_weights(w_in, norm_q, norm_kv, w_q_up, w_kv_up):
    d = w_in.shape[0]
    half = QK_ROPE // 2
    zr = lambda r, c: jnp.zeros((r, c), F32)
    o_rope = F_WIDTH + Q_LORA + KV_LORA
    kr = w_in[:, o_rope:o_rope + QK_ROPE]
    kre, kro = kr[:, 0::2], kr[:, 1::2]
    win = jnp.concatenate([w_in[:, :o_rope],
                           kre, kro, zr(d, ROPE_W - QK_ROPE),
                           -kro, kre, zr(d, ROPE_W - QK_ROPE)], axis=1).astype(BF16)
    hd = QK_NOPE + QK_ROPE
    qa, qb = [], []
    for h in range(N_HEADS):
        wn = w_q_up[:, h * hd:h * hd + QK_NOPE]
        wr = w_q_up[:, h * hd + QK_NOPE:(h + 1) * hd]
        we, wo = wr[:, 0::2], wr[:, 1::2]
        qa += [wn, we, wo, zr(Q_LORA, ROPE_W - QK_ROPE)]
        qb += [-wo, we, zr(Q_LORA, ROPE_W - QK_ROPE)]
    wqa = jnp.concatenate(qa, axis=1).astype(BF16)
    wqb = jnp.concatenate(qb, axis=1).astype(BF16)
    kvd = QK_NOPE + V_DIM
    wk = jnp.concatenate([w_kv_up[:, h * kvd:h * kvd + QK_NOPE] for h in range(N_HEADS)], axis=1).astype(BF16)
    wv = jnp.concatenate([w_kv_up[:, h * kvd + QK_NOPE:(h + 1) * kvd] for h in range(N_HEADS)], axis=1)
    wvt = wv.T.astype(BF16)
    return win, norm_q.reshape(1, -1), norm_kv.reshape(1, -1), wqa, wqb, wk, wvt


def _fourier_consts(n, w_fourier):
    c = F_GDIM
    j = np.arange(c)
    ang = 2.0 * np.pi * np.outer(j, j) / c
    eye = np.eye(F_GROUPS)
    bdc = np.kron(eye, np.cos(ang))
    bds = np.kron(eye, np.sin(ang))
    bdcs = jnp.asarray(np.concatenate([bdc, -bds], axis=1), dtype=BF16)
    n1 = FFT_N1
    n2 = n // n1
    k1 = np.arange(n1)
    a1 = 2.0 * np.pi * np.outer(k1, k1) / n1
    wc, ws = np.cos(a1), np.sin(a1)
    w1 = jnp.asarray(np.block([[wc, ws], [-ws, wc]]), dtype=BF16)
    tw = 2.0 * np.pi * np.outer(k1, np.arange(n2)) / n
    tc = jnp.repeat(jnp.asarray(np.cos(tw), dtype=F32), F_WIDTH, axis=1)
    ts = jnp.repeat(jnp.asarray(np.sin(tw), dtype=F32), F_WIDTH, axis=1)
    k2 = np.arange(n2)
    a2 = 2.0 * np.pi * np.outer(k2, k2) / n2
    norm = 1.0 / math.sqrt(n * c)
    kt = FFT_K1_TILE
    mc = np.zeros((n2, kt, kt, n2))
    ms = np.zeros((n2, kt, kt, n2))
    for r in range(kt):
        mc[:, r, r, :] = np.cos(a2) * norm
        ms[:, r, r, :] = np.sin(a2) * norm
    mmat = np.concatenate([mc.reshape(n2 * kt, kt * n2), ms.reshape(n2 * kt, kt * n2)], axis=1)
    mmat = jnp.asarray(mmat, dtype=BF16)
    bdw = jnp.zeros((F_WIDTH, F_WIDTH), F32)
    for g in range(F_GROUPS):
        bdw = bdw.at[g * c:(g + 1) * c, g * c:(g + 1) * c].set(w_fourier[g])
    return bdcs, w1, tc, ts, mmat, bdw.astype(BF16)


def kernel(x, c, ctx, c_ctx, w_mod, b_mod, norm_attn_pre, norm_attn_post, norm_ffn_pre, norm_ffn_post, w_in, norm_q_lat, norm_kv_lat, w_q_up, w_kv_up, w_fourier, w_out, w_router, b_router, w_gate, b_gate, w_up, b_up, w_down, b_down):
    b, n, d = x.shape
    nctx = ctx.shape[1]
    t = b * n
    assert w_mod.shape[0] == 1 and b + 1 <= SUBLANES and n % (FFT_N1 * SUBLANES) == 0
    row2 = lambda a: a[0].reshape(1, -1)

    c8 = jnp.zeros((SUBLANES, d), F32).at[:b].set(c).at[b].set(c_ctx)
    m3 = _modulation(c8, w_mod[0], b_mod[0]).reshape(SUBLANES * N_MOD, 1, d)

    wts = _prep_weights(w_in[0], norm_q_lat[0], norm_kv_lat[0], w_q_up[0], w_kv_up[0])
    bdcs, w1, tc, ts, mmat, bdw = _fourier_consts(n, w_fourier[0])
    pro_w = (wts[0], bdcs) + wts[1:]
    scale = (QK_NOPE + QK_ROPE) ** -0.5 * LOG2E
    g_pre = row2(norm_attn_pre)

    tm = min(512, n)
    t1x, t2x = _rope_tables(n)
    q, k, vt, za, zb = _prologue(x, m3, lambda bi: bi, g_pre, pro_w, t1x, t2x, tm, scale)
    t1c, t2c = _ctx_tables(nctx)
    _, kc, vtc, _, _ = _prologue(ctx, m3, lambda bi: b, g_pre, pro_w, t1c, t2c, nctx, scale)

    att = _attention(q, k, vt, kc, vtc, min(1024, n))
    four = _fourier(za, zb, w1, tc, ts, mmat, bdw)

    tmp = min(512, n)
    tri = jnp.asarray(np.triu(np.ones((tmp, tmp), np.float32), 1), dtype=BF16)
    width = N_HEADS * V_DIM
    woa = w_out[0, :width].astype(BF16)
    wof = w_out[0, width:].astype(BF16)
    x1, hx3, ridx, gates_t, cnt = _post_attention(
        att, four, x, m3, row2(norm_attn_post), row2(norm_ffn_pre), woa, wof,
        w_router[0].T.astype(BF16), b_router[0].reshape(-1, 1), tri, tmp)

    bm = 256
    counts = cnt[:, 0].astype(I32)
    padded = (counts + bm - 1) // bm * bm
    pad_end = jnp.cumsum(padded)
    pad_start = pad_end - padded
    nb = t * TOP_K // bm + N_EXPERTS
    block_e = jnp.minimum(jnp.searchsorted(pad_end, jnp.arange(nb, dtype=I32) * bm, side='right'),
                          N_EXPERTS - 1).astype(I32)
    nvalid = (pad_end[-1:] // bm).astype(I32)
    dest = pad_start[ridx[:TOP_K]] + ridx[TOP_K:]
    fill = jnp.concatenate([pad_start + counts, padded - counts, nvalid]).astype(I32)

    def tiles(tile):
        return dest.reshape(TOP_K, t // tile, tile).transpose(1, 0, 2).reshape(t // tile, 1, TOP_K * tile)

    tmd = min(512, n)
    xs = _dispatch(fill, tiles(tmd), hx3, nb * bm, tmd, bm)
    ys = _experts(block_e, nvalid, xs, w_gate[0], b_gate[0], w_up[0], b_up[0], w_down[0], b_down[0], bm)
    tmc = 128
    return _combine(tiles(tmc), ys, gates_t, x1, m3, row2(norm_ffn_post), tmc)
```

```python
import functools
import math

import numpy as np
import jax
import jax.numpy as jnp
from jax import lax
from jax.experimental import pallas as pl
from jax.experimental.pallas import tpu as pltpu

F32 = jnp.float32
BF16 = jnp.bfloat16
I32 = jnp.int32

N_HEADS = 6
QK_NOPE = 128
QK_ROPE = 64
V_DIM = 128
Q_LORA = 384
KV_LORA = 256
F_GROUPS = 4
F_GDIM = 64
F_WIDTH = F_GROUPS * F_GDIM
GRID_W = 64
ROPE_BASE = 10000.0
N_EXPERTS = 32
TOP_K = 4
SWIGLU_LIMIT = 7.0
SWIGLU_ALPHA = 1.702
RMS_EPS = 1e-6
N_MOD = 6

LANES = 128
SUBLANES = 8
HEAD_W = 256
ROPE_W = 128
FFT_N1 = 128
FFT_K1_TILE = 8
VMEM_LIMIT = 56 * 1024 * 1024

LOG2E = 1.4426950408889634


def _cp(sem, vmem=None):
    return pltpu.CompilerParams(dimension_semantics=sem, vmem_limit_bytes=vmem or VMEM_LIMIT)


def _rms(x):
    return x * lax.rsqrt(jnp.mean(x * x, axis=-1, keepdims=True) + RMS_EPS)


def _dot(a, b):
    return jnp.dot(a, b, preferred_element_type=F32)


def _stack_rows(rows, height):
    w = rows[0].shape[1]
    sub = lax.broadcasted_iota(I32, (height, w), 0)
    out = jnp.zeros((height, w), rows[0].dtype)
    for r, v in enumerate(rows):
        out = jnp.where(sub == r, v, out)
    return out


def _dot_nt(a, b):
    return lax.dot_general(a, b, (((1,), (1,)), ((), ())), preferred_element_type=F32)


def _mod_kernel(c_ref, w_ref, b_ref, o_ref):
    c = c_ref[...]
    a = c / (1.0 + jnp.exp(-c))
    a_hi = a.astype(BF16)
    a_lo = (a - a_hi.astype(F32)).astype(BF16)
    w = w_ref[...]
    w_hi = w.astype(BF16)
    w_lo = (w - w_hi.astype(F32)).astype(BF16)
    acc = _dot(a_hi, w_hi) + _dot(a_lo, w_hi) + _dot(a_hi, w_lo)
    o_ref[...] = acc + b_ref[...]


def _modulation(c8, w_mod, b_mod):
    d, n = w_mod.shape
    tn = 1024
    return pl.pallas_call(
        _mod_kernel,
        grid=(n // tn,),
        in_specs=[pl.BlockSpec((SUBLANES, d), lambda j: (0, 0)),
                  pl.BlockSpec((d, tn), lambda j: (0, j)),
                  pl.BlockSpec((1, tn), lambda j: (0, j))],
        out_specs=pl.BlockSpec((SUBLANES, tn), lambda j: (0, j)),
        out_shape=jax.ShapeDtypeStruct((SUBLANES, n), F32),
        name="modulation",
        compiler_params=_cp(("arbitrary",)),
    )(c8, w_mod, b_mod.reshape(1, n))


def _pro_kernel(x_ref, sh_ref, sc_ref, g_ref, win_ref, bdcs_ref, nq_ref, nkv_ref, wqa_ref, wqb_ref,
                wk_ref, wvt_ref, t1_ref, t2_ref, q_ref, k_ref, vt_ref, za_ref, zb_ref, *, scale):
    x = x_ref[0]
    hx = _rms(x) * g_ref[...] * (1.0 + sc_ref[0]) + sh_ref[0]
    px = _dot(hx.astype(BF16), win_ref[...])
    u = px[:, 0:F_WIDTH].astype(BF16)
    z = _dot(u, bdcs_ref[...])
    za_ref[0] = z[:, :F_WIDTH].astype(BF16)
    zb_ref[0] = z[:, F_WIDTH:].astype(BF16)
    o_q = F_WIDTH
    o_kv = o_q + Q_LORA
    o_ra = o_kv + KV_LORA
    o_rb = o_ra + ROPE_W
    t1 = t1_ref[...]
    t2 = t2_ref[...]
    qn = (_rms(px[:, o_q:o_kv]) * nq_ref[...]).astype(BF16)
    qa = _dot(qn, wqa_ref[...])
    qb = _dot(qn, wqb_ref[...])
    for h in range(N_HEADS):
        nope = qa[:, h * HEAD_W:h * HEAD_W + QK_NOPE] * scale
        rope = (qa[:, h * HEAD_W + QK_NOPE:(h + 1) * HEAD_W] * t1
                + qb[:, h * ROPE_W:(h + 1) * ROPE_W] * t2) * scale
        q_ref[0, h, :, 0:QK_NOPE] = nope.astype(BF16)
        q_ref[0, h, :, QK_NOPE:HEAD_W] = rope.astype(BF16)
    kvn = (_rms(px[:, o_kv:o_ra]) * nkv_ref[...]).astype(BF16)
    kn = _dot(kvn, wk_ref[...])
    kr = (px[:, o_ra:o_rb] * t1 + px[:, o_rb:o_rb + ROPE_W] * t2).astype(BF16)
    for h in range(N_HEADS):
        k_ref[0, h, :, 0:QK_NOPE] = kn[:, h * QK_NOPE:(h + 1) * QK_NOPE].astype(BF16)
        k_ref[0, h, :, QK_NOPE:HEAD_W] = kr
    vt = _dot_nt(wvt_ref[...], kvn)
    for h in range(N_HEADS):
        vt_ref[0, h, 0] = vt[h * V_DIM:(h + 1) * V_DIM, :].astype(BF16)


def _prologue(x, m3, mod_row, g_pre, wts, t1, t2, tm, scale):
    b, n, d = x.shape
    nt = n // tm
    win, bdcs, nq, nkv, wqa, wqb, wk, wvt = wts
    full = lambda a: pl.BlockSpec(a.shape, lambda bi, i: (0,) * a.ndim)
    kern = functools.partial(_pro_kernel, scale=scale)
    return pl.pallas_call(
        kern,
        grid=(b, nt),
        in_specs=[pl.BlockSpec((1, tm, d), lambda bi, i: (bi, i, 0)),
                  pl.BlockSpec((1, 1, d), lambda bi, i: (mod_row(bi) * N_MOD + 0, 0, 0)),
                  pl.BlockSpec((1, 1, d), lambda bi, i: (mod_row(bi) * N_MOD + 1, 0, 0)),
                  full(g_pre), full(win), full(bdcs), full(nq), full(nkv), full(wqa), full(wqb),
                  full(wk), full(wvt),
                  pl.BlockSpec((tm, ROPE_W), lambda bi, i: (i, 0)),
                  pl.BlockSpec((tm, ROPE_W), lambda bi, i: (i, 0))],
        out_specs=[pl.BlockSpec((1, N_HEADS, tm, HEAD_W), lambda bi, i: (bi, 0, i, 0)),
                   pl.BlockSpec((1, N_HEADS, tm, HEAD_W), lambda bi, i: (bi, 0, i, 0)),
                   pl.BlockSpec((1, N_HEADS, 1, V_DIM, tm), lambda bi, i: (bi, 0, i, 0, 0)),
                   pl.BlockSpec((1, tm, F_WIDTH), lambda bi, i: (bi, i, 0)),
                   pl.BlockSpec((1, tm, F_WIDTH), lambda bi, i: (bi, i, 0))],
        out_shape=[jax.ShapeDtypeStruct((b, N_HEADS, n, HEAD_W), BF16),
                   jax.ShapeDtypeStruct((b, N_HEADS, n, HEAD_W), BF16),
                   jax.ShapeDtypeStruct((b, N_HEADS, nt, V_DIM, tm), BF16),
                   jax.ShapeDtypeStruct((b, n, F_WIDTH), BF16),
                   jax.ShapeDtypeStruct((b, n, F_WIDTH), BF16)],
        name="prologue",
        compiler_params=_cp(("arbitrary", "arbitrary")),
    )(x, m3, m3, g_pre, win, bdcs, nq, nkv, wqa, wqb, wk, wvt, t1, t2)


def _attn_kernel(q_ref, k_ref, vt_ref, kc_ref, vtc_ref, o_ref, sa_sc, sb_sc, ca_sc, cb_sc, m_sc, l_sc, acc_sc,
                 *, nkv, tk):
    q = q_ref[0, 0]
    s = _dot_nt(kc_ref[0, 0], q)
    m0 = jnp.max(s, axis=0, keepdims=True)
    p = jnp.exp2(s - m0)
    m_sc[...] = m0
    l_sc[...] = jnp.sum(p, axis=0, keepdims=True)
    acc_sc[...] = _dot(vtc_ref[0, 0, 0], p.astype(BF16))

    def scores(j, s_dst, c_dst):
        start = pl.multiple_of(j * tk, tk)
        s = _dot_nt(k_ref[0, 0, pl.ds(start, tk), :], q)
        s_dst[...] = s
        c_dst[...] = jnp.max(s, axis=0, keepdims=True)

    def accumulate(j, s_src, c_src):
        m_prev = m_sc[...]
        m_new = jnp.maximum(m_prev, c_src[...])
        alpha = jnp.exp2(m_prev - m_new)
        p = jnp.exp2(s_src[...] - m_new)
        l_sc[...] = alpha * l_sc[...] + jnp.sum(p, axis=0, keepdims=True)
        acc_sc[...] = alpha * acc_sc[...] + _dot(vt_ref[0, 0, j], p.astype(BF16))
        m_sc[...] = m_new

    scores(0, sa_sc, ca_sc)

    def body(i, carry):
        j = 2 * i
        scores(j + 1, sb_sc, cb_sc)
        accumulate(j, sa_sc, ca_sc)
        scores(j + 2, sa_sc, ca_sc)
        accumulate(j + 1, sb_sc, cb_sc)
        return carry

    lax.fori_loop(0, nkv // 2 - 1, body, 0)
    scores(nkv - 1, sb_sc, cb_sc)
    accumulate(nkv - 2, sa_sc, ca_sc)
    accumulate(nkv - 1, sb_sc, cb_sc)
    o = acc_sc[...] / l_sc[...]
    o_ref[0] = o.T.astype(BF16)


def _attention(q, k, vt, kc, vtc, tq):
    b, h, n, _ = q.shape
    nkv, tk = vt.shape[2], vt.shape[4]
    nctx = kc.shape[2]
    assert nkv % 2 == 0 and nkv >= 2, "the score pipeline handles key chunks in pairs"
    kern = functools.partial(_attn_kernel, nkv=nkv, tk=tk)
    return pl.pallas_call(
        kern,
        grid=(b, h, n // tq),
        in_specs=[pl.BlockSpec((1, 1, tq, HEAD_W), lambda bi, hi, i: (bi, hi, i, 0)),
                  pl.BlockSpec((1, 1, n, HEAD_W), lambda bi, hi, i: (bi, hi, 0, 0)),
                  pl.BlockSpec((1, 1, nkv, V_DIM, tk), lambda bi, hi, i: (bi, hi, 0, 0, 0)),
                  pl.BlockSpec((1, 1, nctx, HEAD_W), lambda bi, hi, i: (bi, hi, 0, 0)),
                  pl.BlockSpec((1, 1, 1, V_DIM, nctx), lambda bi, hi, i: (bi, hi, 0, 0, 0))],
        out_specs=pl.BlockSpec((1, tq, V_DIM), lambda bi, hi, i: (bi, i, hi)),
        out_shape=jax.ShapeDtypeStruct((b, n, h * V_DIM), BF16),
        scratch_shapes=[pltpu.VMEM((tk, tq), F32), pltpu.VMEM((tk, tq), F32),
                        pltpu.VMEM((1, tq), F32), pltpu.VMEM((1, tq), F32),
                        pltpu.VMEM((1, tq), F32), pltpu.VMEM((1, tq), F32),
                        pltpu.VMEM((V_DIM, tq), F32)],
        name="attention",
        compiler_params=_cp(("arbitrary", "arbitrary", "arbitrary")),
    )(q, k, vt, kc, vtc)


def _fft_a_kernel(za_ref, zb_ref, w1_ref, tc_ref, ts_ref, yr_ref, yi_ref):
    ab = jnp.concatenate([za_ref[0], zb_ref[0]], axis=0)
    y = _dot(w1_ref[...], ab)
    re = y[:FFT_N1]
    im = y[FFT_N1:]
    tc = tc_ref[...]
    ts = ts_ref[...]
    yr_ref[0] = (re * tc + im * ts).astype(BF16)
    yi_ref[0] = (im * tc - re * ts).astype(BF16)


def _fft_b_kernel(yr_ref, yi_ref, m_ref, bdw_ref, o_ref):
    ri = jnp.concatenate([yr_ref[0], yi_ref[0]], axis=0)
    f = _dot(m_ref[...], ri)
    four = _dot(f.astype(BF16), bdw_ref[...])
    o_ref[0] = four.reshape(o_ref.shape[1:])


def _fourier(za, zb, w1, tc, ts, mmat, bdw):
    b, n, c = za.shape
    n2 = n // FFT_N1
    wide = n2 * c
    tn = min(2048, wide)
    za2 = za.reshape(b, FFT_N1, wide)
    zb2 = zb.reshape(b, FFT_N1, wide)
    yr, yi = pl.pallas_call(
        _fft_a_kernel,
        grid=(wide // tn, b),
        in_specs=[pl.BlockSpec((1, FFT_N1, tn), lambda i, bi: (bi, 0, i)),
                  pl.BlockSpec((1, FFT_N1, tn), lambda i, bi: (bi, 0, i)),
                  pl.BlockSpec(w1.shape, lambda i, bi: (0, 0)),
                  pl.BlockSpec((FFT_N1, tn), lambda i, bi: (0, i)),
                  pl.BlockSpec((FFT_N1, tn), lambda i, bi: (0, i))],
        out_specs=[pl.BlockSpec((1, FFT_N1, tn), lambda i, bi: (bi, 0, i)),
                   pl.BlockSpec((1, FFT_N1, tn), lambda i, bi: (bi, 0, i))],
        out_shape=[jax.ShapeDtypeStruct((b, FFT_N1, wide), BF16)] * 2,
        name="fourier_a",
        compiler_params=_cp(("arbitrary", "arbitrary")),
    )(za2, zb2, w1, tc, ts)
    rows = FFT_K1_TILE * n2
    yr = yr.reshape(b, n, c)
    yi = yi.reshape(b, n, c)
    out = pl.pallas_call(
        _fft_b_kernel,
        grid=(b, FFT_N1 // FFT_K1_TILE),
        in_specs=[pl.BlockSpec((1, rows, c), lambda bi, i: (bi, i, 0)),
                  pl.BlockSpec((1, rows, c), lambda bi, i: (bi, i, 0)),
                  pl.BlockSpec(mmat.shape, lambda bi, i: (0, 0)),
                  pl.BlockSpec(bdw.shape, lambda bi, i: (0, 0))],
        out_specs=pl.BlockSpec((1, n2, FFT_K1_TILE, c), lambda bi, i: (bi, 0, i, 0)),
        out_shape=jax.ShapeDtypeStruct((b, n2, FFT_N1, c), F32),
        name="fourier_b",
        compiler_params=_cp(("arbitrary", "arbitrary")),
    )(yr, yi, mmat, bdw)
    return out.reshape(b, n, c)


def _post_kernel(att_ref, four_ref, x_ref, g1_ref, sh2_ref, sc2_ref, gpost_ref, gpre_ref, woa_ref, wof_ref,
                 wrt_ref, br_ref, tri_ref, x1_ref, hx_ref, ridx_ref, gates_ref, cnt_ref, carry_sc):
    first = (pl.program_id(0) == 0) & (pl.program_id(1) == 0)

    @pl.when(first)
    def _():
        carry_sc[...] = jnp.zeros_like(carry_sc)

    mix = _dot(att_ref[0], woa_ref[...]) + _dot(four_ref[0].astype(BF16), wof_ref[...])
    x1 = x_ref[0] + g1_ref[0] * (_rms(mix) * gpost_ref[...])
    x1_ref[0] = x1
    hx = _rms(x1) * gpre_ref[...] * (1.0 + sc2_ref[0]) + sh2_ref[0]
    tm = hx.shape[0]
    for c in range(hx.shape[1] // LANES):
        hx_ref[:, c, :] = hx[:, c * LANES:(c + 1) * LANES]
    lg = _dot_nt(wrt_ref[...], hx.astype(BF16)) + br_ref[...]
    iota_e = lax.broadcasted_iota(I32, lg.shape, 0)
    cur = lg
    tops, idxs, ohs = [], [], []
    for _ in range(TOP_K):
        mx = jnp.max(cur, axis=0, keepdims=True)
        idx = jnp.min(jnp.where(cur == mx, iota_e, N_EXPERTS), axis=0, keepdims=True)
        oh = iota_e == idx
        cur = jnp.where(oh, -jnp.inf, cur)
        tops.append(mx)
        idxs.append(idx)
        ohs.append(oh)
    es = [jnp.exp(t - tops[0]) for t in tops]
    den = es[0] + es[1] + es[2] + es[3]
    gates = [e / den for e in es]
    ohsum = ohs[0].astype(F32) + ohs[1].astype(F32) + ohs[2].astype(F32) + ohs[3].astype(F32)
    base = carry_sc[:, 0:1] + _dot(ohsum.astype(BF16), tri_ref[...])
    ranks = [jnp.sum(jnp.where(oh, base, 0.0), axis=0, keepdims=True).astype(I32) for oh in ohs]
    ridx_ref[...] = _stack_rows(idxs + ranks, 2 * TOP_K)
    gates_ref[...] = _stack_rows(gates, LANES).T
    carry_sc[...] = carry_sc[...] + jnp.sum(ohsum, axis=1, keepdims=True)
    cnt_ref[...] = carry_sc[...]


def _post_attention(att, four, x, m3, gpost, gpre, woa, wof, wrt, br, tri, tm):
    b, n, d = x.shape
    nt = n // tm
    t = b * n
    nch = d // LANES
    full = lambda a: pl.BlockSpec(a.shape, lambda bi, i: (0,) * a.ndim)
    mrow = lambda j: pl.BlockSpec((1, 1, d), lambda bi, i: (bi * N_MOD + j, 0, 0))
    return pl.pallas_call(
        _post_kernel,
        grid=(b, nt),
        in_specs=[pl.BlockSpec((1, tm, att.shape[2]), lambda bi, i: (bi, i, 0)),
                  pl.BlockSpec((1, tm, F_WIDTH), lambda bi, i: (bi, i, 0)),
                  pl.BlockSpec((1, tm, d), lambda bi, i: (bi, i, 0)),
                  mrow(2), mrow(3), mrow(4),
                  full(gpost), full(gpre), full(woa), full(wof), full(wrt), full(br), full(tri)],
        out_specs=[pl.BlockSpec((1, tm, d), lambda bi, i: (bi, i, 0)),
                   pl.BlockSpec((tm, nch, LANES), lambda bi, i: (bi * nt + i, 0, 0)),
                   pl.BlockSpec((2 * TOP_K, tm), lambda bi, i: (0, bi * nt + i)),
                   pl.BlockSpec((tm, LANES), lambda bi, i: (bi * nt + i, 0)),
                   pl.BlockSpec((N_EXPERTS, LANES), lambda bi, i: (0, 0))],
        out_shape=[jax.ShapeDtypeStruct((b, n, d), F32),
                   jax.ShapeDtypeStruct((t, nch, LANES), F32),
                   jax.ShapeDtypeStruct((2 * TOP_K, t), I32),
                   jax.ShapeDtypeStruct((t, LANES), F32),
                   jax.ShapeDtypeStruct((N_EXPERTS, LANES), F32)],
        scratch_shapes=[pltpu.VMEM((N_EXPERTS, LANES), F32)],
        name="post_attention",
        compiler_params=_cp(("arbitrary", "arbitrary")),
    )(att, four, x, m3, m3, m3, gpost, gpre, woa, wof, wrt, br, tri)


def _dispatch_kernel(fill_ref, dest_ref, hx_ref, xs_hbm, zrow, sem, fsem, *, tm, bm):
    i = pl.program_id(0)

    @pl.when(i == 0)
    def _():
        zrow[...] = jnp.zeros_like(zrow)
        nblk = xs_hbm.shape[0] // bm

        def tail(blk, c):
            cp = pltpu.make_async_copy(zrow, xs_hbm.at[pl.ds(blk * bm, bm)], fsem)
            cp.start()
            cp.wait()
            return c

        lax.fori_loop(fill_ref[2 * N_EXPERTS], nblk, tail, 0)

        def per_expert(e, c):
            s = fill_ref[e]
            n = fill_ref[N_EXPERTS + e]

            def start(r, cc):
                pltpu.make_async_copy(zrow.at[0], xs_hbm.at[s + r], fsem).start()
                return cc

            def wait(r, cc):
                pltpu.make_async_copy(zrow.at[0], xs_hbm.at[s + r], fsem).wait()
                return cc

            lax.fori_loop(0, n, start, 0)
            lax.fori_loop(0, n, wait, 0)
            return c

        lax.fori_loop(0, N_EXPERTS, per_expert, 0)

    def start(r, c):
        for k in range(TOP_K):
            d = dest_ref[0, 0, k * tm + r]
            pltpu.make_async_copy(hx_ref.at[r], xs_hbm.at[d], sem).start()
        return c

    def wait(r, c):
        for k in range(TOP_K):
            d = dest_ref[0, 0, k * tm + r]
            pltpu.make_async_copy(hx_ref.at[r], xs_hbm.at[d], sem).wait()
        return c

    lax.fori_loop(0, tm, start, 0, unroll=4)
    lax.fori_loop(0, tm, wait, 0, unroll=4)


def _dispatch(fill, dest_tiles, hx3, n_slots, tm, bm):
    t, nch, _ = hx3.shape
    kern = functools.partial(_dispatch_kernel, tm=tm, bm=bm)
    return pl.pallas_call(
        kern,
        grid_spec=pltpu.PrefetchScalarGridSpec(
            num_scalar_prefetch=1,
            grid=(t // tm,),
            in_specs=[pl.BlockSpec((1, 1, TOP_K * tm), lambda i, f: (i, 0, 0), memory_space=pltpu.SMEM),
                      pl.BlockSpec((tm, nch, LANES), lambda i, f: (i, 0, 0))],
            out_specs=pl.BlockSpec(memory_space=pl.ANY),
            scratch_shapes=[pltpu.VMEM((bm, nch, LANES), F32),
                            pltpu.SemaphoreType.DMA(()), pltpu.SemaphoreType.DMA(())]),
        out_shape=jax.ShapeDtypeStruct((n_slots, nch, LANES), F32),
        name="dispatch",
        compiler_params=_cp(("arbitrary",)),
    )(fill, dest_tiles, hx3)


def _moe_kernel(be_ref, nv_ref, x_ref, wg_ref, bg_ref, wu_ref, bu_ref, wd_ref, bd_ref, y_ref, wgb, wub, wdb):
    i = pl.program_id(0)
    nch = x_ref.shape[1]

    @pl.when(i < nv_ref[0])
    def _():
        e = be_ref[i]
        prev = be_ref[jnp.maximum(i - 1, 0)]

        @pl.when((i == 0) | (e != prev))
        def _():
            wgb[...] = wg_ref[0].astype(BF16)
            wub[...] = wu_ref[0].astype(BF16)
            wdb[...] = wd_ref[0].astype(BF16)

        xb = jnp.concatenate([x_ref[:, c, :] for c in range(nch)], axis=1).astype(BF16)
        g = jnp.minimum(_dot(xb, wgb[...]) + bg_ref[0], SWIGLU_LIMIT)
        u = jnp.clip(_dot(xb, wub[...]) + bu_ref[0], -SWIGLU_LIMIT, SWIGLU_LIMIT)
        act = g * (1.0 / (1.0 + jnp.exp(-SWIGLU_ALPHA * g))) * (u + 1.0)
        y = _dot(act.astype(BF16), wdb[...]) + bd_ref[0]
        for c in range(nch):
            y_ref[:, c, :] = y[:, c * LANES:(c + 1) * LANES]

    @pl.when(i >= nv_ref[0])
    def _():
        y_ref[...] = jnp.zeros_like(y_ref)


def _experts(block_e, nvalid, xs, wg, bg, wu, bu, wd, bd, bm):
    n_slots, nch, _ = xs.shape
    e, d, f = wg.shape
    nb = n_slots // bm
    xmap = lambda i, be, nv: (jnp.minimum(i, nv[0] - 1), 0, 0)
    wmap = lambda i, be, nv: (be[i], 0, 0)
    return pl.pallas_call(
        _moe_kernel,
        grid_spec=pltpu.PrefetchScalarGridSpec(
            num_scalar_prefetch=2,
            grid=(nb,),
            in_specs=[pl.BlockSpec((bm, nch, LANES), xmap),
                      pl.BlockSpec((1, d, f), wmap), pl.BlockSpec((1, 1, f), wmap),
                      pl.BlockSpec((1, d, f), wmap), pl.BlockSpec((1, 1, f), wmap),
                      pl.BlockSpec((1, f, d), wmap), pl.BlockSpec((1, 1, d), wmap)],
            out_specs=pl.BlockSpec((bm, nch, LANES), lambda i, be, nv: (i, 0, 0)),
            scratch_shapes=[pltpu.VMEM((d, f), BF16), pltpu.VMEM((d, f), BF16), pltpu.VMEM((f, d), BF16)]),
        out_shape=jax.ShapeDtypeStruct((n_slots, nch, LANES), F32),
        name="experts",
        compiler_params=_cp(("arbitrary",)),
    )(block_e, nvalid, xs, wg, bg.reshape(e, 1, f), wu, bu.reshape(e, 1, f), wd, bd.reshape(e, 1, d))


def _combine_kernel(dest_ref, ys_hbm, gates_ref, x1_ref, g2_ref, gpost_ref, o_ref, buf, sem, *, tm):
    def start(r, c):
        for k in range(TOP_K):
            d = dest_ref[0, 0, k * tm + r]
            pltpu.make_async_copy(ys_hbm.at[d], buf.at[k, r], sem).start()
        return c

    def wait(r, c):
        for k in range(TOP_K):
            d = dest_ref[0, 0, k * tm + r]
            pltpu.make_async_copy(ys_hbm.at[d], buf.at[k, r], sem).wait()
        return c

    lax.fori_loop(0, tm, start, 0)
    lax.fori_loop(0, tm, wait, 0)
    gt = gates_ref[...]
    nch = buf.shape[2]
    cols = []
    for c in range(nch):
        acc = gt[:, 0:1] * buf[0, :, c, :]
        for k in range(1, TOP_K):
            acc = acc + gt[:, k:k + 1] * buf[k, :, c, :]
        cols.append(acc)
    y = jnp.concatenate(cols, axis=1)
    o_ref[0] = x1_ref[0] + g2_ref[0] * (_rms(y) * gpost_ref[...])


def _combine(dest_tiles, ys, gates_t, x1, m3, gpost, tm):
    b, n, d = x1.shape
    nt = n // tm
    nch = d // LANES
    kern = functools.partial(_combine_kernel, tm=tm)
    return pl.pallas_call(
        kern,
        grid=(b, nt),
        in_specs=[pl.BlockSpec((1, 1, TOP_K * tm), lambda bi, i: (bi * nt + i, 0, 0), memory_space=pltpu.SMEM),
                  pl.BlockSpec(memory_space=pl.ANY),
                  pl.BlockSpec((tm, LANES), lambda bi, i: (bi * nt + i, 0)),
                  pl.BlockSpec((1, tm, d), lambda bi, i: (bi, i, 0)),
                  pl.BlockSpec((1, 1, d), lambda bi, i: (bi * N_MOD + 5, 0, 0)),
                  pl.BlockSpec(gpost.shape, lambda bi, i: (0, 0))],
        out_specs=pl.BlockSpec((1, tm, d), lambda bi, i: (bi, i, 0)),
        out_shape=jax.ShapeDtypeStruct((b, n, d), F32),
        scratch_shapes=[pltpu.VMEM((TOP_K, tm, nch, LANES), F32), pltpu.SemaphoreType.DMA(())],
        name="combine",
        compiler_params=_cp(("arbitrary", "arbitrary")),
    )(dest_tiles, ys, gates_t, x1, m3, gpost)


def _rope_tables(n):
    rows = n // GRID_W
    row = jnp.repeat(jnp.arange(rows, dtype=F32), GRID_W)
    col = jnp.tile(jnp.arange(GRID_W, dtype=F32), rows)
    half = QK_ROPE // 2
    inv = 1.0 / (ROPE_BASE ** (jnp.arange(0, half, 2, dtype=F32) / half))
    ang = jnp.concatenate([row[:, None] * inv, col[:, None] * inv], axis=-1)
    cos, sin = jnp.cos(ang), jnp.sin(ang)
    zero = jnp.zeros((n, ROPE_W - QK_ROPE), F32)
    return jnp.concatenate([cos, cos, zero], axis=1), jnp.concatenate([sin, sin, zero], axis=1)


def _ctx_tables(n):
    half = QK_ROPE // 2
    t1 = np.zeros((n, ROPE_W), np.float32)
    t1[:, :2 * half] = 1.0
    return jnp.asarray(t1), jnp.zeros((n, ROPE_W), F32)


def _prep_weights(w_in, norm_q, norm_kv, w_q_up, w_kv_up):
    d = w_in.shape[0]
    half = QK_ROPE // 2
    zr = lambda r, c: jnp.zeros((r, c), F32)
    o_rope = F_WIDTH + Q_LORA + KV_LORA
    kr = w_in[:, o_rope:o_rope + QK_ROPE]
    kre, kro = kr[:, 0::2], kr[:, 1::2]
    win = jnp.concatenate([w_in[:, :o_rope],
                           kre, kro, zr(d, ROPE_W - QK_ROPE),
                           -kro, kre, zr(d, ROPE_W - QK_ROPE)], axis=1).astype(BF16)
    hd = QK_NOPE + QK_ROPE
    qa, qb = [], []
    for h in range(N_HEADS):
        wn = w_q_up[:, h * hd:h * hd + QK_NOPE]
        wr = w_q_up[:, h * hd + QK_NOPE:(h + 1) * hd]
        we, wo = wr[:, 0::2], wr[:, 1::2]
        qa += [wn, we, wo, zr(Q_LORA, ROPE_W - QK_ROPE)]
        qb += [-wo, we, zr(Q_LORA, ROPE_W - QK_ROPE)]
    wqa = jnp.concatenate(qa, axis=1).astype(BF16)
    wqb = jnp.concatenate(qb, axis=1).astype(BF16)
    kvd = QK_NOPE + V_DIM
    wk = jnp.concatenate([w_kv_up[:, h * kvd:h * kvd + QK_NOPE] for h in range(N_HEADS)], axis=1).astype(BF16)
    wv = jnp.concatenate([w_kv_up[:, h * kvd + QK_NOPE:(h + 1) * kvd] for h in range(N_HEADS)], axis=1)
    wvt = wv.T.astype(BF16)
    return win, norm_q.reshape(1, -1), norm_kv.reshape(1, -1), wqa, wqb, wk, wvt


def _fourier_consts(n, w_fourier):
    c = F_GDIM
    j = np.arange(c)
    ang = 2.0 * np.pi * np.outer(j, j) / c
    eye = np.eye(F_GROUPS)
    bdc = np.kron(eye, np.cos(ang))
    bds = np.kron(eye, np.sin(ang))
    bdcs = jnp.asarray(np.concatenate([bdc, -bds], axis=1), dtype=BF16)
    n1 = FFT_N1
    n2 = n // n1
    k1 = np.arange(n1)
    a1 = 2.0 * np.pi * np.outer(k1, k1) / n1
    wc, ws = np.cos(a1), np.sin(a1)
    w1 = jnp.asarray(np.block([[wc, ws], [-ws, wc]]), dtype=BF16)
    tw = 2.0 * np.pi * np.outer(k1, np.arange(n2)) / n
    tc = jnp.repeat(jnp.asarray(np.cos(tw), dtype=F32), F_WIDTH, axis=1)
    ts = jnp.repeat(jnp.asarray(np.sin(tw), dtype=F32), F_WIDTH, axis=1)
    k2 = np.arange(n2)
    a2 = 2.0 * np.pi * np.outer(k2, k2) / n2
    norm = 1.0 / math.sqrt(n * c)
    kt = FFT_K1_TILE
    mc = np.zeros((n2, kt, kt, n2))
    ms = np.zeros((n2, kt, kt, n2))
    for r in range(kt):
        mc[:, r, r, :] = np.cos(a2) * norm
        ms[:, r, r, :] = np.sin(a2) * norm
    mmat = np.concatenate([mc.reshape(n2 * kt, kt * n2), ms.reshape(n2 * kt, kt * n2)], axis=1)
    mmat = jnp.asarray(mmat, dtype=BF16)
    zblk = jnp.zeros((c, c), F32)
    bdw = jnp.concatenate(
        [jnp.concatenate([w_fourier[g] if j == g else zblk for j in range(F_GROUPS)], axis=1)
         for g in range(F_GROUPS)], axis=0)
    return bdcs, w1, tc, ts, mmat, bdw.astype(BF16)


def kernel(x, c, ctx, c_ctx, w_mod, b_mod, norm_attn_pre, norm_attn_post, norm_ffn_pre, norm_ffn_post, w_in, norm_q_lat, norm_kv_lat, w_q_up, w_kv_up, w_fourier, w_out, w_router, b_router, w_gate, b_gate, w_up, b_up, w_down, b_down):
    b, n, d = x.shape
    nctx = ctx.shape[1]
    t = b * n
    assert w_mod.shape[0] == 1 and b + 1 <= SUBLANES and n % (FFT_N1 * SUBLANES) == 0
    row2 = lambda a: a[0].reshape(1, -1)

    c8 = jnp.concatenate([c, c_ctx[None, :], jnp.zeros((SUBLANES - b - 1, d), F32)], axis=0)
    m3 = _modulation(c8, w_mod[0], b_mod[0]).reshape(SUBLANES * N_MOD, 1, d)

    wts = _prep_weights(w_in[0], norm_q_lat[0], norm_kv_lat[0], w_q_up[0], w_kv_up[0])
    bdcs, w1, tc, ts, mmat, bdw = _fourier_consts(n, w_fourier[0])
    pro_w = (wts[0], bdcs) + wts[1:]
    scale = (QK_NOPE + QK_ROPE) ** -0.5 * LOG2E
    g_pre = row2(norm_attn_pre)

    tm = min(512, n)
    t1x, t2x = _rope_tables(n)
    q, k, vt, za, zb = _prologue(x, m3, lambda bi: bi, g_pre, pro_w, t1x, t2x, tm, scale)
    t1c, t2c = _ctx_tables(nctx)
    _, kc, vtc, _, _ = _prologue(ctx, m3, lambda bi: b, g_pre, pro_w, t1c, t2c, nctx, scale)

    att = _attention(q, k, vt, kc, vtc, min(1024, n))
    four = _fourier(za, zb, w1, tc, ts, mmat, bdw)

    tmp = min(512, n)
    tri = jnp.asarray(np.triu(np.ones((tmp, tmp), np.float32), 1), dtype=BF16)
    width = N_HEADS * V_DIM
    woa = w_out[0, :width].astype(BF16)
    wof = w_out[0, width:].astype(BF16)
    x1, hx3, ridx, gates_t, cnt = _post_attention(
        att, four, x, m3, row2(norm_attn_post), row2(norm_ffn_pre), woa, wof,
        w_router[0].T.astype(BF16), b_router[0].reshape(-1, 1), tri, tmp)

    bm = 256
    counts = cnt[:, 0].astype(I32)
    padded = (counts + bm - 1) // bm * bm
    pad_end = jnp.cumsum(padded)
    pad_start = pad_end - padded
    nb = t * TOP_K // bm + N_EXPERTS
    blk_start = jnp.arange(nb, dtype=I32)[:, None] * bm
    block_e = jnp.minimum(jnp.sum((pad_end[None, :] <= blk_start).astype(I32), axis=1), N_EXPERTS - 1)
    nvalid = (pad_end[-1:] // bm).astype(I32)
    eids = jnp.arange(N_EXPERTS, dtype=I32)[:, None, None]
    dest = jnp.sum(jnp.where(ridx[None, :TOP_K] == eids, pad_start[:, None, None], 0), axis=0) + ridx[TOP_K:]
    fill = jnp.concatenate([pad_start + counts, padded - counts, nvalid]).astype(I32)

    def tiles(tile):
        return dest.reshape(TOP_K, t // tile, tile).transpose(1, 0, 2).reshape(t // tile, 1, TOP_K * tile)

    tmd = min(512, n)
    xs = _dispatch(fill, tiles(tmd), hx3, nb * bm, tmd, bm)
    ys = _experts(block_e, nvalid, xs, w_gate[0], b_gate[0], w_up[0], b_up[0], w_down[0], b_down[0], bm)
    tmc = 128
    return _combine(tiles(tmc), ys, gates_t, x1, m3, row2(norm_ffn_post), tmc)
```

```python
import functools
import math

import numpy as np
import jax
import jax.numpy as jnp
from jax import lax
from jax.experimental import pallas as pl
from jax.experimental.pallas import tpu as pltpu

F32 = jnp.float32
BF16 = jnp.bfloat16
I32 = jnp.int32

N_HEADS = 6
QK_NOPE = 128
QK_ROPE = 64
V_DIM = 128
Q_LORA = 384
KV_LORA = 256
F_GROUPS = 4
F_GDIM = 64
F_WIDTH = F_GROUPS * F_GDIM
GRID_W = 64
ROPE_BASE = 10000.0
N_EXPERTS = 32
TOP_K = 4
SWIGLU_LIMIT = 7.0
SWIGLU_ALPHA = 1.702
RMS_EPS = 1e-6
N_MOD = 6

LANES = 128
SUBLANES = 8
HEAD_W = 256
ROPE_W = 128
FFT_N1 = 128
FFT_K1_TILE = 8
VMEM_LIMIT = 56 * 1024 * 1024

LOG2E = 1.4426950408889634


def _cp(sem, vmem=None):
    return pltpu.CompilerParams(dimension_semantics=sem, vmem_limit_bytes=vmem or VMEM_LIMIT)


def _rms(x):
    return x * lax.rsqrt(jnp.mean(x * x, axis=-1, keepdims=True) + RMS_EPS)


def _dot(a, b):
    return jnp.dot(a, b, preferred_element_type=F32)


def _stack_rows(rows, height):
    w = rows[0].shape[1]
    sub = lax.broadcasted_iota(I32, (height, w), 0)
    out = jnp.zeros((height, w), rows[0].dtype)
    for r, v in enumerate(rows):
        out = jnp.where(sub == r, v, out)
    return out


def _dot_nt(a, b):
    return lax.dot_general(a, b, (((1,), (1,)), ((), ())), preferred_element_type=F32)


def _mod_kernel(c_ref, w_ref, b_ref, o_ref):
    c = c_ref[...]
    a = c / (1.0 + jnp.exp(-c))
    a_hi = a.astype(BF16)
    a_lo = (a - a_hi.astype(F32)).astype(BF16)
    w = w_ref[...]
    w_hi = w.astype(BF16)
    w_lo = (w - w_hi.astype(F32)).astype(BF16)
    acc = _dot(a_hi, w_hi) + _dot(a_lo, w_hi) + _dot(a_hi, w_lo)
    o_ref[...] = acc + b_ref[...]


def _modulation(c8, w_mod, b_mod):
    d, n = w_mod.shape
    tn = 1024
    return pl.pallas_call(
        _mod_kernel,
        grid=(n // tn,),
        in_specs=[pl.BlockSpec((SUBLANES, d), lambda j: (0, 0)),
                  pl.BlockSpec((d, tn), lambda j: (0, j)),
                  pl.BlockSpec((1, tn), lambda j: (0, j))],
        out_specs=pl.BlockSpec((SUBLANES, tn), lambda j: (0, j)),
        out_shape=jax.ShapeDtypeStruct((SUBLANES, n), F32),
        name="modulation",
        compiler_params=_cp(("arbitrary",)),
    )(c8, w_mod, b_mod.reshape(1, n))


def _pro_kernel(x_ref, sh_ref, sc_ref, g_ref, win_ref, bdcs_ref, nq_ref, nkv_ref, wqa_ref, wqb_ref,
                wk_ref, wvt_ref, t1_ref, t2_ref, q_ref, k_ref, vt_ref, za_ref, zb_ref, *, scale):
    x = x_ref[0]
    hx = _rms(x) * g_ref[...] * (1.0 + sc_ref[0]) + sh_ref[0]
    px = _dot(hx.astype(BF16), win_ref[...])
    u = px[:, 0:F_WIDTH].astype(BF16)
    z = _dot(u, bdcs_ref[...])
    za_ref[0] = z[:, :F_WIDTH].astype(BF16)
    zb_ref[0] = z[:, F_WIDTH:].astype(BF16)
    o_q = F_WIDTH
    o_kv = o_q + Q_LORA
    o_ra = o_kv + KV_LORA
    o_rb = o_ra + ROPE_W
    t1 = t1_ref[...]
    t2 = t2_ref[...]
    qn = (_rms(px[:, o_q:o_kv]) * nq_ref[...]).astype(BF16)
    qa = _dot(qn, wqa_ref[...])
    qb = _dot(qn, wqb_ref[...])
    for h in range(N_HEADS):
        nope = qa[:, h * HEAD_W:h * HEAD_W + QK_NOPE] * scale
        rope = (qa[:, h * HEAD_W + QK_NOPE:(h + 1) * HEAD_W] * t1
                + qb[:, h * ROPE_W:(h + 1) * ROPE_W] * t2) * scale
        q_ref[0, h, :, 0:QK_NOPE] = nope.astype(BF16)
        q_ref[0, h, :, QK_NOPE:HEAD_W] = rope.astype(BF16)
    kvn = (_rms(px[:, o_kv:o_ra]) * nkv_ref[...]).astype(BF16)
    kn = _dot(kvn, wk_ref[...])
    kr = (px[:, o_ra:o_rb] * t1 + px[:, o_rb:o_rb + ROPE_W] * t2).astype(BF16)
    for h in range(N_HEADS):
        k_ref[0, h, :, 0:QK_NOPE] = kn[:, h * QK_NOPE:(h + 1) * QK_NOPE].astype(BF16)
        k_ref[0, h, :, QK_NOPE:HEAD_W] = kr
    vt = _dot_nt(wvt_ref[...], kvn)
    for h in range(N_HEADS):
        vt_ref[0, h, 0] = vt[h * V_DIM:(h + 1) * V_DIM, :].astype(BF16)


def _prologue(x, m3, mod_row, g_pre, wts, t1, t2, tm, scale):
    b, n, d = x.shape
    nt = n // tm
    win, bdcs, nq, nkv, wqa, wqb, wk, wvt = wts
    full = lambda a: pl.BlockSpec(a.shape, lambda bi, i: (0,) * a.ndim)
    kern = functools.partial(_pro_kernel, scale=scale)
    return pl.pallas_call(
        kern,
        grid=(b, nt),
        in_specs=[pl.BlockSpec((1, tm, d), lambda bi, i: (bi, i, 0)),
                  pl.BlockSpec((1, 1, d), lambda bi, i: (mod_row(bi) * N_MOD + 0, 0, 0)),
                  pl.BlockSpec((1, 1, d), lambda bi, i: (mod_row(bi) * N_MOD + 1, 0, 0)),
                  full(g_pre), full(win), full(bdcs), full(nq), full(nkv), full(wqa), full(wqb),
                  full(wk), full(wvt),
                  pl.BlockSpec((tm, ROPE_W), lambda bi, i: (i, 0)),
                  pl.BlockSpec((tm, ROPE_W), lambda bi, i: (i, 0))],
        out_specs=[pl.BlockSpec((1, N_HEADS, tm, HEAD_W), lambda bi, i: (bi, 0, i, 0)),
                   pl.BlockSpec((1, N_HEADS, tm, HEAD_W), lambda bi, i: (bi, 0, i, 0)),
                   pl.BlockSpec((1, N_HEADS, 1, V_DIM, tm), lambda bi, i: (bi, 0, i, 0, 0)),
                   pl.BlockSpec((1, tm, F_WIDTH), lambda bi, i: (bi, i, 0)),
                   pl.BlockSpec((1, tm, F_WIDTH), lambda bi, i: (bi, i, 0))],
        out_shape=[jax.ShapeDtypeStruct((b, N_HEADS, n, HEAD_W), BF16),
                   jax.ShapeDtypeStruct((b, N_HEADS, n, HEAD_W), BF16),
                   jax.ShapeDtypeStruct((b, N_HEADS, nt, V_DIM, tm), BF16),
                   jax.ShapeDtypeStruct((b, n, F_WIDTH), BF16),
                   jax.ShapeDtypeStruct((b, n, F_WIDTH), BF16)],
        name="prologue",
        compiler_params=_cp(("arbitrary", "arbitrary")),
    )(x, m3, m3, g_pre, win, bdcs, nq, nkv, wqa, wqb, wk, wvt, t1, t2)


def _attn_kernel(q_ref, k_ref, vt_ref, kc_ref, vtc_ref, o_ref, sa_sc, sb_sc, ca_sc, cb_sc, m_sc, l_sc, acc_sc,
                 *, nkv, tk, group_size):
    q = q_ref[0, 0]
    s = _dot_nt(kc_ref[0, 0], q)
    m0 = jnp.max(s, axis=0, keepdims=True)
    p = jnp.exp2(s - m0)
    m_sc[...] = m0
    l_sc[...] = jnp.sum(p, axis=0, keepdims=True)
    acc_sc[...] = _dot(vtc_ref[0, 0, 0], p.astype(BF16))

    def scores(j, s_dst, c_dst):
        start = pl.multiple_of(j * tk, tk)
        s = _dot_nt(k_ref[0, 0, pl.ds(start, tk), :], q)
        s_dst[...] = s
        c_dst[...] = jnp.max(s, axis=0, keepdims=True)

    def accumulate(j, s_src, c_src):
        m_prev = m_sc[...]
        m_new = jnp.maximum(m_prev, c_src[...])
        alpha = jnp.exp2(m_prev - m_new)
        p = jnp.exp2(s_src[...] - m_new)
        l_sc[...] = alpha * l_sc[...] + jnp.sum(p, axis=0, keepdims=True)
        acc_sc[...] = alpha * acc_sc[...] + _dot(vt_ref[0, 0, j], p.astype(BF16))
        m_sc[...] = m_new

    bufs = ((sa_sc, ca_sc), (sb_sc, cb_sc))

    def group(j0, last):
        for u in range(group_size):
            if not (last and u == group_size - 1):
                scores(j0 + u + 1, *bufs[(u + 1) % 2])
            accumulate(j0 + u, *bufs[u % 2])

    scores(0, *bufs[0])

    def body(i, carry):
        group(group_size * i, False)
        return carry

    lax.fori_loop(0, nkv // group_size - 1, body, 0)
    group(nkv - group_size, True)
    o = acc_sc[...] / l_sc[...]
    o_ref[0] = o.T.astype(BF16)


def _attention(q, k, vt, kc, vtc, tq):
    b, h, n, _ = q.shape
    nkv, tk = vt.shape[2], vt.shape[4]
    nctx = kc.shape[2]
    assert nkv % 2 == 0 and nkv >= 2, "the score pipeline alternates two score buffers"
    group_size = 4 if nkv % 4 == 0 else 2
    kern = functools.partial(_attn_kernel, nkv=nkv, tk=tk, group_size=group_size)
    return pl.pallas_call(
        kern,
        grid=(b, h, n // tq),
        in_specs=[pl.BlockSpec((1, 1, tq, HEAD_W), lambda bi, hi, i: (bi, hi, i, 0)),
                  pl.BlockSpec((1, 1, n, HEAD_W), lambda bi, hi, i: (bi, hi, 0, 0)),
                  pl.BlockSpec((1, 1, nkv, V_DIM, tk), lambda bi, hi, i: (bi, hi, 0, 0, 0)),
                  pl.BlockSpec((1, 1, nctx, HEAD_W), lambda bi, hi, i: (bi, hi, 0, 0)),
                  pl.BlockSpec((1, 1, 1, V_DIM, nctx), lambda bi, hi, i: (bi, hi, 0, 0, 0))],
        out_specs=pl.BlockSpec((1, tq, V_DIM), lambda bi, hi, i: (bi, i, hi)),
        out_shape=jax.ShapeDtypeStruct((b, n, h * V_DIM), BF16),
        scratch_shapes=[pltpu.VMEM((tk, tq), F32), pltpu.VMEM((tk, tq), F32),
                        pltpu.VMEM((1, tq), F32), pltpu.VMEM((1, tq), F32),
                        pltpu.VMEM((1, tq), F32), pltpu.VMEM((1, tq), F32),
                        pltpu.VMEM((V_DIM, tq), F32)],
        name="attention",
        compiler_params=_cp(("arbitrary", "arbitrary", "arbitrary")),
    )(q, k, vt, kc, vtc)


def _fft_a_kernel(za_ref, zb_ref, w1_ref, tc_ref, ts_ref, yr_ref, yi_ref):
    ab = jnp.concatenate([za_ref[0], zb_ref[0]], axis=0)
    y = _dot(w1_ref[...], ab)
    re = y[:FFT_N1]
    im = y[FFT_N1:]
    tc = tc_ref[...]
    ts = ts_ref[...]
    yr_ref[0] = (re * tc + im * ts).astype(BF16)
    yi_ref[0] = (im * tc - re * ts).astype(BF16)


def _fft_b_kernel(yr_ref, yi_ref, m_ref, bdw_ref, o_ref):
    ri = jnp.concatenate([yr_ref[0], yi_ref[0]], axis=0)
    f = _dot(m_ref[...], ri)
    four = _dot(f.astype(BF16), bdw_ref[...])
    o_ref[0] = four.reshape(o_ref.shape[1:])


def _fourier(za, zb, w1, tc, ts, mmat, bdw):
    b, n, c = za.shape
    n2 = n // FFT_N1
    wide = n2 * c
    tn = min(2048, wide)
    za2 = za.reshape(b, FFT_N1, wide)
    zb2 = zb.reshape(b, FFT_N1, wide)
    yr, yi = pl.pallas_call(
        _fft_a_kernel,
        grid=(wide // tn, b),
        in_specs=[pl.BlockSpec((1, FFT_N1, tn), lambda i, bi: (bi, 0, i)),
                  pl.BlockSpec((1, FFT_N1, tn), lambda i, bi: (bi, 0, i)),
                  pl.BlockSpec(w1.shape, lambda i, bi: (0, 0)),
                  pl.BlockSpec((FFT_N1, tn), lambda i, bi: (0, i)),
                  pl.BlockSpec((FFT_N1, tn), lambda i, bi: (0, i))],
        out_specs=[pl.BlockSpec((1, FFT_N1, tn), lambda i, bi: (bi, 0, i)),
                   pl.BlockSpec((1, FFT_N1, tn), lambda i, bi: (bi, 0, i))],
        out_shape=[jax.ShapeDtypeStruct((b, FFT_N1, wide), BF16)] * 2,
        name="fourier_a",
        compiler_params=_cp(("arbitrary", "arbitrary")),
    )(za2, zb2, w1, tc, ts)
    rows = FFT_K1_TILE * n2
    yr = yr.reshape(b, n, c)
    yi = yi.reshape(b, n, c)
    out = pl.pallas_call(
        _fft_b_kernel,
        grid=(b, FFT_N1 // FFT_K1_TILE),
        in_specs=[pl.BlockSpec((1, rows, c), lambda bi, i: (bi, i, 0)),
                  pl.BlockSpec((1, rows, c), lambda bi, i: (bi, i, 0)),
                  pl.BlockSpec(mmat.shape, lambda bi, i: (0, 0)),
                  pl.BlockSpec(bdw.shape, lambda bi, i: (0, 0))],
        out_specs=pl.BlockSpec((1, n2, FFT_K1_TILE, c), lambda bi, i: (bi, 0, i, 0)),
        out_shape=jax.ShapeDtypeStruct((b, n2, FFT_N1, c), F32),
        name="fourier_b",
        compiler_params=_cp(("arbitrary", "arbitrary")),
    )(yr, yi, mmat, bdw)
    return out.reshape(b, n, c)


def _post_kernel(att_ref, four_ref, x_ref, g1_ref, sh2_ref, sc2_ref, gpost_ref, gpre_ref, woa_ref, wof_ref,
                 wrt_ref, br_ref, tri_ref, x1_ref, hx_ref, ridx_ref, gates_ref, cnt_ref, carry_sc):
    first = (pl.program_id(0) == 0) & (pl.program_id(1) == 0)

    @pl.when(first)
    def _():
        carry_sc[...] = jnp.zeros_like(carry_sc)

    mix = _dot(att_ref[0], woa_ref[...]) + _dot(four_ref[0].astype(BF16), wof_ref[...])
    x1 = x_ref[0] + g1_ref[0] * (_rms(mix) * gpost_ref[...])
    x1_ref[0] = x1
    hx = _rms(x1) * gpre_ref[...] * (1.0 + sc2_ref[0]) + sh2_ref[0]
    hx_ref[...] = hx
    lg =_dot_nt(wrt_ref[...], hx.astype(BF16)) + br_ref[...]
    iota_e = lax.broadcasted_iota(I32, lg.shape, 0)
    cur = lg
    tops, idxs, ohs = [], [], []
    for _ in range(TOP_K):
        mx = jnp.max(cur, axis=0, keepdims=True)
        idx = jnp.min(jnp.where(cur == mx, iota_e, N_EXPERTS), axis=0, keepdims=True)
        oh = iota_e == idx
        cur = jnp.where(oh, -jnp.inf, cur)
        tops.append(mx)
        idxs.append(idx)
        ohs.append(oh)
    es = [jnp.exp(t - tops[0]) for t in tops]
    den = es[0] + es[1] + es[2] + es[3]
    gates = [e / den for e in es]
    ohsum = ohs[0].astype(F32) + ohs[1].astype(F32) + ohs[2].astype(F32) + ohs[3].astype(F32)
    base = carry_sc[:, 0:1] + _dot(ohsum.astype(BF16), tri_ref[...])
    ranks = [jnp.sum(jnp.where(oh, base, 0.0), axis=0, keepdims=True).astype(I32) for oh in ohs]
    ridx_ref[...] = _stack_rows(idxs + ranks, 2 * TOP_K)
    gates_ref[...] = _stack_rows(gates, LANES).T
    carry_sc[...] = carry_sc[...] + jnp.sum(ohsum, axis=1, keepdims=True)
    cnt_ref[...] = carry_sc[...]


def _post_attention(att, four, x, m3, gpost, gpre, woa, wof, wrt, br, tri, tm):
    b, n, d = x.shape
    nt = n // tm
    t = b * n
    full = lambda a: pl.BlockSpec(a.shape, lambda bi, i: (0,) * a.ndim)
    mrow = lambda j: pl.BlockSpec((1, 1, d), lambda bi, i: (bi * N_MOD + j, 0, 0))
    return pl.pallas_call(
        _post_kernel,
        grid=(b, nt),
        in_specs=[pl.BlockSpec((1, tm, att.shape[2]), lambda bi, i: (bi, i, 0)),
                  pl.BlockSpec((1, tm, F_WIDTH), lambda bi, i: (bi, i, 0)),
                  pl.BlockSpec((1, tm, d), lambda bi, i: (bi, i, 0)),
                  mrow(2), mrow(3), mrow(4),
                  full(gpost), full(gpre), full(woa), full(wof), full(wrt), full(br), full(tri)],
        out_specs=[pl.BlockSpec((1, tm, d), lambda bi, i: (bi, i, 0)),
                   pl.BlockSpec((tm, d), lambda bi, i: (bi * nt + i, 0)),
                   pl.BlockSpec((2 * TOP_K, tm), lambda bi, i: (0, bi * nt + i)),
                   pl.BlockSpec((tm, LANES), lambda bi, i: (bi * nt + i, 0)),
                   pl.BlockSpec((N_EXPERTS, LANES), lambda bi, i: (0, 0))],
        out_shape=[jax.ShapeDtypeStruct((b, n, d), F32),
                   jax.ShapeDtypeStruct((t, d), F32),
                   jax.ShapeDtypeStruct((2 * TOP_K, t), I32),
                   jax.ShapeDtypeStruct((t, LANES), F32),
                   jax.ShapeDtypeStruct((N_EXPERTS, LANES), F32)],
        scratch_shapes=[pltpu.VMEM((N_EXPERTS, LANES), F32)],
        name="post_attention",
        compiler_params=_cp(("arbitrary", "arbitrary")),
    )(att, four, x, m3, m3, m3, gpost, gpre, woa, wof, wrt, br, tri)


def _dispatch_kernel(fill_ref, dest_ref, hx_ref, xs_hbm, zrow, sem, fsem, *, tm, bm):
    i = pl.program_id(0)

    @pl.when(i == 0)
    def _():
        zrow[...] = jnp.zeros_like(zrow)
        nblk = xs_hbm.shape[0] // bm

        def tail(blk, c):
            cp = pltpu.make_async_copy(zrow, xs_hbm.at[pl.ds(blk * bm, bm), :], fsem)
            cp.start()
            cp.wait()
            return c

        lax.fori_loop(fill_ref[2 * N_EXPERTS], nblk, tail, 0)

        def per_expert(e, c):
            s = fill_ref[e]
            n = fill_ref[N_EXPERTS + e]

            def fill_copy(r):
                return pltpu.make_async_copy(zrow.at[pl.ds(0, 1), :], xs_hbm.at[pl.ds(s + r, 1), :], fsem)

            def start(r, cc):
                fill_copy(r).start()
                return cc

            def wait(r, cc):
                fill_copy(r).wait()
                return cc

            lax.fori_loop(0, n, start, 0)
            lax.fori_loop(0, n, wait, 0)
            return c

        lax.fori_loop(0, N_EXPERTS, per_expert, 0)

    def row_copy(r, k):
        d = dest_ref[0, 0, k * tm + r]
        return pltpu.make_async_copy(hx_ref.at[pl.ds(r, 1), :], xs_hbm.at[pl.ds(d, 1), :], sem)

    def start(r, c):
        for k in range(TOP_K):
            row_copy(r, k).start()
        return c

    def wait(r, c):
        for k in range(TOP_K):
            row_copy(r, k).wait()
        return c

    lax.fori_loop(0, tm, start, 0, unroll=4)
    lax.fori_loop(0, tm, wait, 0, unroll=4)


def _dispatch(fill, dest_tiles, hx, n_slots, tm, bm):
    t, dm = hx.shape
    kern = functools.partial(_dispatch_kernel, tm=tm, bm=bm)
    return pl.pallas_call(
        kern,
        grid_spec=pltpu.PrefetchScalarGridSpec(
            num_scalar_prefetch=1,
            grid=(t // tm,),
            in_specs=[pl.BlockSpec((1, 1, TOP_K * tm), lambda i, f: (i, 0, 0), memory_space=pltpu.SMEM),
                      pl.BlockSpec((tm, dm), lambda i, f: (i, 0))],
            out_specs=pl.BlockSpec(memory_space=pl.ANY),
            scratch_shapes=[pltpu.VMEM((bm, dm), F32),
                            pltpu.SemaphoreType.DMA(()), pltpu.SemaphoreType.DMA(())]),
        out_shape=jax.ShapeDtypeStruct((n_slots, dm), F32),
        name="dispatch",
        compiler_params=_cp(("arbitrary",)),
    )(fill, dest_tiles, hx)


def _moe_kernel(be_ref, nv_ref, x_ref, wg_ref, bg_ref, wu_ref, bu_ref, wd_ref, bd_ref, y_ref, wgb, wub, wdb):
    i = pl.program_id(0)

    @pl.when(i < nv_ref[0])
    def _():
        e = be_ref[i]
        prev = be_ref[jnp.maximum(i - 1, 0)]

        @pl.when((i == 0) | (e != prev))
        def _():
            wgb[...] = wg_ref[0].astype(BF16)
            wub[...] = wu_ref[0].astype(BF16)
            wdb[...] = wd_ref[0].astype(BF16)

        xb = x_ref[...].astype(BF16)
        g = jnp.minimum(_dot(xb, wgb[...]) + bg_ref[0], SWIGLU_LIMIT)
        u = jnp.clip(_dot(xb, wub[...]) + bu_ref[0], -SWIGLU_LIMIT, SWIGLU_LIMIT)
        act = g * (1.0 / (1.0 + jnp.exp(-SWIGLU_ALPHA * g))) * (u + 1.0)
        y_ref[...] = _dot(act.astype(BF16), wdb[...]) + bd_ref[0]

    @pl.when(i >= nv_ref[0])
    def _():
        y_ref[...] = jnp.zeros_like(y_ref)


def _experts(block_e, nvalid, xs, wg, bg, wu, bu, wd, bd, bm):
    n_slots = xs.shape[0]
    e, d, f = wg.shape
    nb = n_slots // bm
    xmap = lambda i, be, nv: (jnp.minimum(i, nv[0] - 1), 0)
    wmap = lambda i, be, nv: (be[i], 0, 0)
    return pl.pallas_call(
        _moe_kernel,
        grid_spec=pltpu.PrefetchScalarGridSpec(
            num_scalar_prefetch=2,
            grid=(nb,),
            in_specs=[pl.BlockSpec((bm, d), xmap),
                      pl.BlockSpec((1, d, f), wmap), pl.BlockSpec((1, 1, f), wmap),
                      pl.BlockSpec((1, d, f), wmap), pl.BlockSpec((1, 1, f), wmap),
                      pl.BlockSpec((1, f, d), wmap), pl.BlockSpec((1, 1, d), wmap)],
            out_specs=pl.BlockSpec((bm, d), lambda i, be, nv: (i, 0)),
            scratch_shapes=[pltpu.VMEM((d, f), BF16), pltpu.VMEM((d, f), BF16), pltpu.VMEM((f, d), BF16)]),
        out_shape=jax.ShapeDtypeStruct((n_slots, d), F32),
        name="experts",
        compiler_params=_cp(("arbitrary",)),
    )(block_e, nvalid, xs, wg, bg.reshape(e, 1, f), wu, bu.reshape(e, 1, f), wd, bd.reshape(e, 1, d))


def _combine_kernel(dcur_ref, dnxt_ref, ys_hbm, gates_ref, x1_ref, g2_ref, gpost_ref, o_ref, buf, sem,
                    *, tm, nsteps):
    i = pl.program_id(0)
    slot = i % 2

    def row_copy(dref, sl, r, k):
        d = dref[0, 0, k * tm + r]
        return pltpu.make_async_copy(ys_hbm.at[pl.ds(d, 1), :], buf.at[sl, k, pl.ds(r, 1), :], sem.at[sl])

    def issue(dref, sl):
        def start(r, c):
            for k in range(TOP_K):
                row_copy(dref, sl, r, k).start()
            return c

        lax.fori_loop(0, tm, start, 0, unroll=4)

    @pl.when(i == 0)
    def _():
        issue(dcur_ref, 0)

    @pl.when(i + 1 < nsteps)
    def _():
        issue(dnxt_ref, 1 - slot)

    def wait(r, c):
        for k in range(TOP_K):
            row_copy(dcur_ref, slot, r, k).wait()
        return c

    lax.fori_loop(0, tm, wait, 0, unroll=4)
    gt = gates_ref[...]
    y = gt[:, 0:1] * buf[slot, 0]
    for k in range(1, TOP_K):
        y = y + gt[:, k:k + 1] * buf[slot, k]
    o_ref[...] = x1_ref[...] + g2_ref[0] * (_rms(y) * gpost_ref[...])


def _combine(dest_tiles, ys, gates_t, x1, m3, gpost, tm):
    b, n, d = x1.shape
    nt = n // tm
    nsteps = b * nt
    kern = functools.partial(_combine_kernel, tm=tm, nsteps=nsteps)
    dspec = lambda f: pl.BlockSpec((1, 1, TOP_K * tm), f, memory_space=pltpu.SMEM)
    out = pl.pallas_call(
        kern,
        grid=(nsteps,),
        in_specs=[dspec(lambda i: (i, 0, 0)),
                  dspec(lambda i: (jnp.minimum(i + 1, nsteps - 1), 0, 0)),
                  pl.BlockSpec(memory_space=pl.ANY),
                  pl.BlockSpec((tm, LANES), lambda i: (i, 0)),
                  pl.BlockSpec((tm, d), lambda i: (i, 0)),
                  pl.BlockSpec((1, 1, d), lambda i: (i // nt * N_MOD + 5, 0, 0)),
                  pl.BlockSpec(gpost.shape, lambda i: (0, 0))],
        out_specs=pl.BlockSpec((tm, d), lambda i: (i, 0)),
        out_shape=jax.ShapeDtypeStruct((b * n, d), F32),
        scratch_shapes=[pltpu.VMEM((2, TOP_K, tm, d), F32), pltpu.SemaphoreType.DMA((2,))],
        name="combine",
        compiler_params=_cp(("arbitrary",)),
    )(dest_tiles, dest_tiles, ys, gates_t, x1.reshape(b * n, d), m3, gpost)
    return out.reshape(b, n, d)


def _rope_tables(n):
    rows = n // GRID_W
    row = jnp.repeat(jnp.arange(rows, dtype=F32), GRID_W)
    col = jnp.tile(jnp.arange(GRID_W, dtype=F32), rows)
    half = QK_ROPE // 2
    inv = 1.0 / (ROPE_BASE ** (jnp.arange(0, half, 2, dtype=F32) / half))
    ang = jnp.concatenate([row[:, None] * inv, col[:, None] * inv], axis=-1)
    cos, sin = jnp.cos(ang), jnp.sin(ang)
    zero = jnp.zeros((n, ROPE_W - QK_ROPE), F32)
    return jnp.concatenate([cos, cos, zero], axis=1), jnp.concatenate([sin, sin, zero], axis=1)


def _ctx_tables(n):
    half = QK_ROPE // 2
    t1 = np.zeros((n, ROPE_W), np.float32)
    t1[:, :2 * half] = 1.0
    return jnp.asarray(t1), jnp.zeros((n, ROPE_W), F32)


def _prep_weights(w_in, norm_q, norm_kv, w_q_up, w_kv_up):
    d = w_in.shape[0]
    half = QK_ROPE // 2
    zr = lambda r, c: jnp.zeros((r, c), F32)
    o_rope = F_WIDTH + Q_LORA + KV_LORA
    kr = w_in[:, o_rope:o_rope + QK_ROPE]
    kre, kro = kr[:, 0::2], kr[:, 1::2]
    win = jnp.concatenate([w_in[:, :o_rope],
                           kre, kro, zr(d, ROPE_W - QK_ROPE),
                           -kro, kre, zr(d, ROPE_W - QK_ROPE)], axis=1).astype(BF16)
    hd = QK_NOPE + QK_ROPE
    qa, qb = [], []
    for h in range(N_HEADS):
        wn = w_q_up[:, h * hd:h * hd + QK_NOPE]
        wr = w_q_up[:, h * hd + QK_NOPE:(h + 1) * hd]
        we, wo = wr[:, 0::2], wr[:, 1::2]
        qa += [wn, we, wo, zr(Q_LORA, ROPE_W - QK_ROPE)]
        qb += [-wo, we, zr(Q_LORA, ROPE_W - QK_ROPE)]
    wqa = jnp.concatenate(qa, axis=1).astype(BF16)
    wqb = jnp.concatenate(qb, axis=1).astype(BF16)
    kvd = QK_NOPE + V_DIM
    wk = jnp.concatenate([w_kv_up[:, h * kvd:h * kvd + QK_NOPE] for h in range(N_HEADS)], axis=1).astype(BF16)
    wv = jnp.concatenate([w_kv_up[:, h * kvd + QK_NOPE:(h + 1) * kvd] for h in range(N_HEADS)], axis=1)
    wvt = wv.T.astype(BF16)
    return win, norm_q.reshape(1, -1), norm_kv.reshape(1, -1), wqa, wqb, wk, wvt


def _fourier_consts(n, w_fourier):
    c = F_GDIM
    j = np.arange(c)
    ang = 2.0 * np.pi * np.outer(j, j) / c
    eye = np.eye(F_GROUPS)
    bdc = np.kron(eye, np.cos(ang))
    bds = np.kron(eye, np.sin(ang))
    bdcs = jnp.asarray(np.concatenate([bdc, -bds], axis=1), dtype=BF16)
    n1 = FFT_N1
    n2 = n // n1
    k1 = np.arange(n1)
    a1 = 2.0 * np.pi * np.outer(k1, k1) / n1
    wc, ws = np.cos(a1), np.sin(a1)
    w1 = jnp.asarray(np.block([[wc, ws], [-ws, wc]]), dtype=BF16)
    tw = 2.0 * np.pi * np.outer(k1, np.arange(n2)) / n
    tc = jnp.repeat(jnp.asarray(np.cos(tw), dtype=F32), F_WIDTH, axis=1)
    ts = jnp.repeat(jnp.asarray(np.sin(tw), dtype=F32), F_WIDTH, axis=1)
    k2 = np.arange(n2)
    a2 = 2.0 * np.pi * np.outer(k2, k2) / n2
    norm = 1.0 / math.sqrt(n * c)
    kt = FFT_K1_TILE
    mc = np.zeros((n2, kt, kt, n2))
    ms = np.zeros((n2, kt, kt, n2))
    for r in range(kt):
        mc[:, r, r, :] = np.cos(a2) * norm
        ms[:, r, r, :] = np.sin(a2) * norm
    mmat = np.concatenate([mc.reshape(n2 * kt, kt * n2), ms.reshape(n2 * kt, kt * n2)], axis=1)
    mmat = jnp.asarray(mmat, dtype=BF16)
    zblk = jnp.zeros((c, c), F32)
    bdw = jnp.concatenate(
        [jnp.concatenate([w_fourier[g] if j == g else zblk for j in range(F_GROUPS)], axis=1)
         for g in range(F_GROUPS)], axis=0)
    return bdcs, w1, tc, ts, mmat, bdw.astype(BF16)


def kernel(x, c, ctx, c_ctx, w_mod, b_mod, norm_attn_pre, norm_attn_post, norm_ffn_pre, norm_ffn_post, w_in, norm_q_lat, norm_kv_lat, w_q_up, w_kv_up, w_fourier, w_out, w_router, b_router, w_gate, b_gate, w_up, b_up, w_down, b_down):
    b, n, d = x.shape
    nctx = ctx.shape[1]
    t = b * n
    assert w_mod.shape[0] == 1 and b + 1 <= SUBLANES and n % (FFT_N1 * SUBLANES) == 0
    row2 = lambda a: a[0].reshape(1, -1)

    c8 = jnp.concatenate([c, c_ctx[None, :], jnp.zeros((SUBLANES - b - 1, d), F32)], axis=0)
    m3 = _modulation(c8, w_mod[0], b_mod[0]).reshape(SUBLANES * N_MOD, 1, d)

    wts = _prep_weights(w_in[0], norm_q_lat[0], norm_kv_lat[0], w_q_up[0], w_kv_up[0])
    bdcs, w1, tc, ts, mmat, bdw = _fourier_consts(n, w_fourier[0])
    pro_w = (wts[0], bdcs) + wts[1:]
    scale = (QK_NOPE + QK_ROPE) ** -0.5 * LOG2E
    g_pre = row2(norm_attn_pre)

    tm = min(512, n)
    t1x, t2x = _rope_tables(n)
    q, k, vt, za, zb = _prologue(x, m3, lambda bi: bi, g_pre, pro_w, t1x, t2x, tm, scale)
    t1c, t2c = _ctx_tables(nctx)
    _, kc, vtc, _, _ = _prologue(ctx, m3, lambda bi: b, g_pre, pro_w, t1c, t2c, nctx, scale)

    att = _attention(q, k, vt, kc, vtc, min(1024, n))
    four = _fourier(za, zb, w1, tc, ts, mmat, bdw)

    tmp = min(512, n)
    tri = jnp.asarray(np.triu(np.ones((tmp, tmp), np.float32), 1), dtype=BF16)
    width = N_HEADS * V_DIM
    woa = w_out[0, :width].astype(BF16)
    wof = w_out[0, width:].astype(BF16)
    x1, hx2, ridx, gates_t, cnt = _post_attention(
        att, four, x, m3, row2(norm_attn_post), row2(norm_ffn_pre), woa, wof,
        w_router[0].T.astype(BF16), b_router[0].reshape(-1, 1), tri, tmp)

    bm = 256
    counts = cnt[:, 0].astype(I32)
    padded = (counts + bm - 1) // bm * bm
    pad_end = jnp.cumsum(padded)
    pad_start = pad_end - padded
    nb = t * TOP_K // bm + N_EXPERTS
    blk_start = jnp.arange(nb, dtype=I32)[:, None] * bm
    block_e = jnp.minimum(jnp.sum((pad_end[None, :] <= blk_start).astype(I32), axis=1), N_EXPERTS - 1)
    nvalid = (pad_end[-1:] // bm).astype(I32)
    eids = jnp.arange(N_EXPERTS, dtype=I32)[:, None, None]
    dest = jnp.sum(jnp.where(ridx[None, :TOP_K] == eids, pad_start[:, None, None], 0), axis=0) + ridx[TOP_K:]
    fill = jnp.concatenate([pad_start + counts, padded - counts, nvalid]).astype(I32)

    def tiles(tile):
        return dest.reshape(TOP_K, t // tile, tile).transpose(1, 0, 2).reshape(t // tile, 1, TOP_K * tile)

    tmd = min(512, n)
    xs = _dispatch(fill, tiles(tmd), hx2, nb * bm, tmd, bm)
    ys = _experts(block_e, nvalid, xs, w_gate[0], b_gate[0], w_up[0], b_up[0], w_down[0], b_down[0], bm)
    tmc = 128
    return _combine(tiles(tmc), ys, gates_t, x1, m3, row2(norm_ffn_post), tmc)
```

```python
import functools
import math

import numpy as np
import jax
import jax.numpy as jnp
from jax import lax
from jax.experimental import pallas as pl
from jax.experimental.pallas import tpu as pltpu

F32 = jnp.float32
BF16 = jnp.bfloat16
I32 = jnp.int32

N_HEADS = 6
QK_NOPE = 128
QK_ROPE = 64
V_DIM = 128
Q_LORA = 384
KV_LORA = 256
F_GROUPS = 4
F_GDIM = 64
F_WIDTH = F_GROUPS * F_GDIM
GRID_W = 64
ROPE_BASE = 10000.0
N_EXPERTS = 32
TOP_K = 4
SWIGLU_LIMIT = 7.0
SWIGLU_ALPHA = 1.702
RMS_EPS = 1e-6
N_MOD = 6

LANES = 128
SUBLANES = 8
HEAD_W = 256
ROPE_W = 128
FFT_N1 = 128
FFT_K1_TILE = 8
VMEM_LIMIT = 56 * 1024 * 1024

LOG2E = 1.4426950408889634
NORM_SLACK = 1.0 + 2.0 ** -6
MIN_ROW_SUM = 2.0 ** -60


def _cp(sem, vmem=None):
    return pltpu.CompilerParams(dimension_semantics=sem, vmem_limit_bytes=vmem or VMEM_LIMIT)


def _rms(x):
    return x * lax.rsqrt(jnp.mean(x * x, axis=-1, keepdims=True) + RMS_EPS)


def _dot(a, b):
    return jnp.dot(a, b, preferred_element_type=F32)


def _stack_rows(rows, height):
    w = rows[0].shape[1]
    sub = lax.broadcasted_iota(I32, (height, w), 0)
    out = jnp.zeros((height, w), rows[0].dtype)
    for r, v in enumerate(rows):
        out = jnp.where(sub == r, v, out)
    return out


def _dot_nt(a, b):
    return lax.dot_general(a, b, (((1,), (1,)), ((), ())), preferred_element_type=F32)


def _mod_kernel(c_ref, w_ref, b_ref, o_ref):
    c = c_ref[...]
    a = c / (1.0 + jnp.exp(-c))
    a_hi = a.astype(BF16)
    a_lo = (a - a_hi.astype(F32)).astype(BF16)
    w = w_ref[...]
    w_hi = w.astype(BF16)
    w_lo = (w - w_hi.astype(F32)).astype(BF16)
    acc = _dot(a_hi, w_hi) + _dot(a_lo, w_hi) + _dot(a_hi, w_lo)
    o_ref[...] = acc + b_ref[...]


def _modulation(c8, w_mod, b_mod):
    d, n = w_mod.shape
    tn = 1024
    return pl.pallas_call(
        _mod_kernel,
        grid=(n // tn,),
        in_specs=[pl.BlockSpec((SUBLANES, d), lambda j: (0, 0)),
                  pl.BlockSpec((d, tn), lambda j: (0, j)),
                  pl.BlockSpec((1, tn), lambda j: (0, j))],
        out_specs=pl.BlockSpec((SUBLANES, tn), lambda j: (0, j)),
        out_shape=jax.ShapeDtypeStruct((SUBLANES, n), F32),
        name="modulation",
        compiler_params=_cp(("arbitrary",)),
    )(c8, w_mod, b_mod.reshape(1, n))


def _pro_kernel(x_ref, sh_ref, sc_ref, g_ref, win_ref, bdcs_ref, nq_ref, nkv_ref, wqa_ref, wqb_ref,
                wk_ref, wvt_ref, t1_ref, t2_ref, q_ref, k_ref, vt_ref, za_ref, zb_ref, *, scale):
    x = x_ref[0]
    hx = _rms(x) * g_ref[...] * (1.0 + sc_ref[0]) + sh_ref[0]
    px = _dot(hx.astype(BF16), win_ref[...])
    u = px[:, 0:F_WIDTH].astype(BF16)
    z = _dot(u, bdcs_ref[...])
    za_ref[0] = z[:, :F_WIDTH].astype(BF16)
    zb_ref[0] = z[:, F_WIDTH:].astype(BF16)
    o_q = F_WIDTH
    o_kv = o_q + Q_LORA
    o_ra = o_kv + KV_LORA
    o_rb = o_ra + ROPE_W
    t1 = t1_ref[...]
    t2 = t2_ref[...]
    qn = (_rms(px[:, o_q:o_kv]) * nq_ref[...]).astype(BF16)
    qa = _dot(qn, wqa_ref[...])
    qb = _dot(qn, wqb_ref[...])
    for h in range(N_HEADS):
        nope = qa[:, h * HEAD_W:h * HEAD_W + QK_NOPE] * scale
        rope = (qa[:, h * HEAD_W + QK_NOPE:(h + 1) * HEAD_W] * t1
                + qb[:, h * ROPE_W:(h + 1) * ROPE_W] * t2) * scale
        q_ref[0, h, :, 0:QK_NOPE] = nope.astype(BF16)
        q_ref[0, h, :, QK_NOPE:HEAD_W] = rope.astype(BF16)
    kvn = (_rms(px[:, o_kv:o_ra]) * nkv_ref[...]).astype(BF16)
    kn = _dot(kvn, wk_ref[...])
    kr = (px[:, o_ra:o_rb] * t1 + px[:, o_rb:o_rb + ROPE_W] * t2).astype(BF16)
    for h in range(N_HEADS):
        k_ref[0, h, :, 0:QK_NOPE] = kn[:, h * QK_NOPE:(h + 1) * QK_NOPE].astype(BF16)
        k_ref[0, h, :, QK_NOPE:HEAD_W] = kr
    vt = _dot_nt(wvt_ref[...], kvn)
    for h in range(N_HEADS):
        vt_ref[0, h, 0] = vt[h * V_DIM:(h + 1) * V_DIM, :].astype(BF16)


def _prologue(x, m3, mod_row, g_pre, wts, t1, t2, tm, scale):
    b, n, d = x.shape
    nt = n // tm
    win, bdcs, nq, nkv, wqa, wqb, wk, wvt = wts
    full = lambda a: pl.BlockSpec(a.shape, lambda bi, i: (0,) * a.ndim)
    kern = functools.partial(_pro_kernel, scale=scale)
    return pl.pallas_call(
        kern,
        grid=(b, nt),
        in_specs=[pl.BlockSpec((1, tm, d), lambda bi, i: (bi, i, 0)),
                  pl.BlockSpec((1, 1, d), lambda bi, i: (mod_row(bi) * N_MOD + 0, 0, 0)),
                  pl.BlockSpec((1, 1, d), lambda bi, i: (mod_row(bi) * N_MOD + 1, 0, 0)),
                  full(g_pre), full(win), full(bdcs), full(nq), full(nkv), full(wqa), full(wqb),
                  full(wk), full(wvt),
                  pl.BlockSpec((tm, ROPE_W), lambda bi, i: (i, 0)),
                  pl.BlockSpec((tm, ROPE_W), lambda bi, i: (i, 0))],
        out_specs=[pl.BlockSpec((1, N_HEADS, tm, HEAD_W), lambda bi, i: (bi, 0, i, 0)),
                   pl.BlockSpec((1, N_HEADS, tm, HEAD_W), lambda bi, i: (bi, 0, i, 0)),
                   pl.BlockSpec((1, N_HEADS, 1, V_DIM, tm), lambda bi, i: (bi, 0, i, 0, 0)),
                   pl.BlockSpec((1, tm, F_WIDTH), lambda bi, i: (bi, i, 0)),
                   pl.BlockSpec((1, tm, F_WIDTH), lambda bi, i: (bi, i, 0))],
        out_shape=[jax.ShapeDtypeStruct((b, N_HEADS, n, HEAD_W), BF16),
                   jax.ShapeDtypeStruct((b, N_HEADS, n, HEAD_W), BF16),
                   jax.ShapeDtypeStruct((b, N_HEADS, nt, V_DIM, tm), BF16),
                   jax.ShapeDtypeStruct((b, n, F_WIDTH), BF16),
                   jax.ShapeDtypeStruct((b, n, F_WIDTH), BF16)],
        name="prologue",
        compiler_params=_cp(("arbitrary", "arbitrary")),
    )(x, m3, m3, g_pre, win, bdcs, nq, nkv, wqa, wqb, wk, wvt, t1, t2)


def _attn_kernel(q_ref, k_ref, vt_ref, kc_ref, vtc_ref, o_ref, kmax_sc, m_sc, l_sc, acc_sc,
                 *, nkv, tk, group_size):
    q = q_ref[0, 0]
    ones_kw = jnp.ones((HEAD_W, LANES), BF16)

    def sq_norms(kb):
        kf = kb.astype(F32)
        return jnp.max(_dot((kf * kf).astype(BF16), ones_kw), axis=0, keepdims=True)

    @pl.when(pl.program_id(2) == 0)
    def _():
        def body(j, best):
            start = pl.multiple_of(j * tk, tk)
            return jnp.maximum(best, sq_norms(k_ref[0, 0, pl.ds(start, tk), :]))

        kmax_sc[...] = lax.fori_loop(0, nkv, body, sq_norms(kc_ref[0, 0]))

    qf = q.astype(F32)
    qn2 = _dot_nt(jnp.ones((SUBLANES, HEAD_W), BF16), (qf * qf).astype(BF16))[0:1]
    ref = jnp.sqrt(qn2 * kmax_sc[:, 0:1]) * NORM_SLACK

    def fast_chunk(kb, vtb):
        p = jnp.exp2(_dot_nt(kb, q) - ref)
        l_sc[...] += jnp.sum(p, axis=0, keepdims=True)
        acc_sc[...] += _dot(vtb, p.astype(BF16))

    l_sc[...] = jnp.zeros_like(l_sc)
    acc_sc[...] = jnp.zeros_like(acc_sc)
    fast_chunk(kc_ref[0, 0], vtc_ref[0, 0, 0])

    def fast_body(i, carry):
        for u in range(group_size):
            j = group_size * i + u
            start = pl.multiple_of(j * tk, tk)
            fast_chunk(k_ref[0, 0, pl.ds(start, tk), :], vt_ref[0, 0, j])
        return carry

    lax.fori_loop(0, nkv // group_size, fast_body, 0)

    @pl.when(jnp.min(l_sc[...]) < MIN_ROW_SUM)
    def _():
        s = _dot_nt(kc_ref[0, 0], q)
        m0 = jnp.max(s, axis=0, keepdims=True)
        p = jnp.exp2(s - m0)
        m_sc[...] = m0
        l_sc[...] = jnp.sum(p, axis=0, keepdims=True)
        acc_sc[...] = _dot(vtc_ref[0, 0, 0], p.astype(BF16))

        def body(j, carry):
            start = pl.multiple_of(j * tk, tk)
            s = _dot_nt(k_ref[0, 0, pl.ds(start, tk), :], q)
            m_prev = m_sc[...]
            m_new = jnp.maximum(m_prev, jnp.max(s, axis=0, keepdims=True))
            alpha = jnp.exp2(m_prev - m_new)
            p = jnp.exp2(s - m_new)
            l_sc[...] = alpha * l_sc[...] + jnp.sum(p, axis=0, keepdims=True)
            acc_sc[...] = alpha * acc_sc[...] + _dot(vt_ref[0, 0, j], p.astype(BF16))
            m_sc[...] = m_new
            return carry

        lax.fori_loop(0, nkv, body, 0)

    o = acc_sc[...] / l_sc[...]
    o_ref[0] = o.T.astype(BF16)


def _attention(q, k, vt, kc, vtc, tq):
    b, h, n, _ = q.shape
    nkv, tk = vt.shape[2], vt.shape[4]
    nctx = kc.shape[2]
    group_size = next(g for g in (8, 4, 2, 1) if nkv % g == 0)
    kern = functools.partial(_attn_kernel, nkv=nkv, tk=tk, group_size=group_size)
    return pl.pallas_call(
        kern,
        grid=(b, h, n // tq),
        in_specs=[pl.BlockSpec((1, 1, tq, HEAD_W), lambda bi, hi, i: (bi, hi, i, 0)),
                  pl.BlockSpec((1, 1, n, HEAD_W), lambda bi, hi, i: (bi, hi, 0, 0)),
                  pl.BlockSpec((1, 1, nkv, V_DIM, tk), lambda bi, hi, i: (bi, hi, 0, 0, 0)),
                  pl.BlockSpec((1, 1, nctx, HEAD_W), lambda bi, hi, i: (bi, hi, 0, 0)),
                  pl.BlockSpec((1, 1, 1, V_DIM, nctx), lambda bi, hi, i: (bi, hi, 0, 0, 0))],
        out_specs=pl.BlockSpec((1, tq, V_DIM), lambda bi, hi, i: (bi, i, hi)),
        out_shape=jax.ShapeDtypeStruct((b, n, h * V_DIM), BF16),
        scratch_shapes=[pltpu.VMEM((1, LANES), F32),
                        pltpu.VMEM((1, tq), F32), pltpu.VMEM((1, tq), F32),
                        pltpu.VMEM((V_DIM, tq), F32)],
        name="attention",
        compiler_params=_cp(("arbitrary", "arbitrary", "arbitrary")),
    )(q, k, vt, kc, vtc)


def _fft_a_kernel(za_ref, zb_ref, w1_ref, tc_ref, ts_ref, yr_ref, yi_ref):
    ab = jnp.concatenate([za_ref[0], zb_ref[0]], axis=0)
    y = _dot(w1_ref[...], ab)
    re = y[:FFT_N1]
    im = y[FFT_N1:]
    tc = tc_ref[...]
    ts = ts_ref[...]
    yr_ref[0] = (re * tc + im * ts).astype(BF16)
    yi_ref[0] = (im * tc - re * ts).astype(BF16)


def _fft_b_kernel(yr_ref, yi_ref, m_ref, bdw_ref, o_ref):
    ri = jnp.concatenate([yr_ref[0], yi_ref[0]], axis=0)
    f = _dot(m_ref[...], ri)
    four = _dot(f.astype(BF16), bdw_ref[...])
    o_ref[0] = four.reshape(o_ref.shape[1:])


def _fourier(za, zb, w1, tc, ts, mmat, bdw):
    b, n, c = za.shape
    n2 = n // FFT_N1
    wide = n2 * c
    tn = min(2048, wide)
    za2 = za.reshape(b, FFT_N1, wide)
    zb2 = zb.reshape(b, FFT_N1, wide)
    yr, yi = pl.pallas_call(
        _fft_a_kernel,
        grid=(wide // tn, b),
        in_specs=[pl.BlockSpec((1, FFT_N1, tn), lambda i, bi: (bi, 0, i)),
                  pl.BlockSpec((1, FFT_N1, tn), lambda i, bi: (bi, 0, i)),
                  pl.BlockSpec(w1.shape, lambda i, bi: (0, 0)),
                  pl.BlockSpec((FFT_N1, tn), lambda i, bi: (0, i)),
                  pl.BlockSpec((FFT_N1, tn), lambda i, bi: (0, i))],
        out_specs=[pl.BlockSpec((1, FFT_N1, tn), lambda i, bi: (bi, 0, i)),
                   pl.BlockSpec((1, FFT_N1, tn), lambda i, bi: (bi, 0, i))],
        out_shape=[jax.ShapeDtypeStruct((b, FFT_N1, wide), BF16)] * 2,
        name="fourier_a",
        compiler_params=_cp(("arbitrary", "arbitrary")),
    )(za2, zb2, w1, tc, ts)
    rows = FFT_K1_TILE * n2
    yr = yr.reshape(b, n, c)
    yi = yi.reshape(b, n, c)
    out = pl.pallas_call(
        _fft_b_kernel,
        grid=(b, FFT_N1 // FFT_K1_TILE),
        in_specs=[pl.BlockSpec((1, rows, c), lambda bi, i: (bi, i, 0)),
                  pl.BlockSpec((1, rows, c), lambda bi, i: (bi, i, 0)),
                  pl.BlockSpec(mmat.shape, lambda bi, i: (0, 0)),
                  pl.BlockSpec(bdw.shape, lambda bi, i: (0, 0))],
        out_specs=pl.BlockSpec((1, n2, FFT_K1_TILE, c), lambda bi, i: (bi, 0, i, 0)),
        out_shape=jax.ShapeDtypeStruct((b, n2, FFT_N1, c), F32),
        name="fourier_b",
        compiler_params=_cp(("arbitrary", "arbitrary")),
    )(yr, yi, mmat, bdw)
    return out.reshape(b, n, c)


def _post_kernel(att_ref, four_ref, x_ref, g1_ref, sh2_ref, sc2_ref, gpost_ref, gpre_ref, woa_ref, wof_ref,
                 wrt_ref, br_ref, tri_ref, x1_ref, hx_ref, ridx_ref, gates_ref, cnt_ref, carry_sc):
    first = (pl.program_id(0) == 0) & (pl.program_id(1) == 0)

    @pl.when(first)
    def _():
        carry_sc[...] = jnp.zeros_like(carry_sc)

    mix = _dot(att_ref[0], woa_ref[...]) + _dot(four_ref[0].astype(BF16), wof_ref[...])
    x1 = x_ref[0] + g1_ref[0] * (_rms(mix) * gpost_ref[...])
    x1_ref[0] = x1
    hx = _rms(x1) * gpre_ref[...] * (1.0 + sc2_ref[0]) + sh2_ref[0]
    hx_ref[...] = hx
    lg =_dot_nt(wrt_ref[...], hx.astype(BF16)) + br_ref[...]
    iota_e = lax.broadcasted_iota(I32, lg.shape, 0)
    cur = lg
    tops, idxs, ohs = [], [], []
    for _ in range(TOP_K):
        mx = jnp.max(cur, axis=0, keepdims=True)
        idx = jnp.min(jnp.where(cur == mx, iota_e, N_EXPERTS), axis=0, keepdims=True)
        oh = iota_e == idx
        cur = jnp.where(oh, -jnp.inf, cur)
        tops.append(mx)
        idxs.append(idx)
        ohs.append(oh)
    es = [jnp.exp(t - tops[0]) for t in tops]
    den = es[0] + es[1] + es[2] + es[3]
    gates = [e / den for e in es]
    ohsum = ohs[0].astype(F32) + ohs[1].astype(F32) + ohs[2].astype(F32) + ohs[3].astype(F32)
    base = carry_sc[:, 0:1] + _dot(ohsum.astype(BF16), tri_ref[...])
    ranks = [jnp.sum(jnp.where(oh, base, 0.0), axis=0, keepdims=True).astype(I32) for oh in ohs]
    ridx_ref[...] = _stack_rows(idxs + ranks, 2 * TOP_K)
    gates_ref[...] = _stack_rows(gates, LANES).T
    carry_sc[...] = carry_sc[...] + jnp.sum(ohsum, axis=1, keepdims=True)
    cnt_ref[...] = carry_sc[...]


def _post_attention(att, four, x, m3, gpost, gpre, woa, wof, wrt, br, tri, tm):
    b, n, d = x.shape
    nt = n // tm
    t = b * n
    full = lambda a: pl.BlockSpec(a.shape, lambda bi, i: (0,) * a.ndim)
    mrow = lambda j: pl.BlockSpec((1, 1, d), lambda bi, i: (bi * N_MOD + j, 0, 0))
    return pl.pallas_call(
        _post_kernel,
        grid=(b, nt),
        in_specs=[pl.BlockSpec((1, tm, att.shape[2]), lambda bi, i: (bi, i, 0)),
                  pl.BlockSpec((1, tm, F_WIDTH), lambda bi, i: (bi, i, 0)),
                  pl.BlockSpec((1, tm, d), lambda bi, i: (bi, i, 0)),
                  mrow(2), mrow(3), mrow(4),
                  full(gpost), full(gpre), full(woa), full(wof), full(wrt), full(br), full(tri)],
        out_specs=[pl.BlockSpec((1, tm, d), lambda bi, i: (bi, i, 0)),
                   pl.BlockSpec((tm, d), lambda bi, i: (bi * nt + i, 0)),
                   pl.BlockSpec((2 * TOP_K, tm), lambda bi, i: (0, bi * nt + i)),
                   pl.BlockSpec((tm, LANES), lambda bi, i: (bi * nt + i, 0)),
                   pl.BlockSpec((N_EXPERTS, LANES), lambda bi, i: (0, 0))],
        out_shape=[jax.ShapeDtypeStruct((b, n, d), F32),
                   jax.ShapeDtypeStruct((t, d), F32),
                   jax.ShapeDtypeStruct((2 * TOP_K, t), I32),
                   jax.ShapeDtypeStruct((t, LANES), F32),
                   jax.ShapeDtypeStruct((N_EXPERTS, LANES), F32)],
        scratch_shapes=[pltpu.VMEM((N_EXPERTS, LANES), F32)],
        name="post_attention",
        compiler_params=_cp(("arbitrary", "arbitrary")),
    )(att, four, x, m3, m3, m3, gpost, gpre, woa, wof, wrt, br, tri)


def _dispatch_kernel(fill_ref, dest_ref, hx_ref, xs_hbm, zrow, sem, fsem, *, tm, bm):
    i = pl.program_id(0)

    @pl.when(i == 0)
    def _():
        zrow[...] = jnp.zeros_like(zrow)
        nblk = xs_hbm.shape[0] // bm

        def tail(blk, c):
            cp = pltpu.make_async_copy(zrow, xs_hbm.at[pl.ds(blk * bm, bm), :], fsem)
            cp.start()
            cp.wait()
            return c

        lax.fori_loop(fill_ref[2 * N_EXPERTS], nblk, tail, 0)

        def per_expert(e, c):
            s = fill_ref[e]
            n = fill_ref[N_EXPERTS + e]
            head = jnp.minimum(n, (-s) & (SUBLANES - 1))
            body = s + head
            nchunk = (n - head) // SUBLANES

            def row_copy(r):
                return pltpu.make_async_copy(zrow.at[pl.ds(0, 1), :], xs_hbm.at[pl.ds(s + r, 1), :], fsem)

            def chunk_copy(j):
                dst = pl.multiple_of(body + j * SUBLANES, SUBLANES)
                return pltpu.make_async_copy(zrow.at[pl.ds(0, SUBLANES), :], xs_hbm.at[pl.ds(dst, SUBLANES), :], fsem)

            def loop(n_it, copy, action):
                def step(r, cc):
                    action(copy(r))
                    return cc

                lax.fori_loop(0, n_it, step, 0)

            loop(head, row_copy, lambda cp: cp.start())
            loop(nchunk, chunk_copy, lambda cp: cp.start())
            loop(head, row_copy, lambda cp: cp.wait())
            loop(nchunk, chunk_copy, lambda cp: cp.wait())
            return c

        lax.fori_loop(0, N_EXPERTS, per_expert, 0)

    def row_copy(g, u, k):
        d = dest_ref[0, 0, k * tm + g * SUBLANES + u]
        return pltpu.make_async_copy(hx_ref.at[g, pl.ds(u, 1), :], xs_hbm.at[pl.ds(d, 1), :], sem)

    def start(g, c):
        for u in range(SUBLANES):
            for k in range(TOP_K):
                row_copy(g, u, k).start()
        return c

    def wait(g, c):
        for u in range(SUBLANES):
            for k in range(TOP_K):
                row_copy(g, u, k).wait()
        return c

    lax.fori_loop(0, tm // SUBLANES, start, 0)
    lax.fori_loop(0, tm // SUBLANES, wait, 0)


def _dispatch(fill, dest_tiles, hx, n_slots, tm, bm):
    t, dm = hx.shape
    kern = functools.partial(_dispatch_kernel, tm=tm, bm=bm)
    return pl.pallas_call(
        kern,
        grid_spec=pltpu.PrefetchScalarGridSpec(
            num_scalar_prefetch=1,
            grid=(t // tm,),
            in_specs=[pl.BlockSpec((1, 1, TOP_K * tm), lambda i, f: (i, 0, 0), memory_space=pltpu.SMEM),
                      pl.BlockSpec((tm // SUBLANES, SUBLANES, dm), lambda i, f: (i, 0, 0))],
            out_specs=pl.BlockSpec(memory_space=pl.ANY),
            scratch_shapes=[pltpu.VMEM((bm, dm), F32),
                            pltpu.SemaphoreType.DMA(()), pltpu.SemaphoreType.DMA(())]),
        out_shape=jax.ShapeDtypeStruct((n_slots, dm), F32),
        name="dispatch",
        compiler_params=_cp(("arbitrary",)),
    )(fill, dest_tiles, hx.reshape(t // SUBLANES, SUBLANES, dm))


def _moe_kernel(be_ref, nv_ref, x_ref, wg_ref, bg_ref, wu_ref, bu_ref, wd_ref, bd_ref, y_ref, wgb, wub, wdb):
    i = pl.program_id(0)

    @pl.when(i < nv_ref[0])
    def _():
        e = be_ref[i]
        prev = be_ref[jnp.maximum(i - 1, 0)]

        @pl.when((i == 0) | (e != prev))
        def _():
            wgb[...] = wg_ref[0].astype(BF16)
            wub[...] = wu_ref[0].astype(BF16)
            wdb[...] = wd_ref[0].astype(BF16)

        xb = x_ref[...].astype(BF16)
        g = jnp.minimum(_dot(xb, wgb[...]) + bg_ref[0], SWIGLU_LIMIT)
        u = jnp.clip(_dot(xb, wub[...]) + bu_ref[0], -SWIGLU_LIMIT, SWIGLU_LIMIT)
        act = g * (1.0 / (1.0 + jnp.exp(-SWIGLU_ALPHA * g))) * (u + 1.0)
        y_ref[...] = _dot(act.astype(BF16), wdb[...]) + bd_ref[0]

    @pl.when(i >= nv_ref[0])
    def _():
        y_ref[...] = jnp.zeros_like(y_ref)


def _experts(block_e, nvalid, xs, wg, bg, wu, bu, wd, bd, bm):
    n_slots = xs.shape[0]
    e, d, f = wg.shape
    nb = n_slots // bm
    xmap = lambda i, be, nv: (jnp.minimum(i, nv[0] - 1), 0)
    wmap = lambda i, be, nv: (be[i], 0, 0)
    return pl.pallas_call(
        _moe_kernel,
        grid_spec=pltpu.PrefetchScalarGridSpec(
            num_scalar_prefetch=2,
            grid=(nb,),
            in_specs=[pl.BlockSpec((bm, d), xmap),
                      pl.BlockSpec((1, d, f), wmap), pl.BlockSpec((1, 1, f), wmap),
                      pl.BlockSpec((1, d, f), wmap), pl.BlockSpec((1, 1, f), wmap),
                      pl.BlockSpec((1, f, d), wmap), pl.BlockSpec((1, 1, d), wmap)],
            out_specs=pl.BlockSpec((bm, d), lambda i, be, nv: (i, 0)),
            scratch_shapes=[pltpu.VMEM((d, f), BF16), pltpu.VMEM((d, f), BF16), pltpu.VMEM((f, d), BF16)]),
        out_shape=jax.ShapeDtypeStruct((n_slots, d), F32),
        name="experts",
        compiler_params=_cp(("arbitrary",)),
    )(block_e, nvalid, xs, wg, bg.reshape(e, 1, f), wu, bu.reshape(e, 1, f), wd, bd.reshape(e, 1, d))


def _combine_kernel(dcur_ref, dnxt_ref, ys_hbm, gates_ref, x1_ref, g2_ref, gpost_ref, o_ref, buf, sem,
                    *, tm, nsteps):
    i = pl.program_id(0)
    slot = i % 2

    def row_copy(dref, sl, g, u, k):
        d = dref[0, 0, k * tm + g * SUBLANES + u]
        return pltpu.make_async_copy(ys_hbm.at[pl.ds(d, 1), :], buf.at[sl, k, g, pl.ds(u, 1), :], sem.at[sl])

    def issue(dref, sl):
        def start(g, c):
            for u in range(SUBLANES):
                for k in range(TOP_K):
                    row_copy(dref, sl, g, u, k).start()
            return c

        lax.fori_loop(0, tm // SUBLANES, start, 0)

    @pl.when(i == 0)
    def _():
        issue(dcur_ref, 0)

    @pl.when(i + 1 < nsteps)
    def _():
        issue(dnxt_ref, 1 - slot)

    def wait(g, c):
        for u in range(SUBLANES):
            for k in range(TOP_K):
                row_copy(dcur_ref, slot, g, u, k).wait()
        return c

    lax.fori_loop(0, tm // SUBLANES, wait, 0)
    gt = gates_ref[...]
    rows = lambda k: buf[slot, k].reshape(tm, buf.shape[-1])
    y = gt[:, 0:1] * rows(0)
    for k in range(1, TOP_K):
        y = y + gt[:, k:k + 1] * rows(k)
    o_ref[...] = x1_ref[...] + g2_ref[0] * (_rms(y) * gpost_ref[...])


def _combine(dest_tiles, ys, gates_t, x1, m3, gpost, tm):
    b, n, d = x1.shape
    nt = n // tm
    nsteps = b * nt
    kern = functools.partial(_combine_kernel, tm=tm, nsteps=nsteps)
    dspec = lambda f: pl.BlockSpec((1, 1, TOP_K * tm), f, memory_space=pltpu.SMEM)
    out = pl.pallas_call(
        kern,
        grid=(nsteps,),
        in_specs=[dspec(lambda i: (i, 0, 0)),
                  dspec(lambda i: (jnp.minimum(i + 1, nsteps - 1), 0, 0)),
                  pl.BlockSpec(memory_space=pl.ANY),
                  pl.BlockSpec((tm, LANES), lambda i: (i, 0)),
                  pl.BlockSpec((tm, d), lambda i: (i, 0)),
                  pl.BlockSpec((1, 1, d), lambda i: (i // nt * N_MOD + 5, 0, 0)),
                  pl.BlockSpec(gpost.shape, lambda i: (0, 0))],
        out_specs=pl.BlockSpec((tm, d), lambda i: (i, 0)),
        out_shape=jax.ShapeDtypeStruct((b * n, d), F32),
        scratch_shapes=[pltpu.VMEM((2, TOP_K, tm // SUBLANES, SUBLANES, d), F32),
                        pltpu.SemaphoreType.DMA((2,))],
        name="combine",
        compiler_params=_cp(("arbitrary",)),
    )(dest_tiles, dest_tiles, ys, gates_t, x1.reshape(b * n, d), m3, gpost)
    return out.reshape(b, n, d)


def _rope_tables(n):
    rows = n // GRID_W
    row = jnp.repeat(jnp.arange(rows, dtype=F32), GRID_W)
    col = jnp.tile(jnp.arange(GRID_W, dtype=F32), rows)
    half = QK_ROPE // 2
    inv = 1.0 / (ROPE_BASE ** (jnp.arange(0, half, 2, dtype=F32) / half))
    ang = jnp.concatenate([row[:, None] * inv, col[:, None] * inv], axis=-1)
    cos, sin = jnp.cos(ang), jnp.sin(ang)
    zero = jnp.zeros((n, ROPE_W - QK_ROPE), F32)
    return jnp.concatenate([cos, cos, zero], axis=1), jnp.concatenate([sin, sin, zero], axis=1)


def _ctx_tables(n):
    half = QK_ROPE // 2
    t1 = np.zeros((n, ROPE_W), np.float32)
    t1[:, :2 * half] = 1.0
    return jnp.asarray(t1), jnp.zeros((n, ROPE_W), F32)


def _prep_weights(w_in, norm_q, norm_kv, w_q_up, w_kv_up):
    d = w_in.shape[0]
    half = QK_ROPE // 2
    zr = lambda r, c: jnp.zeros((r, c), F32)
    o_rope = F_WIDTH + Q_LORA + KV_LORA
    kr = w_in[:, o_rope:o_rope + QK_ROPE]
    kre, kro = kr[:, 0::2], kr[:, 1::2]
    win = jnp.concatenate([w_in[:, :o_rope],
                           kre, kro, zr(d, ROPE_W - QK_ROPE),
                           -kro, kre, zr(d, ROPE_W - QK_ROPE)], axis=1).astype(BF16)
    hd = QK_NOPE + QK_ROPE
    qa, qb = [], []
    for h in range(N_HEADS):
        wn = w_q_up[:, h * hd:h * hd + QK_NOPE]
        wr = w_q_up[:, h * hd + QK_NOPE:(h + 1) * hd]
        we, wo = wr[:, 0::2], wr[:, 1::2]
        qa += [wn, we, wo, zr(Q_LORA, ROPE_W - QK_ROPE)]
        qb += [-wo, we, zr(Q_LORA, ROPE_W - QK_ROPE)]
    wqa = jnp.concatenate(qa, axis=1).astype(BF16)
    wqb = jnp.concatenate(qb, axis=1).astype(BF16)
    kvd = QK_NOPE + V_DIM
    wk = jnp.concatenate([w_kv_up[:, h * kvd:h * kvd + QK_NOPE] for h in range(N_HEADS)], axis=1).astype(BF16)
    wv = jnp.concatenate([w_kv_up[:, h * kvd + QK_NOPE:(h + 1) * kvd] for h in range(N_HEADS)], axis=1)
    wvt = wv.T.astype(BF16)
    return win, norm_q.reshape(1, -1), norm_kv.reshape(1, -1), wqa, wqb, wk, wvt


def _fourier_consts(n, w_fourier):
    c = F_GDIM
    j = np.arange(c)
    ang = 2.0 * np.pi * np.outer(j, j) / c
    eye = np.eye(F_GROUPS)
    bdc = np.kron(eye, np.cos(ang))
    bds = np.kron(eye, np.sin(ang))
    bdcs = jnp.asarray(np.concatenate([bdc, -bds], axis=1), dtype=BF16)
    n1 = FFT_N1
    n2 = n // n1
    k1 = np.arange(n1)
    a1 = 2.0 * np.pi * np.outer(k1, k1) / n1
    wc, ws = np.cos(a1), np.sin(a1)
    w1 = jnp.asarray(np.block([[wc, ws], [-ws, wc]]), dtype=BF16)
    tw = 2.0 * np.pi * np.outer(k1, np.arange(n2)) / n
    tc = jnp.repeat(jnp.asarray(np.cos(tw), dtype=F32), F_WIDTH, axis=1)
    ts = jnp.repeat(jnp.asarray(np.sin(tw), dtype=F32), F_WIDTH, axis=1)
    k2 = np.arange(n2)
    a2 = 2.0 * np.pi * np.outer(k2, k2) / n2
    norm = 1.0 / math.sqrt(n * c)
    kt = FFT_K1_TILE
    mc = np.zeros((n2, kt, kt, n2))
    ms = np.zeros((n2, kt, kt, n2))
    for r in range(kt):
        mc[:, r, r, :] = np.cos(a2) * norm
        ms[:, r, r, :] = np.sin(a2) * norm
    mmat = np.concatenate([mc.reshape(n2 * kt, kt * n2), ms.reshape(n2 * kt, kt * n2)], axis=1)
    mmat = jnp.asarray(mmat, dtype=BF16)
    zblk = jnp.zeros((c, c), F32)
    bdw = jnp.concatenate(
        [jnp.concatenate([w_fourier[g] if j == g else zblk for j in range(F_GROUPS)], axis=1)
         for g in range(F_GROUPS)], axis=0)
    return bdcs, w1, tc, ts, mmat, bdw.astype(BF16)


def kernel(x, c, ctx, c_ctx, w_mod, b_mod, norm_attn_pre, norm_attn_post, norm_ffn_pre, norm_ffn_post, w_in, norm_q_lat, norm_kv_lat, w_q_up, w_kv_up, w_fourier, w_out, w_router, b_router, w_gate, b_gate, w_up, b_up, w_down, b_down):
    b, n, d = x.shape
    nctx = ctx.shape[1]
    t = b * n
    assert w_mod.shape[0] == 1 and b + 1 <= SUBLANES and n % (FFT_N1 * SUBLANES) == 0
    row2 = lambda a: a[0].reshape(1, -1)

    c8 = jnp.concatenate([c, c_ctx[None, :], jnp.zeros((SUBLANES - b - 1, d), F32)], axis=0)
    m3 = _modulation(c8, w_mod[0], b_mod[0]).reshape(SUBLANES * N_MOD, 1, d)

    wts = _prep_weights(w_in[0], norm_q_lat[0], norm_kv_lat[0], w_q_up[0], w_kv_up[0])
    bdcs, w1, tc, ts, mmat, bdw = _fourier_consts(n, w_fourier[0])
    pro_w = (wts[0], bdcs) + wts[1:]
    scale = (QK_NOPE + QK_ROPE) ** -0.5 * LOG2E
    g_pre = row2(norm_attn_pre)

    tm = min(512, n)
    t1x, t2x = _rope_tables(n)
    q, k, vt, za, zb = _prologue(x, m3, lambda bi: bi, g_pre, pro_w, t1x, t2x, tm, scale)
    t1c, t2c = _ctx_tables(nctx)
    _, kc, vtc, _, _ = _prologue(ctx, m3, lambda bi: b, g_pre, pro_w, t1c, t2c, nctx, scale)

    att = _attention(q, k, vt, kc, vtc, min(1024, n))
    four = _fourier(za, zb, w1, tc, ts, mmat, bdw)

    tmp = min(512, n)
    tri = jnp.asarray(np.triu(np.ones((tmp, tmp), np.float32), 1), dtype=BF16)
    width = N_HEADS * V_DIM
    woa = w_out[0, :width].astype(BF16)
    wof = w_out[0, width:].astype(BF16)
    x1, hx2, ridx, gates_t, cnt = _post_attention(
        att, four, x, m3, row2(norm_attn_post), row2(norm_ffn_pre), woa, wof,
        w_router[0].T.astype(BF16), b_router[0].reshape(-1, 1), tri, tmp)

    bm = 512
    counts = cnt[:, 0].astype(I32)
    padded = (counts + bm - 1) // bm * bm
    pad_end = jnp.cumsum(padded)
    pad_start = pad_end - padded
    nb = t * TOP_K // bm + N_EXPERTS
    blk_start = jnp.arange(nb, dtype=I32)[:, None] * bm
    block_e = jnp.minimum(jnp.sum((pad_end[None, :] <= blk_start).astype(I32), axis=1), N_EXPERTS - 1)
    nvalid = (pad_end[-1:] // bm).astype(I32)
    eids = jnp.arange(N_EXPERTS, dtype=I32)[:, None, None]
    dest = jnp.sum(jnp.where(ridx[None, :TOP_K] == eids, pad_start[:, None, None], 0), axis=0) + ridx[TOP_K:]
    fill = jnp.concatenate([pad_start + counts, padded - counts, nvalid]).astype(I32)

    def tiles(tile):
        return dest.reshape(TOP_K, t // tile, tile).transpose(1, 0, 2).reshape(t // tile, 1, TOP_K * tile)

    tmd = min(512, n)
    xs = _dispatch(fill, tiles(tmd), hx2, nb * bm, tmd, bm)
    ys = _experts(block_e, nvalid, xs, w_gate[0], b_gate[0], w_up[0], b_up[0], w_down[0], b_down[0], bm)
    tmc = 128
    return _combine(tiles(tmc), ys, gates_t, x1, m3, row2(norm_ffn_post), tmc)
```

```python
import functools
import math

import numpy as np
import jax
import jax.numpy as jnp
from jax import lax
from jax.experimental import pallas as pl
from jax.experimental.pallas import tpu as pltpu

F32 = jnp.float32
BF16 = jnp.bfloat16
I32 = jnp.int32

N_HEADS = 6
QK_NOPE = 128
QK_ROPE = 64
V_DIM = 128
Q_LORA = 384
KV_LORA = 256
F_GROUPS = 4
F_GDIM = 64
F_WIDTH = F_GROUPS * F_GDIM
GRID_W = 64
ROPE_BASE = 10000.0
N_EXPERTS = 32
TOP_K = 4
SWIGLU_LIMIT = 7.0
SWIGLU_ALPHA = 1.702
RMS_EPS = 1e-6
N_MOD = 6

LANES = 128
SUBLANES = 8
HEAD_W = 256
ROPE_W = 128
FFT_N1 = 128
FFT_K1_TILE = 8
VMEM_LIMIT = 56 * 1024 * 1024

LOG2E = 1.4426950408889634
NORM_SLACK = 1.0 + 2.0 ** -6
MIN_ROW_SUM = 2.0 ** -60


def _cp(sem, vmem=None):
    return pltpu.CompilerParams(dimension_semantics=sem, vmem_limit_bytes=vmem or VMEM_LIMIT)


def _rms(x):
    return x * lax.rsqrt(jnp.mean(x * x, axis=-1, keepdims=True) + RMS_EPS)


def _dot(a, b):
    return jnp.dot(a, b, preferred_element_type=F32)


def _stack_rows(rows, height):
    w = rows[0].shape[1]
    sub = lax.broadcasted_iota(I32, (height, w), 0)
    out = jnp.zeros((height, w), rows[0].dtype)
    for r, v in enumerate(rows):
        out = jnp.where(sub == r, v, out)
    return out


def _dot_nt(a, b):
    return lax.dot_general(a, b, (((1,), (1,)), ((), ())), preferred_element_type=F32)


def _mod_kernel(c_ref, w_ref, b_ref, o_ref):
    c = c_ref[...]
    a = c / (1.0 + jnp.exp(-c))
    a_hi = a.astype(BF16)
    a_lo = (a - a_hi.astype(F32)).astype(BF16)
    w = w_ref[...]
    w_hi = w.astype(BF16)
    w_lo = (w - w_hi.astype(F32)).astype(BF16)
    acc = _dot(a_hi, w_hi) + _dot(a_lo, w_hi) + _dot(a_hi, w_lo)
    o_ref[...] = acc + b_ref[...]


def _modulation(c8, w_mod, b_mod):
    d, n = w_mod.shape
    tn = 1024
    return pl.pallas_call(
        _mod_kernel,
        grid=(n // tn,),
        in_specs=[pl.BlockSpec((SUBLANES, d), lambda j: (0, 0)),
                  pl.BlockSpec((d, tn), lambda j: (0, j)),
                  pl.BlockSpec((1, tn), lambda j: (0, j))],
        out_specs=pl.BlockSpec((SUBLANES, tn), lambda j: (0, j)),
        out_shape=jax.ShapeDtypeStruct((SUBLANES, n), F32),
        name="modulation",
        compiler_params=_cp(("arbitrary",)),
    )(c8, w_mod, b_mod.reshape(1, n))


def _pro_kernel(x_ref, sh_ref, sc_ref, g_ref, win_ref, bdcs_ref, nq_ref, nkv_ref, wqa_ref, wqb_ref,
                wk_ref, wvt_ref, t1_ref, t2_ref, q_ref, k_ref, vt_ref, za_ref, zb_ref, *, scale):
    x = x_ref[0]
    hx = _rms(x) * g_ref[...] * (1.0 + sc_ref[0]) + sh_ref[0]
    px = _dot(hx.astype(BF16), win_ref[...])
    u = px[:, 0:F_WIDTH].astype(BF16)
    z = _dot(u, bdcs_ref[...])
    za_ref[0] = z[:, :F_WIDTH].astype(BF16)
    zb_ref[0] = z[:, F_WIDTH:].astype(BF16)
    o_q = F_WIDTH
    o_kv = o_q + Q_LORA
    o_ra = o_kv + KV_LORA
    o_rb = o_ra + ROPE_W
    t1 = t1_ref[...]
    t2 = t2_ref[...]
    qn = (_rms(px[:, o_q:o_kv]) * nq_ref[...]).astype(BF16)
    qa = _dot(qn, wqa_ref[...])
    qb = _dot(qn, wqb_ref[...])
    for h in range(N_HEADS):
        nope = qa[:, h * HEAD_W:h * HEAD_W + QK_NOPE] * scale
        rope = (qa[:, h * HEAD_W + QK_NOPE:(h + 1) * HEAD_W] * t1
                + qb[:, h * ROPE_W:(h + 1) * ROPE_W] * t2) * scale
        q_ref[0, h, :, 0:QK_NOPE] = nope.astype(BF16)
        q_ref[0, h, :, QK_NOPE:HEAD_W] = rope.astype(BF16)
    kvn = (_rms(px[:, o_kv:o_ra]) * nkv_ref[...]).astype(BF16)
    kn = _dot(kvn, wk_ref[...])
    kr = (px[:, o_ra:o_rb] * t1 + px[:, o_rb:o_rb + ROPE_W] * t2).astype(BF16)
    for h in range(N_HEADS):
        k_ref[0, h, :, 0:QK_NOPE] = kn[:, h * QK_NOPE:(h + 1) * QK_NOPE].astype(BF16)
        k_ref[0, h, :, QK_NOPE:HEAD_W] = kr
    vt = _dot_nt(wvt_ref[...], kvn)
    for h in range(N_HEADS):
        vt_ref[0, h, 0] = vt[h * V_DIM:(h + 1) * V_DIM, :].astype(BF16)


def _prologue(x, m3, mod_row, g_pre, wts, t1, t2, tm, scale):
    b, n, d = x.shape
    nt = n // tm
    win, bdcs, nq, nkv, wqa, wqb, wk, wvt = wts
    full = lambda a: pl.BlockSpec(a.shape, lambda bi, i: (0,) * a.ndim)
    kern = functools.partial(_pro_kernel, scale=scale)
    return pl.pallas_call(
        kern,
        grid=(b, nt),
        in_specs=[pl.BlockSpec((1, tm, d), lambda bi, i: (bi, i, 0)),
                  pl.BlockSpec((1, 1, d), lambda bi, i: (mod_row(bi) * N_MOD + 0, 0, 0)),
                  pl.BlockSpec((1, 1, d), lambda bi, i: (mod_row(bi) * N_MOD + 1, 0, 0)),
                  full(g_pre), full(win), full(bdcs), full(nq), full(nkv), full(wqa), full(wqb),
                  full(wk), full(wvt),
                  pl.BlockSpec((tm, ROPE_W), lambda bi, i: (i, 0)),
                  pl.BlockSpec((tm, ROPE_W), lambda bi, i: (i, 0))],
        out_specs=[pl.BlockSpec((1, N_HEADS, tm, HEAD_W), lambda bi, i: (bi, 0, i, 0)),
                   pl.BlockSpec((1, N_HEADS, tm, HEAD_W), lambda bi, i: (bi, 0, i, 0)),
                   pl.BlockSpec((1, N_HEADS, 1, V_DIM, tm), lambda bi, i: (bi, 0, i, 0, 0)),
                   pl.BlockSpec((1, tm, F_WIDTH), lambda bi, i: (bi, i, 0)),
                   pl.BlockSpec((1, tm, F_WIDTH), lambda bi, i: (bi, i, 0))],
        out_shape=[jax.ShapeDtypeStruct((b, N_HEADS, n, HEAD_W), BF16),
                   jax.ShapeDtypeStruct((b, N_HEADS, n, HEAD_W), BF16),
                   jax.ShapeDtypeStruct((b, N_HEADS, nt, V_DIM, tm), BF16),
                   jax.ShapeDtypeStruct((b, n, F_WIDTH), BF16),
                   jax.ShapeDtypeStruct((b, n, F_WIDTH), BF16)],
        name="prologue",
        compiler_params=_cp(("arbitrary", "arbitrary")),
    )(x, m3, m3, g_pre, win, bdcs, nq, nkv, wqa, wqb, wk, wvt, t1, t2)


def _attn_kernel(q_ref, k_ref, vt_ref, kc_ref, vtc_ref, o_ref, kmax_sc, m_sc, l_sc, acc_sc,
                 *, nkv, tk, group_size):
    q = q_ref[0, 0]
    ones_kw = jnp.ones((HEAD_W, LANES), BF16)

    def sq_norms(kb):
        kf = kb.astype(F32)
        return jnp.max(_dot((kf * kf).astype(BF16), ones_kw), axis=0, keepdims=True)

    @pl.when(pl.program_id(2) == 0)
    def _():
        def body(j, best):
            start = pl.multiple_of(j * tk, tk)
            return jnp.maximum(best, sq_norms(k_ref[0, 0, pl.ds(start, tk), :]))

        kmax_sc[...] = lax.fori_loop(0, nkv, body, sq_norms(kc_ref[0, 0]))

    qf = q.astype(F32)
    qn2 = _dot_nt(jnp.ones((SUBLANES, HEAD_W), BF16), (qf * qf).astype(BF16))[0:1]
    ref = jnp.sqrt(qn2 * kmax_sc[:, 0:1]) * NORM_SLACK

    def fast_chunk(kb, vtb):
        p = jnp.exp2(_dot_nt(kb, q) - ref)
        l_sc[...] += jnp.sum(p, axis=0, keepdims=True)
        acc_sc[...] += _dot(vtb, p.astype(BF16))

    l_sc[...] = jnp.zeros_like(l_sc)
    acc_sc[...] = jnp.zeros_like(acc_sc)
    fast_chunk(kc_ref[0, 0], vtc_ref[0, 0, 0])

    def fast_body(i, carry):
        for u in range(group_size):
            j = group_size * i + u
            start = pl.multiple_of(j * tk, tk)
            fast_chunk(k_ref[0, 0, pl.ds(start, tk), :], vt_ref[0, 0, j])
        return carry

    lax.fori_loop(0, nkv // group_size, fast_body, 0)

    @pl.when(jnp.min(l_sc[...]) < MIN_ROW_SUM)
    def _():
        s = _dot_nt(kc_ref[0, 0], q)
        m0 = jnp.max(s, axis=0, keepdims=True)
        p = jnp.exp2(s - m0)
        m_sc[...] = m0
        l_sc[...] = jnp.sum(p, axis=0, keepdims=True)
        acc_sc[...] = _dot(vtc_ref[0, 0, 0], p.astype(BF16))

        def body(j, carry):
            start = pl.multiple_of(j * tk, tk)
            s = _dot_nt(k_ref[0, 0, pl.ds(start, tk), :], q)
            m_prev = m_sc[...]
            m_new = jnp.maximum(m_prev, jnp.max(s, axis=0, keepdims=True))
            alpha = jnp.exp2(m_prev - m_new)
            p = jnp.exp2(s - m_new)
            l_sc[...] = alpha * l_sc[...] + jnp.sum(p, axis=0, keepdims=True)
            acc_sc[...] = alpha * acc_sc[...] + _dot(vt_ref[0, 0, j], p.astype(BF16))
            m_sc[...] = m_new
            return carry

        lax.fori_loop(0, nkv, body, 0)

    o = acc_sc[...] / l_sc[...]
    o_ref[0] = o.T.astype(BF16)


def _attention(q, k, vt, kc, vtc, tq):
    b, h, n, _ = q.shape
    nkv, tk = vt.shape[2], vt.shape[4]
    nctx = kc.shape[2]
    group_size = next(g for g in (16, 8, 4, 2, 1) if nkv % g == 0)
    kern = functools.partial(_attn_kernel, nkv=nkv, tk=tk, group_size=group_size)
    return pl.pallas_call(
        kern,
        grid=(b, h, n // tq),
        in_specs=[pl.BlockSpec((1, 1, tq, HEAD_W), lambda bi, hi, i: (bi, hi, i, 0)),
                  pl.BlockSpec((1, 1, n, HEAD_W), lambda bi, hi, i: (bi, hi, 0, 0)),
                  pl.BlockSpec((1, 1, nkv, V_DIM, tk), lambda bi, hi, i: (bi, hi, 0, 0, 0)),
                  pl.BlockSpec((1, 1, nctx, HEAD_W), lambda bi, hi, i: (bi, hi, 0, 0)),
                  pl.BlockSpec((1, 1, 1, V_DIM, nctx), lambda bi, hi, i: (bi, hi, 0, 0, 0))],
        out_specs=pl.BlockSpec((1, tq, V_DIM), lambda bi, hi, i: (bi, i, hi)),
        out_shape=jax.ShapeDtypeStruct((b, n, h * V_DIM), BF16),
        scratch_shapes=[pltpu.VMEM((1, LANES), F32),
                        pltpu.VMEM((1, tq), F32), pltpu.VMEM((1, tq), F32),
                        pltpu.VMEM((V_DIM, tq), F32)],
        name="attention",
        compiler_params=_cp(("arbitrary", "arbitrary", "arbitrary")),
    )(q, k, vt, kc, vtc)


def _fft_a_kernel(za_ref, zb_ref, w1_ref, tc_ref, ts_ref, yr_ref, yi_ref):
    ab = jnp.concatenate([za_ref[0], zb_ref[0]], axis=0)
    y = _dot(w1_ref[...], ab)
    re = y[:FFT_N1]
    im = y[FFT_N1:]
    tc = tc_ref[...]
    ts = ts_ref[...]
    yr_ref[0] = (re * tc + im * ts).astype(BF16)
    yi_ref[0] = (im * tc - re * ts).astype(BF16)


def _fft_b_kernel(yr_ref, yi_ref, m_ref, bdw_ref, o_ref):
    ri = jnp.concatenate([yr_ref[0], yi_ref[0]], axis=0)
    f = _dot(m_ref[...], ri)
    four = _dot(f.astype(BF16), bdw_ref[...])
    o_ref[0] = four.reshape(o_ref.shape[1:])


def _fourier(za, zb, w1, tc, ts, mmat, bdw):
    b, n, c = za.shape
    n2 = n // FFT_N1
    wide = n2 * c
    tn = min(2048, wide)
    za2 = za.reshape(b, FFT_N1, wide)
    zb2 = zb.reshape(b, FFT_N1, wide)
    yr, yi = pl.pallas_call(
        _fft_a_kernel,
        grid=(wide // tn, b),
        in_specs=[pl.BlockSpec((1, FFT_N1, tn), lambda i, bi: (bi, 0, i)),
                  pl.BlockSpec((1, FFT_N1, tn), lambda i, bi: (bi, 0, i)),
                  pl.BlockSpec(w1.shape, lambda i, bi: (0, 0)),
                  pl.BlockSpec((FFT_N1, tn), lambda i, bi: (0, i)),
                  pl.BlockSpec((FFT_N1, tn), lambda i, bi: (0, i))],
        out_specs=[pl.BlockSpec((1, FFT_N1, tn), lambda i, bi: (bi, 0, i)),
                   pl.BlockSpec((1, FFT_N1, tn), lambda i, bi: (bi, 0, i))],
        out_shape=[jax.ShapeDtypeStruct((b, FFT_N1, wide), BF16)] * 2,
        name="fourier_a",
        compiler_params=_cp(("arbitrary", "arbitrary")),
    )(za2, zb2, w1, tc, ts)
    rows = FFT_K1_TILE * n2
    yr = yr.reshape(b, n, c)
    yi = yi.reshape(b, n, c)
    out = pl.pallas_call(
        _fft_b_kernel,
        grid=(b, FFT_N1 // FFT_K1_TILE),
        in_specs=[pl.BlockSpec((1, rows, c), lambda bi, i: (bi, i, 0)),
                  pl.BlockSpec((1, rows, c), lambda bi, i: (bi, i, 0)),
                  pl.BlockSpec(mmat.shape, lambda bi, i: (0, 0)),
                  pl.BlockSpec(bdw.shape, lambda bi, i: (0, 0))],
        out_specs=pl.BlockSpec((1, n2, FFT_K1_TILE, c), lambda bi, i: (bi, 0, i, 0)),
        out_shape=jax.ShapeDtypeStruct((b, n2, FFT_N1, c), F32),
        name="fourier_b",
        compiler_params=_cp(("arbitrary", "arbitrary")),
    )(yr, yi, mmat, bdw)
    return out.reshape(b, n, c)


def _post_kernel(att_ref, four_ref, x_ref, g1_ref, sh2_ref, sc2_ref, gpost_ref, gpre_ref, woa_ref, wof_ref,
                 wrt_ref, br_ref, tri_ref, x1_ref, hx_ref, ridx_ref, gates_ref, cnt_ref, carry_sc):
    first = (pl.program_id(0) == 0) & (pl.program_id(1) == 0)

    @pl.when(first)
    def _():
        carry_sc[...] = jnp.zeros_like(carry_sc)

    mix = _dot(att_ref[0], woa_ref[...]) + _dot(four_ref[0].astype(BF16), wof_ref[...])
    x1 = x_ref[0] + g1_ref[0] * (_rms(mix) * gpost_ref[...])
    x1_ref[0] = x1
    hx = _rms(x1) * gpre_ref[...] * (1.0 + sc2_ref[0]) + sh2_ref[0]
    hx_ref[...] = hx
    lg =_dot_nt(wrt_ref[...], hx.astype(BF16)) + br_ref[...]
    iota_e = lax.broadcasted_iota(I32, lg.shape, 0)
    cur = lg
    tops, idxs, ohs = [], [], []
    for _ in range(TOP_K):
        mx = jnp.max(cur, axis=0, keepdims=True)
        idx = jnp.min(jnp.where(cur == mx, iota_e, N_EXPERTS), axis=0, keepdims=True)
        oh = iota_e == idx
        cur = jnp.where(oh, -jnp.inf, cur)
        tops.append(mx)
        idxs.append(idx)
        ohs.append(oh)
    es = [jnp.exp(t - tops[0]) for t in tops]
    den = es[0] + es[1] + es[2] + es[3]
    gates = [e / den for e in es]
    ohsum = ohs[0].astype(F32) + ohs[1].astype(F32) + ohs[2].astype(F32) + ohs[3].astype(F32)
    base = carry_sc[:, 0:1] + _dot(ohsum.astype(BF16), tri_ref[...])
    ranks = [jnp.sum(jnp.where(oh, base, 0.0), axis=0, keepdims=True).astype(I32) for oh in ohs]
    ridx_ref[0] = _stack_rows(idxs + ranks, 2 * TOP_K)
    gates_ref[...] = _stack_rows(gates, LANES).T
    carry_sc[...] = carry_sc[...] + jnp.sum(ohsum, axis=1, keepdims=True)
    cnt_ref[...] = carry_sc[...]


def _post_attention(att, four, x, m3, gpost, gpre, woa, wof, wrt, br, tri, tm):
    b, n, d = x.shape
    nt = n // tm
    t = b * n
    full = lambda a: pl.BlockSpec(a.shape, lambda bi, i: (0,) * a.ndim)
    mrow = lambda j: pl.BlockSpec((1, 1, d), lambda bi, i: (bi * N_MOD + j, 0, 0))
    return pl.pallas_call(
        _post_kernel,
        grid=(b, nt),
        in_specs=[pl.BlockSpec((1, tm, att.shape[2]), lambda bi, i: (bi, i, 0)),
                  pl.BlockSpec((1, tm, F_WIDTH), lambda bi, i: (bi, i, 0)),
                  pl.BlockSpec((1, tm, d), lambda bi, i: (bi, i, 0)),
                  mrow(2), mrow(3), mrow(4),
                  full(gpost), full(gpre), full(woa), full(wof), full(wrt), full(br), full(tri)],
        out_specs=[pl.BlockSpec((1, tm, d), lambda bi, i: (bi, i, 0)),
                   pl.BlockSpec((tm, d), lambda bi, i: (bi * nt + i, 0)),
                   pl.BlockSpec((1, 2 * TOP_K, tm), lambda bi, i: (bi * nt + i, 0, 0)),
                   pl.BlockSpec((tm, LANES), lambda bi, i: (bi * nt + i, 0)),
                   pl.BlockSpec((N_EXPERTS, LANES), lambda bi, i: (0, 0))],
        out_shape=[jax.ShapeDtypeStruct((b, n, d), F32),
                   jax.ShapeDtypeStruct((t, d), F32),
                   jax.ShapeDtypeStruct((t // tm, 2 * TOP_K, tm), I32),
                   jax.ShapeDtypeStruct((t, LANES), F32),
                   jax.ShapeDtypeStruct((N_EXPERTS, LANES), F32)],
        scratch_shapes=[pltpu.VMEM((N_EXPERTS, LANES), F32)],
        name="post_attention",
        compiler_params=_cp(("arbitrary", "arbitrary")),
    )(att, four, x, m3, m3, m3, gpost, gpre, woa, wof, wrt, br, tri)


def _dispatch_kernel(fill_ref, dest_ref, hx_ref, xs_hbm, zrow, sem, fsem, *, tm, bm):
    i = pl.program_id(0)

    @pl.when(i == 0)
    def _():
        zrow[...] = jnp.zeros_like(zrow)
        nblk = xs_hbm.shape[0] // bm

        def tail(blk, c):
            cp = pltpu.make_async_copy(zrow, xs_hbm.at[pl.ds(blk * bm, bm), :], fsem)
            cp.start()
            cp.wait()
            return c

        lax.fori_loop(fill_ref[2 * N_EXPERTS], nblk, tail, 0)

        def per_expert(e, c):
            s = fill_ref[e]
            n = fill_ref[N_EXPERTS + e]
            head = jnp.minimum(n, (-s) & (SUBLANES - 1))
            body = s + head
            nchunk = (n - head) // SUBLANES

            def row_copy(r):
                return pltpu.make_async_copy(zrow.at[pl.ds(0, 1), :], xs_hbm.at[pl.ds(s + r, 1), :], fsem)

            def chunk_copy(j):
                dst = pl.multiple_of(body + j * SUBLANES, SUBLANES)
                return pltpu.make_async_copy(zrow.at[pl.ds(0, SUBLANES), :], xs_hbm.at[pl.ds(dst, SUBLANES), :], fsem)

            def loop(n_it, copy, action):
                def step(r, cc):
                    action(copy(r))
                    return cc

                lax.fori_loop(0, n_it, step, 0)

            loop(head, row_copy, lambda cp: cp.start())
            loop(nchunk, chunk_copy, lambda cp: cp.start())
            loop(head, row_copy, lambda cp: cp.wait())
            loop(nchunk, chunk_copy, lambda cp: cp.wait())
            return c

        lax.fori_loop(0, N_EXPERTS, per_expert, 0)

    def row_copy(g, u, k):
        d = dest_ref[0, 0, k * tm + g * SUBLANES + u]
        return pltpu.make_async_copy(hx_ref.at[g, pl.ds(u, 1), :], xs_hbm.at[pl.ds(d, 1), :], sem)

    def start(g, c):
        for u in range(SUBLANES):
            for k in range(TOP_K):
                row_copy(g, u, k).start(priority=k % 2)
        return c

    def wait(g, c):
        for u in range(SUBLANES):
            for k in range(TOP_K):
                row_copy(g, u, k).wait()
        return c

    lax.fori_loop(0, tm // SUBLANES, start, 0)
    lax.fori_loop(0, tm // SUBLANES, wait, 0)


def _dispatch(fill, dest_tiles, hx, n_slots, tm, bm):
    t, dm = hx.shape
    kern = functools.partial(_dispatch_kernel, tm=tm, bm=bm)
    return pl.pallas_call(
        kern,
        grid_spec=pltpu.PrefetchScalarGridSpec(
            num_scalar_prefetch=1,
            grid=(t // tm,),
            in_specs=[pl.BlockSpec((1, 1, TOP_K * tm), lambda i, f: (i, 0, 0), memory_space=pltpu.SMEM),
                      pl.BlockSpec((tm // SUBLANES, SUBLANES, dm), lambda i, f: (i, 0, 0))],
            out_specs=pl.BlockSpec(memory_space=pl.ANY),
            scratch_shapes=[pltpu.VMEM((bm, dm), F32),
                            pltpu.SemaphoreType.DMA(()), pltpu.SemaphoreType.DMA(())]),
        out_shape=jax.ShapeDtypeStruct((n_slots, dm), F32),
        name="dispatch",
        compiler_params=_cp(("arbitrary",)),
    )(fill, dest_tiles, hx.reshape(t // SUBLANES, SUBLANES, dm))


def _moe_kernel(be_ref, nv_ref, x_ref, wg_ref, bg_ref, wu_ref, bu_ref, wd_ref, bd_ref, y_ref, wgb, wub, wdb):
    i = pl.program_id(0)

    @pl.when(i < nv_ref[0])
    def _():
        e = be_ref[i]
        prev = be_ref[jnp.maximum(i - 1, 0)]

        @pl.when((i == 0) | (e != prev))
        def _():
            wgb[...] = wg_ref[0].astype(BF16)
            wub[...] = wu_ref[0].astype(BF16)
            wdb[...] = wd_ref[0].astype(BF16)

        xb = x_ref[...].astype(BF16)
        g = jnp.minimum(_dot(xb, wgb[...]) + bg_ref[0], SWIGLU_LIMIT)
        u = jnp.clip(_dot(xb, wub[...]) + bu_ref[0], -SWIGLU_LIMIT, SWIGLU_LIMIT)
        act = g * (1.0 / (1.0 + jnp.exp(-SWIGLU_ALPHA * g))) * (u + 1.0)
        y_ref[...] = _dot(act.astype(BF16), wdb[...]) + bd_ref[0]

    @pl.when(i >= nv_ref[0])
    def _():
        y_ref[...] = jnp.zeros_like(y_ref)


def _experts(block_e, nvalid, xs, wg, bg, wu, bu, wd, bd, bm):
    n_slots = xs.shape[0]
    e, d, f = wg.shape
    nb = n_slots // bm
    xmap = lambda i, be, nv: (jnp.minimum(i, nv[0] - 1), 0)
    wmap = lambda i, be, nv: (be[i], 0, 0)
    return pl.pallas_call(
        _moe_kernel,
        grid_spec=pltpu.PrefetchScalarGridSpec(
            num_scalar_prefetch=2,
            grid=(nb,),
            in_specs=[pl.BlockSpec((bm, d), xmap),
                      pl.BlockSpec((1, d, f), wmap), pl.BlockSpec((1, 1, f), wmap),
                      pl.BlockSpec((1, d, f), wmap), pl.BlockSpec((1, 1, f), wmap),
                      pl.BlockSpec((1, f, d), wmap), pl.BlockSpec((1, 1, d), wmap)],
            out_specs=pl.BlockSpec((bm, d), lambda i, be, nv: (i, 0)),
            scratch_shapes=[pltpu.VMEM((d, f), BF16), pltpu.VMEM((d, f), BF16), pltpu.VMEM((f, d), BF16)]),
        out_shape=jax.ShapeDtypeStruct((n_slots, d), F32),
        name="experts",
        compiler_params=_cp(("arbitrary",)),
    )(block_e, nvalid, xs, wg, bg.reshape(e, 1, f), wu, bu.reshape(e, 1, f), wd, bd.reshape(e, 1, d))


def _combine_kernel(dcur_ref, dnxt_ref, ys_hbm, gates_ref, x1_ref, g2_ref, gpost_ref, o_ref, buf, sem,
                    *, tm, nsteps):
    i = pl.program_id(0)
    slot = i % 2

    def row_copy(dref, sl, g, u, k):
        d = dref[0, 0, k * tm + g * SUBLANES + u]
        return pltpu.make_async_copy(ys_hbm.at[pl.ds(d, 1), :], buf.at[sl, k, g, pl.ds(u, 1), :], sem.at[sl])

    def issue(dref, sl):
        def start(g, c):
            for u in range(SUBLANES):
                for k in range(TOP_K):
                    row_copy(dref, sl, g, u, k).start(priority=k % 2)
            return c

        lax.fori_loop(0, tm // SUBLANES, start, 0)

    @pl.when(i == 0)
    def _():
        issue(dcur_ref, 0)

    @pl.when(i + 1 < nsteps)
    def _():
        issue(dnxt_ref, 1 - slot)

    def wait(g, c):
        for u in range(SUBLANES):
            for k in range(TOP_K):
                row_copy(dcur_ref, slot, g, u, k).wait()
        return c

    lax.fori_loop(0, tm // SUBLANES, wait, 0)
    gt = gates_ref[...]
    rows = lambda k: buf[slot, k].reshape(tm, buf.shape[-1])
    y = gt[:, 0:1] * rows(0)
    for k in range(1, TOP_K):
        y = y + gt[:, k:k + 1] * rows(k)
    o_ref[...] = x1_ref[...] + g2_ref[0] * (_rms(y) * gpost_ref[...])


def _combine(dest_tiles, ys, gates_t, x1, m3, gpost, tm):
    b, n, d = x1.shape
    nt = n // tm
    nsteps = b * nt
    kern = functools.partial(_combine_kernel, tm=tm, nsteps=nsteps)
    dspec = lambda f: pl.BlockSpec((1, 1, TOP_K * tm), f, memory_space=pltpu.SMEM)
    out = pl.pallas_call(
        kern,
        grid=(nsteps,),
        in_specs=[dspec(lambda i: (i, 0, 0)),
                  dspec(lambda i: (jnp.minimum(i + 1, nsteps - 1), 0, 0)),
                  pl.BlockSpec(memory_space=pl.ANY),
                  pl.BlockSpec((tm, LANES), lambda i: (i, 0)),
                  pl.BlockSpec((tm, d), lambda i: (i, 0)),
                  pl.BlockSpec((1, 1, d), lambda i: (i // nt * N_MOD + 5, 0, 0)),
                  pl.BlockSpec(gpost.shape, lambda i: (0, 0))],
        out_specs=pl.BlockSpec((tm, d), lambda i: (i, 0)),
        out_shape=jax.ShapeDtypeStruct((b * n, d), F32),
        scratch_shapes=[pltpu.VMEM((2, TOP_K, tm // SUBLANES, SUBLANES, d), F32),
                        pltpu.SemaphoreType.DMA((2,))],
        name="combine",
        compiler_params=_cp(("arbitrary",)),
    )(dest_tiles, dest_tiles, ys, gates_t, x1.reshape(b * n, d), m3, gpost)
    return out.reshape(b, n, d)


def _rope_tables(n):
    rows = n // GRID_W
    row = jnp.repeat(jnp.arange(rows, dtype=F32), GRID_W)
    col = jnp.tile(jnp.arange(GRID_W, dtype=F32), rows)
    half = QK_ROPE // 2
    inv = 1.0 / (ROPE_BASE ** (jnp.arange(0, half, 2, dtype=F32) / half))
    ang = jnp.concatenate([row[:, None] * inv, col[:, None] * inv], axis=-1)
    cos, sin = jnp.cos(ang), jnp.sin(ang)
    zero = jnp.zeros((n, ROPE_W - QK_ROPE), F32)
    return jnp.concatenate([cos, cos, zero], axis=1), jnp.concatenate([sin, sin, zero], axis=1)


def _ctx_tables(n):
    half = QK_ROPE // 2
    t1 = np.zeros((n, ROPE_W), np.float32)
    t1[:, :2 * half] = 1.0
    return jnp.asarray(t1), jnp.zeros((n, ROPE_W), F32)


def _prep_weights(w_in, norm_q, norm_kv, w_q_up, w_kv_up):
    d = w_in.shape[0]
    half = QK_ROPE // 2
    zr = lambda r, c: jnp.zeros((r, c), F32)
    o_rope = F_WIDTH + Q_LORA + KV_LORA
    kr = w_in[:, o_rope:o_rope + QK_ROPE]
    kre, kro = kr[:, 0::2], kr[:, 1::2]
    win = jnp.concatenate([w_in[:, :o_rope],
                           kre, kro, zr(d, ROPE_W - QK_ROPE),
                           -kro, kre, zr(d, ROPE_W - QK_ROPE)], axis=1).astype(BF16)
    hd = QK_NOPE + QK_ROPE
    qa, qb = [], []
    for h in range(N_HEADS):
        wn = w_q_up[:, h * hd:h * hd + QK_NOPE]
        wr = w_q_up[:, h * hd + QK_NOPE:(h + 1) * hd]
        we, wo = wr[:, 0::2], wr[:, 1::2]
        qa += [wn, we, wo, zr(Q_LORA, ROPE_W - QK_ROPE)]
        qb += [-wo, we, zr(Q_LORA, ROPE_W - QK_ROPE)]
    wqa = jnp.concatenate(qa, axis=1).astype(BF16)
    wqb = jnp.concatenate(qb, axis=1).astype(BF16)
    kvd = QK_NOPE + V_DIM
    wk = jnp.concatenate([w_kv_up[:, h * kvd:h * kvd + QK_NOPE] for h in range(N_HEADS)], axis=1).astype(BF16)
    wv = jnp.concatenate([w_kv_up[:, h * kvd + QK_NOPE:(h + 1) * kvd] for h in range(N_HEADS)], axis=1)
    wvt = wv.T.astype(BF16)
    return win, norm_q.reshape(1, -1), norm_kv.reshape(1, -1), wqa, wqb, wk, wvt


def _fourier_consts(n, w_fourier):
    c = F_GDIM
    j = np.arange(c)
    ang = 2.0 * np.pi * np.outer(j, j) / c
    eye = np.eye(F_GROUPS)
    bdc = np.kron(eye, np.cos(ang))
    bds = np.kron(eye, np.sin(ang))
    bdcs = jnp.asarray(np.concatenate([bdc, -bds], axis=1), dtype=BF16)
    n1 = FFT_N1
    n2 = n // n1
    k1 = np.arange(n1)
    a1 = 2.0 * np.pi * np.outer(k1, k1) / n1
    wc, ws = np.cos(a1), np.sin(a1)
    w1 = jnp.asarray(np.block([[wc, ws], [-ws, wc]]), dtype=BF16)
    tw = 2.0 * np.pi * np.outer(k1, np.arange(n2)) / n
    tc = jnp.repeat(jnp.asarray(np.cos(tw), dtype=F32), F_WIDTH, axis=1)
    ts = jnp.repeat(jnp.asarray(np.sin(tw), dtype=F32), F_WIDTH, axis=1)
    k2 = np.arange(n2)
    a2 = 2.0 * np.pi * np.outer(k2, k2) / n2
    norm = 1.0 / math.sqrt(n * c)
    kt = FFT_K1_TILE
    mc = np.zeros((n2, kt, kt, n2))
    ms = np.zeros((n2, kt, kt, n2))
    for r in range(kt):
        mc[:, r, r, :] = np.cos(a2) * norm
        ms[:, r, r, :] = np.sin(a2) * norm
    mmat = np.concatenate([mc.reshape(n2 * kt, kt * n2), ms.reshape(n2 * kt, kt * n2)], axis=1)
    mmat = jnp.asarray(mmat, dtype=BF16)
    zblk = jnp.zeros((c, c), F32)
    bdw = jnp.concatenate(
        [jnp.concatenate([w_fourier[g] if j == g else zblk for j in range(F_GROUPS)], axis=1)
         for g in range(F_GROUPS)], axis=0)
    return bdcs, w1, tc, ts, mmat, bdw.astype(BF16)


def kernel(x, c, ctx, c_ctx, w_mod, b_mod, norm_attn_pre, norm_attn_post, norm_ffn_pre, norm_ffn_post, w_in, norm_q_lat, norm_kv_lat, w_q_up, w_kv_up, w_fourier, w_out, w_router, b_router, w_gate, b_gate, w_up, b_up, w_down, b_down):
    b, n, d = x.shape
    nctx = ctx.shape[1]
    t = b * n
    assert w_mod.shape[0] == 1 and b + 1 <= SUBLANES and n % (FFT_N1 * SUBLANES) == 0
    row2 = lambda a: a[0].reshape(1, -1)

    c8 = jnp.concatenate([c, c_ctx[None, :], jnp.zeros((SUBLANES - b - 1, d), F32)], axis=0)
    m3 = _modulation(c8, w_mod[0], b_mod[0]).reshape(SUBLANES * N_MOD, 1, d)

    wts = _prep_weights(w_in[0], norm_q_lat[0], norm_kv_lat[0], w_q_up[0], w_kv_up[0])
    bdcs, w1, tc, ts, mmat, bdw = _fourier_consts(n, w_fourier[0])
    pro_w = (wts[0], bdcs) + wts[1:]
    scale = (QK_NOPE + QK_ROPE) ** -0.5 * LOG2E
    g_pre = row2(norm_attn_pre)

    tm = min(512, n)
    t1x, t2x = _rope_tables(n)
    q, k, vt, za, zb = _prologue(x, m3, lambda bi: bi, g_pre, pro_w, t1x, t2x, tm, scale)
    t1c, t2c = _ctx_tables(nctx)
    _, kc, vtc, _, _ = _prologue(ctx, m3, lambda bi: b, g_pre, pro_w, t1c, t2c, nctx, scale)

    att = _attention(q, k, vt, kc, vtc, min(1024, n))
    four = _fourier(za, zb, w1, tc, ts, mmat, bdw)

    tmp = min(512, n)
    tri = jnp.asarray(np.triu(np.ones((tmp, tmp), np.float32), 1), dtype=BF16)
    width = N_HEADS * V_DIM
    woa = w_out[0, :width].astype(BF16)
    wof = w_out[0, width:].astype(BF16)
    x1, hx2, ridx, gates_t, cnt = _post_attention(
        att, four, x, m3, row2(norm_attn_post), row2(norm_ffn_pre), woa, wof,
        w_router[0].T.astype(BF16), b_router[0].reshape(-1, 1), tri, tmp)

    bm = 512
    counts = cnt[:, 0].astype(I32)
    padded = (counts + bm - 1) // bm * bm
    pad_end = jnp.cumsum(padded)
    pad_start = pad_end - padded
    nb = t * TOP_K // bm + N_EXPERTS
    blk_start = jnp.arange(nb, dtype=I32)[:, None] * bm
    block_e = jnp.minimum(jnp.sum((pad_end[None, :] <= blk_start).astype(I32), axis=1), N_EXPERTS - 1)
    nvalid = (pad_end[-1:] // bm).astype(I32)
    eids = jnp.arange(N_EXPERTS, dtype=I32)[:, None, None, None]
    eidx, rank = ridx[:, :TOP_K], ridx[:, TOP_K:]
    dest = jnp.sum(jnp.where(eidx[None] == eids, pad_start[:, None, None, None], 0), axis=0) + rank
    dest_tiles = dest.reshape(t // tmp, 1, TOP_K * tmp)
    fill = jnp.concatenate([pad_start + counts, padded - counts, nvalid]).astype(I32)

    xs = _dispatch(fill, dest_tiles, hx2, nb * bm, tmp, bm)
    ys = _experts(block_e, nvalid, xs, w_gate[0], b_gate[0], w_up[0], b_up[0], w_down[0], b_down[0], bm)
    return _combine(dest_tiles, ys, gates_t, x1, m3, row2(norm_ffn_post), tmp)
```

```python
import functools
import math

import numpy as np
import jax
import jax.numpy as jnp
from jax import lax
from jax.experimental import pallas as pl
from jax.experimental.pallas import tpu as pltpu

F32 = jnp.float32
BF16 = jnp.bfloat16
I32 = jnp.int32

N_HEADS = 6
QK_NOPE = 128
QK_ROPE = 64
V_DIM = 128
Q_LORA = 384
KV_LORA = 256
F_GROUPS = 4
F_GDIM = 64
F_WIDTH = F_GROUPS * F_GDIM
GRID_W = 64
ROPE_BASE = 10000.0
N_EXPERTS = 32
TOP_K = 4
SWIGLU_LIMIT = 7.0
SWIGLU_ALPHA = 1.702
RMS_EPS = 1e-6
N_MOD = 6

LANES = 128
SUBLANES = 8
HEAD_W = 256
ROPE_W = 128
FFT_N1 = 128
FFT_K1_TILE = 8
VMEM_LIMIT = 56 * 1024 * 1024

LOG2E = 1.4426950408889634
NORM_SLACK = 1.0 + 2.0 ** -6
MIN_ROW_SUM = 2.0 ** -60


def _cp(sem, vmem=None):
    return pltpu.CompilerParams(dimension_semantics=sem, vmem_limit_bytes=vmem or VMEM_LIMIT)


def _rms(x):
    return x * lax.rsqrt(jnp.mean(x * x, axis=-1, keepdims=True) + RMS_EPS)


def _dot(a, b):
    return jnp.dot(a, b, preferred_element_type=F32)


def _stack_rows(rows, height):
    w = rows[0].shape[1]
    sub = lax.broadcasted_iota(I32, (height, w), 0)
    out = jnp.zeros((height, w), rows[0].dtype)
    for r, v in enumerate(rows):
        out = jnp.where(sub == r, v, out)
    return out


def _dot_nt(a, b):
    return lax.dot_general(a, b, (((1,), (1,)), ((), ())), preferred_element_type=F32)


def _mod_kernel(c_ref, w_ref, b_ref, o_ref):
    c = c_ref[...]
    a = c / (1.0 + jnp.exp(-c))
    a_hi = a.astype(BF16)
    a_lo = (a - a_hi.astype(F32)).astype(BF16)
    w = w_ref[...]
    w_hi = w.astype(BF16)
    w_lo = (w - w_hi.astype(F32)).astype(BF16)
    acc = _dot(a_hi, w_hi) + _dot(a_lo, w_hi) + _dot(a_hi, w_lo)
    o_ref[...] = acc + b_ref[...]


def _modulation(c8, w_mod, b_mod):
    d, n = w_mod.shape
    tn = 1024
    return pl.pallas_call(
        _mod_kernel,
        grid=(n // tn,),
        in_specs=[pl.BlockSpec((SUBLANES, d), lambda j: (0, 0)),
                  pl.BlockSpec((d, tn), lambda j: (0, j)),
                  pl.BlockSpec((1, tn), lambda j: (0, j))],
        out_specs=pl.BlockSpec((SUBLANES, tn), lambda j: (0, j)),
        out_shape=jax.ShapeDtypeStruct((SUBLANES, n), F32),
        name="modulation",
        compiler_params=_cp(("arbitrary",)),
    )(c8, w_mod, b_mod.reshape(1, n))


def _pro_kernel(x_ref, sh_ref, sc_ref, g_ref, win_ref, bdcs_ref, nq_ref, nkv_ref, wqa_ref, wqb_ref,
                wk_ref, wvt_ref, t1_ref, t2_ref, q_ref, k_ref, vt_ref, za_ref, zb_ref, *, scale):
    x = x_ref[0]
    hx = _rms(x) * g_ref[...] * (1.0 + sc_ref[0]) + sh_ref[0]
    px = _dot(hx.astype(BF16), win_ref[...])
    u = px[:, 0:F_WIDTH].astype(BF16)
    z = _dot(u, bdcs_ref[...])
    za_ref[0] = z[:, :F_WIDTH].astype(BF16)
    zb_ref[0] = z[:, F_WIDTH:].astype(BF16)
    o_q = F_WIDTH
    o_kv = o_q + Q_LORA
    o_ra = o_kv + KV_LORA
    o_rb = o_ra + ROPE_W
    t1 = t1_ref[...]
    t2 = t2_ref[...]
    qn = (_rms(px[:, o_q:o_kv]) * nq_ref[...]).astype(BF16)
    qa = _dot(qn, wqa_ref[...])
    qb = _dot(qn, wqb_ref[...])
    for h in range(N_HEADS):
        nope = qa[:, h * HEAD_W:h * HEAD_W + QK_NOPE] * scale
        rope = (qa[:, h * HEAD_W + QK_NOPE:(h + 1) * HEAD_W] * t1
                + qb[:, h * ROPE_W:(h + 1) * ROPE_W] * t2) * scale
        q_ref[0, h, :, 0:QK_NOPE] = nope.astype(BF16)
        q_ref[0, h, :, QK_NOPE:HEAD_W] = rope.astype(BF16)
    kvn = (_rms(px[:, o_kv:o_ra]) * nkv_ref[...]).astype(BF16)
    kn = _dot(kvn, wk_ref[...])
    kr = (px[:, o_ra:o_rb] * t1 + px[:, o_rb:o_rb + ROPE_W] * t2).astype(BF16)
    for h in range(N_HEADS):
        k_ref[0, h, :, 0:QK_NOPE] = kn[:, h * QK_NOPE:(h + 1) * QK_NOPE].astype(BF16)
        k_ref[0, h, :, QK_NOPE:HEAD_W] = kr
    vt = _dot_nt(wvt_ref[...], kvn)
    for h in range(N_HEADS):
        vt_ref[0, h, 0] = vt[h * V_DIM:(h + 1) * V_DIM, :].astype(BF16)


def _prologue(x, m3, mod_row, g_pre, wts, t1, t2, tm, scale):
    b, n, d = x.shape
    nt = n // tm
    win, bdcs, nq, nkv, wqa, wqb, wk, wvt = wts
    full = lambda a: pl.BlockSpec(a.shape, lambda bi, i: (0,) * a.ndim)
    kern = functools.partial(_pro_kernel, scale=scale)
    return pl.pallas_call(
        kern,
        grid=(b, nt),
        in_specs=[pl.BlockSpec((1, tm, d), lambda bi, i: (bi, i, 0)),
                  pl.BlockSpec((1, 1, d), lambda bi, i: (mod_row(bi) * N_MOD + 0, 0, 0)),
                  pl.BlockSpec((1, 1, d), lambda bi, i: (mod_row(bi) * N_MOD + 1, 0, 0)),
                  full(g_pre), full(win), full(bdcs), full(nq), full(nkv), full(wqa), full(wqb),
                  full(wk), full(wvt),
                  pl.BlockSpec((tm, ROPE_W), lambda bi, i: (i, 0)),
                  pl.BlockSpec((tm, ROPE_W), lambda bi, i: (i, 0))],
        out_specs=[pl.BlockSpec((1, N_HEADS, tm, HEAD_W), lambda bi, i: (bi, 0, i, 0)),
                   pl.BlockSpec((1, N_HEADS, tm, HEAD_W), lambda bi, i: (bi, 0, i, 0)),
                   pl.BlockSpec((1, N_HEADS, 1, V_DIM, tm), lambda bi, i: (bi, 0, i, 0, 0)),
                   pl.BlockSpec((1, tm, F_WIDTH), lambda bi, i: (bi, i, 0)),
                   pl.BlockSpec((1, tm, F_WIDTH), lambda bi, i: (bi, i, 0))],
        out_shape=[jax.ShapeDtypeStruct((b, N_HEADS, n, HEAD_W), BF16),
                   jax.ShapeDtypeStruct((b, N_HEADS, n, HEAD_W), BF16),
                   jax.ShapeDtypeStruct((b, N_HEADS, nt, V_DIM, tm), BF16),
                   jax.ShapeDtypeStruct((b, n, F_WIDTH), BF16),
                   jax.ShapeDtypeStruct((b, n, F_WIDTH), BF16)],
        name="prologue",
        compiler_params=_cp(("arbitrary", "arbitrary")),
    )(x, m3, m3, g_pre, win, bdcs, nq, nkv, wqa, wqb, wk, wvt, t1, t2)


def _attn_kernel(q_ref, k_ref, vt_ref, kc_ref, vtc_ref, o_ref, kmax_sc, m_sc, l_sc, acc_sc,
                 *, nkv, tk, group_size):
    q = q_ref[0, 0]
    ones_kw = jnp.ones((HEAD_W, LANES), BF16)

    def sq_norms(kb):
        kf = kb.astype(F32)
        return jnp.max(_dot((kf * kf).astype(BF16), ones_kw), axis=0, keepdims=True)

    @pl.when(pl.program_id(2) == 0)
    def _():
        def body(j, best):
            start = pl.multiple_of(j * tk, tk)
            return jnp.maximum(best, sq_norms(k_ref[0, 0, pl.ds(start, tk), :]))

        kmax_sc[...] = lax.fori_loop(0, nkv, body, sq_norms(kc_ref[0, 0]))

    qf = q.astype(F32)
    qn2 = _dot_nt(jnp.ones((SUBLANES, HEAD_W), BF16), (qf * qf).astype(BF16))[0:1]
    ref = jnp.sqrt(qn2 * kmax_sc[:, 0:1]) * NORM_SLACK

    def fast_chunk(kb, vtb):
        p = jnp.exp2(_dot_nt(kb, q) - ref)
        l_sc[...] += jnp.sum(p, axis=0, keepdims=True)
        acc_sc[...] += _dot(vtb, p.astype(BF16))

    l_sc[...] = jnp.zeros_like(l_sc)
    acc_sc[...] = jnp.zeros_like(acc_sc)
    fast_chunk(kc_ref[0, 0], vtc_ref[0, 0, 0])

    def fast_body(i, carry):
        for u in range(0, group_size, 2):
            j = group_size * i + u
            start = pl.multiple_of(j * tk, tk)
            vtb = jnp.concatenate([vt_ref[0, 0, j], vt_ref[0, 0, j + 1]], axis=1)
            fast_chunk(k_ref[0, 0, pl.ds(start, 2 * tk), :], vtb)
        return carry

    lax.fori_loop(0, nkv // group_size, fast_body, 0)

    @pl.when(jnp.min(l_sc[...]) < MIN_ROW_SUM)
    def _():
        s = _dot_nt(kc_ref[0, 0], q)
        m0 = jnp.max(s, axis=0, keepdims=True)
        p = jnp.exp2(s - m0)
        m_sc[...] = m0
        l_sc[...] = jnp.sum(p, axis=0, keepdims=True)
        acc_sc[...] = _dot(vtc_ref[0, 0, 0], p.astype(BF16))

        def body(j, carry):
            start = pl.multiple_of(j * tk, tk)
            s = _dot_nt(k_ref[0, 0, pl.ds(start, tk), :], q)
            m_prev = m_sc[...]
            m_new = jnp.maximum(m_prev, jnp.max(s, axis=0, keepdims=True))
            alpha = jnp.exp2(m_prev - m_new)
            p = jnp.exp2(s - m_new)
            l_sc[...] = alpha * l_sc[...] + jnp.sum(p, axis=0, keepdims=True)
            acc_sc[...] = alpha * acc_sc[...] + _dot(vt_ref[0, 0, j], p.astype(BF16))
            m_sc[...] = m_new
            return carry

        lax.fori_loop(0, nkv, body, 0)

    o = acc_sc[...] / l_sc[...]
    o_ref[0] = o.T.astype(BF16)


def _attention(q, k, vt, kc, vtc, tq):
    b, h, n, _ = q.shape
    nkv, tk = vt.shape[2], vt.shape[4]
    nctx = kc.shape[2]
    group_size = next(g for g in (16, 8, 4, 2, 1) if nkv % g == 0)
    kern = functools.partial(_attn_kernel, nkv=nkv, tk=tk, group_size=group_size)
    return pl.pallas_call(
        kern,
        grid=(b, h, n // tq),
        in_specs=[pl.BlockSpec((1, 1, tq, HEAD_W), lambda bi, hi, i: (bi, hi, i, 0)),
                  pl.BlockSpec((1, 1, n, HEAD_W), lambda bi, hi, i: (bi, hi, 0, 0)),
                  pl.BlockSpec((1, 1, nkv, V_DIM, tk), lambda bi, hi, i: (bi, hi, 0, 0, 0)),
                  pl.BlockSpec((1, 1, nctx, HEAD_W), lambda bi, hi, i: (bi, hi, 0, 0)),
                  pl.BlockSpec((1, 1, 1, V_DIM, nctx), lambda bi, hi, i: (bi, hi, 0, 0, 0))],
        out_specs=pl.BlockSpec((1, tq, V_DIM), lambda bi, hi, i: (bi, i, hi)),
        out_shape=jax.ShapeDtypeStruct((b, n, h * V_DIM), BF16),
        scratch_shapes=[pltpu.VMEM((1, LANES), F32),
                        pltpu.VMEM((1, tq), F32), pltpu.VMEM((1, tq), F32),
                        pltpu.VMEM((V_DIM, tq), F32)],
        name="attention",
        compiler_params=_cp(("arbitrary", "arbitrary", "arbitrary")),
    )(q, k, vt, kc, vtc)


def _fft_a_kernel(za_ref, zb_ref, w1_ref, tc_ref, ts_ref, yr_ref, yi_ref):
    ab = jnp.concatenate([za_ref[0], zb_ref[0]], axis=0)
    y = _dot(w1_ref[...], ab)
    re = y[:FFT_N1]
    im = y[FFT_N1:]
    tc = tc_ref[...]
    ts = ts_ref[...]
    yr_ref[0] = (re * tc + im * ts).astype(BF16)
    yi_ref[0] = (im * tc - re * ts).astype(BF16)


def _fft_b_kernel(yr_ref, yi_ref, m_ref, bdw_ref, o_ref):
    ri = jnp.concatenate([yr_ref[0], yi_ref[0]], axis=0)
    f = _dot(m_ref[...], ri)
    four = _dot(f.astype(BF16), bdw_ref[...])
    o_ref[0] = four.reshape(o_ref.shape[1:])


def _fourier(za, zb, w1, tc, ts, mmat, bdw):
    b, n, c = za.shape
    n2 = n // FFT_N1
    wide = n2 * c
    tn = min(2048, wide)
    za2 = za.reshape(b, FFT_N1, wide)
    zb2 = zb.reshape(b, FFT_N1, wide)
    yr, yi = pl.pallas_call(
        _fft_a_kernel,
        grid=(wide // tn, b),
        in_specs=[pl.BlockSpec((1, FFT_N1, tn), lambda i, bi: (bi, 0, i)),
                  pl.BlockSpec((1, FFT_N1, tn), lambda i, bi: (bi, 0, i)),
                  pl.BlockSpec(w1.shape, lambda i, bi: (0, 0)),
                  pl.BlockSpec((FFT_N1, tn), lambda i, bi: (0, i)),
                  pl.BlockSpec((FFT_N1, tn), lambda i, bi: (0, i))],
        out_specs=[pl.BlockSpec((1, FFT_N1, tn), lambda i, bi: (bi, 0, i)),
                   pl.BlockSpec((1, FFT_N1, tn), lambda i, bi: (bi, 0, i))],
        out_shape=[jax.ShapeDtypeStruct((b, FFT_N1, wide), BF16)] * 2,
        name="fourier_a",
        compiler_params=_cp(("arbitrary", "arbitrary")),
    )(za2, zb2, w1, tc, ts)
    rows = FFT_K1_TILE * n2
    yr = yr.reshape(b, n, c)
    yi = yi.reshape(b, n, c)
    out = pl.pallas_call(
        _fft_b_kernel,
        grid=(b, FFT_N1 // FFT_K1_TILE),
        in_specs=[pl.BlockSpec((1, rows, c), lambda bi, i: (bi, i, 0)),
                  pl.BlockSpec((1, rows, c), lambda bi, i: (bi, i, 0)),
                  pl.BlockSpec(mmat.shape, lambda bi, i: (0, 0)),
                  pl.BlockSpec(bdw.shape, lambda bi, i: (0, 0))],
        out_specs=pl.BlockSpec((1, n2, FFT_K1_TILE, c), lambda bi, i: (bi, 0, i, 0)),
        out_shape=jax.ShapeDtypeStruct((b, n2, FFT_N1, c), F32),
        name="fourier_b",
        compiler_params=_cp(("arbitrary", "arbitrary")),
    )(yr, yi, mmat, bdw)
    return out.reshape(b, n, c)


def _post_kernel(att_ref, four_ref, x_ref, g1_ref, sh2_ref, sc2_ref, gpost_ref, gpre_ref, woa_ref, wof_ref,
                 wrt_ref, br_ref, tri_ref, x1_ref, hx_ref, ridx_ref, gates_ref, cnt_ref, carry_sc):
    first = (pl.program_id(0) == 0) & (pl.program_id(1) == 0)

    @pl.when(first)
    def _():
        carry_sc[...] = jnp.zeros_like(carry_sc)

    mix = _dot(att_ref[0], woa_ref[...]) + _dot(four_ref[0].astype(BF16), wof_ref[...])
    x1 = x_ref[0] + g1_ref[0] * (_rms(mix) * gpost_ref[...])
    x1_ref[0] = x1
    hx = _rms(x1) * gpre_ref[...] * (1.0 + sc2_ref[0]) + sh2_ref[0]
    hx_ref[...] = hx
    lg =_dot_nt(wrt_ref[...], hx.astype(BF16)) + br_ref[...]
    iota_e = lax.broadcasted_iota(I32, lg.shape, 0)
    cur = lg
    tops, idxs, ohs = [], [], []
    for _ in range(TOP_K):
        mx = jnp.max(cur, axis=0, keepdims=True)
        idx = jnp.min(jnp.where(cur == mx, iota_e, N_EXPERTS), axis=0, keepdims=True)
        oh = iota_e == idx
        cur = jnp.where(oh, -jnp.inf, cur)
        tops.append(mx)
        idxs.append(idx)
        ohs.append(oh)
    es = [jnp.exp(t - tops[0]) for t in tops]
    den = es[0] + es[1] + es[2] + es[3]
    gates = [e / den for e in es]
    ohsum = ohs[0].astype(F32) + ohs[1].astype(F32) + ohs[2].astype(F32) + ohs[3].astype(F32)
    base = carry_sc[:, 0:1] + _dot(ohsum.astype(BF16), tri_ref[...])
    ranks = [jnp.sum(jnp.where(oh, base, 0.0), axis=0, keepdims=True).astype(I32) for oh in ohs]
    ridx_ref[0] = _stack_rows(idxs + ranks, 2 * TOP_K)
    gates_ref[...] = _stack_rows(gates, LANES).T
    carry_sc[...] = carry_sc[...] + jnp.sum(ohsum, axis=1, keepdims=True)
    cnt_ref[...] = carry_sc[...]


def _post_attention(att, four, x, m3, gpost, gpre, woa, wof, wrt, br, tri, tm):
    b, n, d = x.shape
    nt = n // tm
    t = b * n
    full = lambda a: pl.BlockSpec(a.shape, lambda bi, i: (0,) * a.ndim)
    mrow = lambda j: pl.BlockSpec((1, 1, d), lambda bi, i: (bi * N_MOD + j, 0, 0))
    return pl.pallas_call(
        _post_kernel,
        grid=(b, nt),
        in_specs=[pl.BlockSpec((1, tm, att.shape[2]), lambda bi, i: (bi, i, 0)),
                  pl.BlockSpec((1, tm, F_WIDTH), lambda bi, i: (bi, i, 0)),
                  pl.BlockSpec((1, tm, d), lambda bi, i: (bi, i, 0)),
                  mrow(2), mrow(3), mrow(4),
                  full(gpost), full(gpre), full(woa), full(wof), full(wrt), full(br), full(tri)],
        out_specs=[pl.BlockSpec((1, tm, d), lambda bi, i: (bi, i, 0)),
                   pl.BlockSpec((tm, d), lambda bi, i: (bi * nt + i, 0)),
                   pl.BlockSpec((1, 2 * TOP_K, tm), lambda bi, i: (bi * nt + i, 0, 0)),
                   pl.BlockSpec((tm, LANES), lambda bi, i: (bi * nt + i, 0)),
                   pl.BlockSpec((N_EXPERTS, LANES), lambda bi, i: (0, 0))],
        out_shape=[jax.ShapeDtypeStruct((b, n, d), F32),
                   jax.ShapeDtypeStruct((t, d), F32),
                   jax.ShapeDtypeStruct((t // tm, 2 * TOP_K, tm), I32),
                   jax.ShapeDtypeStruct((t, LANES), F32),
                   jax.ShapeDtypeStruct((N_EXPERTS, LANES), F32)],
        scratch_shapes=[pltpu.VMEM((N_EXPERTS, LANES), F32)],
        name="post_attention",
        compiler_params=_cp(("arbitrary", "arbitrary")),
    )(att, four, x, m3, m3, m3, gpost, gpre, woa, wof, wrt, br, tri)


def _dispatch_kernel(fill_ref, dest_ref, hx_ref, xs_hbm, zrow, sem, fsem, *, tm, bm):
    i = pl.program_id(0)

    @pl.when(i == 0)
    def _():
        zrow[...] = jnp.zeros_like(zrow)
        nblk = xs_hbm.shape[0] // bm

        def tail(blk, c):
            cp = pltpu.make_async_copy(zrow, xs_hbm.at[pl.ds(blk * bm, bm), :], fsem)
            cp.start()
            cp.wait()
            return c

        lax.fori_loop(fill_ref[2 * N_EXPERTS], nblk, tail, 0)

        def per_expert(e, c):
            s = fill_ref[e]
            n = fill_ref[N_EXPERTS + e]
            head = jnp.minimum(n, (-s) & (SUBLANES - 1))
            body = s + head
            nchunk = (n - head) // SUBLANES

            def row_copy(r):
                return pltpu.make_async_copy(zrow.at[pl.ds(0, 1), :], xs_hbm.at[pl.ds(s + r, 1), :], fsem)

            def chunk_copy(j):
                dst = pl.multiple_of(body + j * SUBLANES, SUBLANES)
                return pltpu.make_async_copy(zrow.at[pl.ds(0, SUBLANES), :], xs_hbm.at[pl.ds(dst, SUBLANES), :], fsem)

            def loop(n_it, copy, action):
                def step(r, cc):
                    action(copy(r))
                    return cc

                lax.fori_loop(0, n_it, step, 0)

            loop(head, row_copy, lambda cp: cp.start())
            loop(nchunk, chunk_copy, lambda cp: cp.start())
            loop(head, row_copy, lambda cp: cp.wait())
            loop(nchunk, chunk_copy, lambda cp: cp.wait())
            return c

        lax.fori_loop(0, N_EXPERTS, per_expert, 0)

    def row_copy(g, u, k):
        d = dest_ref[0, 0, k * tm + g * SUBLANES + u]
        return pltpu.make_async_copy(hx_ref.at[g, pl.ds(u, 1), :], xs_hbm.at[pl.ds(d, 1), :], sem)

    def start(g, c):
        for u in range(SUBLANES):
            for k in range(TOP_K):
                row_copy(g, u, k).start(priority=k % 2)
        return c

    def wait(g, c):
        for u in range(SUBLANES):
            for k in range(TOP_K):
                row_copy(g, u, k).wait()
        return c

    lax.fori_loop(0, tm // SUBLANES, start, 0)
    lax.fori_loop(0, tm // SUBLANES, wait, 0)


def _dispatch(fill, dest_tiles, hx, n_slots, tm, bm):
    t, dm = hx.shape
    kern = functools.partial(_dispatch_kernel, tm=tm, bm=bm)
    return pl.pallas_call(
        kern,
        grid_spec=pltpu.PrefetchScalarGridSpec(
            num_scalar_prefetch=1,
            grid=(t // tm,),
            in_specs=[pl.BlockSpec((1, 1, TOP_K * tm), lambda i, f: (i, 0, 0), memory_space=pltpu.SMEM),
                      pl.BlockSpec((tm // SUBLANES, SUBLANES, dm), lambda i, f: (i, 0, 0))],
            out_specs=pl.BlockSpec(memory_space=pl.ANY),
            scratch_shapes=[pltpu.VMEM((bm, dm), F32),
                            pltpu.SemaphoreType.DMA(()), pltpu.SemaphoreType.DMA(())]),
        out_shape=jax.ShapeDtypeStruct((n_slots, dm), F32),
        name="dispatch",
        compiler_params=_cp(("arbitrary",)),
    )(fill, dest_tiles, hx.reshape(t // SUBLANES, SUBLANES, dm))


def _moe_kernel(be_ref, nv_ref, x_ref, wg_ref, bg_ref, wu_ref, bu_ref, wd_ref, bd_ref, y_ref, wgb, wub, wdb):
    i = pl.program_id(0)

    @pl.when(i < nv_ref[0])
    def _():
        e = be_ref[i]
        prev = be_ref[jnp.maximum(i - 1, 0)]

        @pl.when((i == 0) | (e != prev))
        def _():
            wgb[...] = wg_ref[0].astype(BF16)
            wub[...] = wu_ref[0].astype(BF16)
            wdb[...] = wd_ref[0].astype(BF16)

        xb = x_ref[...].astype(BF16)
        g = jnp.minimum(_dot(xb, wgb[...]) + bg_ref[0], SWIGLU_LIMIT)
        u = jnp.clip(_dot(xb, wub[...]) + bu_ref[0], -SWIGLU_LIMIT, SWIGLU_LIMIT)
        act = g * (1.0 / (1.0 + jnp.exp(-SWIGLU_ALPHA * g))) * (u + 1.0)
        y_ref[...] = _dot(act.astype(BF16), wdb[...]) + bd_ref[0]

    @pl.when(i >= nv_ref[0])
    def _():
        y_ref[...] = jnp.zeros_like(y_ref)


def _experts(block_e, nvalid, xs, wg, bg, wu, bu, wd, bd, bm):
    n_slots = xs.shape[0]
    e, d, f = wg.shape
    nb = n_slots // bm
    xmap = lambda i, be, nv: (jnp.minimum(i, nv[0] - 1), 0)
    wmap = lambda i, be, nv: (be[i], 0, 0)
    return pl.pallas_call(
        _moe_kernel,
        grid_spec=pltpu.PrefetchScalarGridSpec(
            num_scalar_prefetch=2,
            grid=(nb,),
            in_specs=[pl.BlockSpec((bm, d), xmap),
                      pl.BlockSpec((1, d, f), wmap), pl.BlockSpec((1, 1, f), wmap),
                      pl.BlockSpec((1, d, f), wmap), pl.BlockSpec((1, 1, f), wmap),
                      pl.BlockSpec((1, f, d), wmap), pl.BlockSpec((1, 1, d), wmap)],
            out_specs=pl.BlockSpec((bm, d), lambda i, be, nv: (i, 0)),
            scratch_shapes=[pltpu.VMEM((d, f), BF16), pltpu.VMEM((d, f), BF16), pltpu.VMEM((f, d), BF16)]),
        out_shape=jax.ShapeDtypeStruct((n_slots, d), F32),
        name="experts",
        compiler_params=_cp(("arbitrary",)),
    )(block_e, nvalid, xs, wg, bg.reshape(e, 1, f), wu, bu.reshape(e, 1, f), wd, bd.reshape(e, 1, d))


def _combine_kernel(dcur_ref, dnxt_ref, ys_hbm, gates_ref, x1_ref, g2_ref, gpost_ref, o_ref, buf, sem,
                    *, tm, nsteps):
    i = pl.program_id(0)
    slot = i % 2

    def row_copy(dref, sl, g, u, k):
        d = dref[0, 0, k * tm + g * SUBLANES + u]
        return pltpu.make_async_copy(ys_hbm.at[pl.ds(d, 1), :], buf.at[sl, k, g, pl.ds(u, 1), :], sem.at[sl])

    def issue(dref, sl):
        def start(g, c):
            for u in range(SUBLANES):
                for k in range(TOP_K):
                    row_copy(dref, sl, g, u, k).start(priority=k % 2)
            return c

        lax.fori_loop(0, tm // SUBLANES, start, 0)

    @pl.when(i == 0)
    def _():
        issue(dcur_ref, 0)

    @pl.when(i + 1 < nsteps)
    def _():
        issue(dnxt_ref, 1 - slot)

    def wait(g, c):
        for u in range(SUBLANES):
            for k in range(TOP_K):
                row_copy(dcur_ref, slot, g, u, k).wait()
        return c

    lax.fori_loop(0, tm // SUBLANES, wait, 0)
    gt = gates_ref[...]
    rows = lambda k: buf[slot, k].reshape(tm, buf.shape[-1])
    y = gt[:, 0:1] * rows(0)
    for k in range(1, TOP_K):
        y = y + gt[:, k:k + 1] * rows(k)
    o_ref[...] = x1_ref[...] + g2_ref[0] * (_rms(y) * gpost_ref[...])


def _combine(dest_tiles, ys, gates_t, x1, m3, gpost, tm):
    b, n, d = x1.shape
    nt = n // tm
    nsteps = b * nt
    kern = functools.partial(_combine_kernel, tm=tm, nsteps=nsteps)
    dspec = lambda f: pl.BlockSpec((1, 1, TOP_K * tm), f, memory_space=pltpu.SMEM)
    out = pl.pallas_call(
        kern,
        grid=(nsteps,),
        in_specs=[dspec(lambda i: (i, 0, 0)),
                  dspec(lambda i: (jnp.minimum(i + 1, nsteps - 1), 0, 0)),
                  pl.BlockSpec(memory_space=pl.ANY),
                  pl.BlockSpec((tm, LANES), lambda i: (i, 0)),
                  pl.BlockSpec((tm, d), lambda i: (i, 0)),
                  pl.BlockSpec((1, 1, d), lambda i: (i // nt * N_MOD + 5, 0, 0)),
                  pl.BlockSpec(gpost.shape, lambda i: (0, 0))],
        out_specs=pl.BlockSpec((tm, d), lambda i: (i, 0)),
        out_shape=jax.ShapeDtypeStruct((b * n, d), F32),
        scratch_shapes=[pltpu.VMEM((2, TOP_K, tm // SUBLANES, SUBLANES, d), F32),
                        pltpu.SemaphoreType.DMA((2,))],
        name="combine",
        compiler_params=_cp(("arbitrary",)),
    )(dest_tiles, dest_tiles, ys, gates_t, x1.reshape(b * n, d), m3, gpost)
    return out.reshape(b, n, d)


def _rope_tables(n):
    f32 = np.float32
    rows = n // GRID_W
    row = np.repeat(np.arange(rows, dtype=f32), GRID_W)
    col = np.tile(np.arange(GRID_W, dtype=f32), rows)
    half = QK_ROPE // 2
    inv = (f32(1.0) / (f32(ROPE_BASE) ** (np.arange(0, half, 2, dtype=f32) / f32(half)))).astype(f32)
    ang = np.concatenate([row[:, None] * inv, col[:, None] * inv], axis=-1).astype(f32)
    cos, sin = np.cos(ang), np.sin(ang)
    zero = np.zeros((n, ROPE_W - QK_ROPE), f32)
    return (jnp.asarray(np.concatenate([cos, cos, zero], axis=1)),
            jnp.asarray(np.concatenate([sin, sin, zero], axis=1)))


def _ctx_tables(n):
    half = QK_ROPE // 2
    t1 = np.zeros((n, ROPE_W), np.float32)
    t1[:, :2 * half] = 1.0
    return jnp.asarray(t1), jnp.zeros((n, ROPE_W), F32)


def _prep_weights(w_in, norm_q, norm_kv, w_q_up, w_kv_up):
    d = w_in.shape[0]
    half = QK_ROPE // 2
    zr = lambda r, c: jnp.zeros((r, c), F32)
    o_rope = F_WIDTH + Q_LORA + KV_LORA
    kr = w_in[:, o_rope:o_rope + QK_ROPE]
    kre, kro = kr[:, 0::2], kr[:, 1::2]
    win = jnp.concatenate([w_in[:, :o_rope],
                           kre, kro, zr(d, ROPE_W - QK_ROPE),
                           -kro, kre, zr(d, ROPE_W - QK_ROPE)], axis=1).astype(BF16)
    hd = QK_NOPE + QK_ROPE
    qa, qb = [], []
    for h in range(N_HEADS):
        wn = w_q_up[:, h * hd:h * hd + QK_NOPE]
        wr = w_q_up[:, h * hd + QK_NOPE:(h + 1) * hd]
        we, wo = wr[:, 0::2], wr[:, 1::2]
        qa += [wn, we, wo, zr(Q_LORA, ROPE_W - QK_ROPE)]
        qb += [-wo, we, zr(Q_LORA, ROPE_W - QK_ROPE)]
    wqa = jnp.concatenate(qa, axis=1).astype(BF16)
    wqb = jnp.concatenate(qb, axis=1).astype(BF16)
    kvd = QK_NOPE + V_DIM
    wk = jnp.concatenate([w_kv_up[:, h * kvd:h * kvd + QK_NOPE] for h in range(N_HEADS)], axis=1).astype(BF16)
    wv = jnp.concatenate([w_kv_up[:, h * kvd + QK_NOPE:(h + 1) * kvd] for h in range(N_HEADS)], axis=1)
    wvt = wv.T.astype(BF16)
    return win, norm_q.reshape(1, -1), norm_kv.reshape(1, -1), wqa, wqb, wk, wvt


def _fourier_consts(n, w_fourier):
    c = F_GDIM
    j = np.arange(c)
    ang = 2.0 * np.pi * np.outer(j, j) / c
    eye = np.eye(F_GROUPS)
    bdc = np.kron(eye, np.cos(ang))
    bds = np.kron(eye, np.sin(ang))
    bdcs = jnp.asarray(np.concatenate([bdc, -bds], axis=1), dtype=BF16)
    n1 = FFT_N1
    n2 = n // n1
    k1 = np.arange(n1)
    a1 = 2.0 * np.pi * np.outer(k1, k1) / n1
    wc, ws = np.cos(a1), np.sin(a1)
    w1 = jnp.asarray(np.block([[wc, ws], [-ws, wc]]), dtype=BF16)
    tw = 2.0 * np.pi * np.outer(k1, np.arange(n2)) / n
    tc = jnp.asarray(np.repeat(np.cos(tw).astype(np.float32), F_WIDTH, axis=1))
    ts = jnp.asarray(np.repeat(np.sin(tw).astype(np.float32), F_WIDTH, axis=1))
    k2 = np.arange(n2)
    a2 = 2.0 * np.pi * np.outer(k2, k2) / n2
    norm = 1.0 / math.sqrt(n * c)
    kt = FFT_K1_TILE
    mc = np.zeros((n2, kt, kt, n2))
    ms = np.zeros((n2, kt, kt, n2))
    for r in range(kt):
        mc[:, r, r, :] = np.cos(a2) * norm
        ms[:, r, r, :] = np.sin(a2) * norm
    mmat = np.concatenate([mc.reshape(n2 * kt, kt * n2), ms.reshape(n2 * kt, kt * n2)], axis=1)
    mmat = jnp.asarray(mmat, dtype=BF16)
    zblk = jnp.zeros((c, c), F32)
    bdw = jnp.concatenate(
        [jnp.concatenate([w_fourier[g] if j == g else zblk for j in range(F_GROUPS)], axis=1)
         for g in range(F_GROUPS)], axis=0)
    return bdcs, w1, tc, ts, mmat, bdw.astype(BF16)


def kernel(x, c, ctx, c_ctx, w_mod, b_mod, norm_attn_pre, norm_attn_post, norm_ffn_pre, norm_ffn_post, w_in, norm_q_lat, norm_kv_lat, w_q_up, w_kv_up, w_fourier, w_out, w_router, b_router, w_gate, b_gate, w_up, b_up, w_down, b_down):
    b, n, d = x.shape
    nctx = ctx.shape[1]
    t = b * n
    assert w_mod.shape[0] == 1 and b + 1 <= SUBLANES and n % (FFT_N1 * SUBLANES) == 0
    row2 = lambda a: a[0].reshape(1, -1)

    c8 = jnp.concatenate([c, c_ctx[None, :], jnp.zeros((SUBLANES - b - 1, d), F32)], axis=0)
    m3 = _modulation(c8, w_mod[0], b_mod[0]).reshape(SUBLANES * N_MOD, 1, d)

    wts = _prep_weights(w_in[0], norm_q_lat[0], norm_kv_lat[0], w_q_up[0], w_kv_up[0])
    bdcs, w1, tc, ts, mmat, bdw = _fourier_consts(n, w_fourier[0])
    pro_w = (wts[0], bdcs) + wts[1:]
    scale = (QK_NOPE + QK_ROPE) ** -0.5 * LOG2E
    g_pre = row2(norm_attn_pre)

    tm = min(512, n)
    t1x, t2x = _rope_tables(n)
    q, k, vt, za, zb = _prologue(x, m3, lambda bi: bi, g_pre, pro_w, t1x, t2x, tm, scale)
    t1c, t2c = _ctx_tables(nctx)
    _, kc, vtc, _, _ = _prologue(ctx, m3, lambda bi: b, g_pre, pro_w, t1c, t2c, nctx, scale)

    att = _attention(q, k, vt, kc, vtc, min(1024, n))
    four = _fourier(za, zb, w1, tc, ts, mmat, bdw)

    tmp = min(512, n)
    tri = jnp.asarray(np.triu(np.ones((tmp, tmp), np.float32), 1), dtype=BF16)
    width = N_HEADS * V_DIM
    woa = w_out[0, :width].astype(BF16)
    wof = w_out[0, width:].astype(BF16)
    x1, hx2, ridx, gates_t, cnt = _post_attention(
        att, four, x, m3, row2(norm_attn_post), row2(norm_ffn_pre), woa, wof,
        w_router[0].T.astype(BF16), b_router[0].reshape(-1, 1), tri, tmp)

    bm = 512
    counts = cnt[:, 0].astype(I32)
    padded = (counts + bm - 1) // bm * bm
    pad_end = jnp.cumsum(padded)
    pad_start = pad_end - padded
    nb = t * TOP_K // bm + N_EXPERTS
    blk_start = jnp.arange(nb, dtype=I32)[:, None] * bm
    block_e = jnp.minimum(jnp.sum((pad_end[None, :] <= blk_start).astype(I32), axis=1), N_EXPERTS - 1)
    nvalid = (pad_end[-1:] // bm).astype(I32)
    eids = jnp.arange(N_EXPERTS, dtype=I32)[:, None, None, None]
    eidx, rank = ridx[:, :TOP_K], ridx[:, TOP_K:]
    dest = jnp.sum(jnp.where(eidx[None] == eids, pad_start[:, None, None, None], 0), axis=0) + rank
    dest_tiles = dest.reshape(t // tmp, 1, TOP_K * tmp)
    fill = jnp.concatenate([pad_start + counts, padded - counts, nvalid]).astype(I32)

    xs = _dispatch(fill, dest_tiles, hx2, nb * bm, tmp, bm)
    ys = _experts(block_e, nvalid, xs, w_gate[0], b_gate[0], w_up[0], b_up[0], w_down[0], b_down[0], bm)
    return _combine(dest_tiles, ys, gates_t, x1, m3, row2(norm_ffn_post), tmp)
```

```python
import functools
import math

import numpy as np
import jax
import jax.numpy as jnp
from jax import lax
from jax.experimental import pallas as pl
from jax.experimental.pallas import tpu as pltpu

F32 = jnp.float32
BF16 = jnp.bfloat16
I32 = jnp.int32

N_HEADS = 6
QK_NOPE = 128
QK_ROPE = 64
V_DIM = 128
Q_LORA = 384
KV_LORA = 256
F_GROUPS = 4
F_GDIM = 64
F_WIDTH = F_GROUPS * F_GDIM
GRID_W = 64
ROPE_BASE = 10000.0
N_EXPERTS = 32
TOP_K = 4
SWIGLU_LIMIT = 7.0
SWIGLU_ALPHA = 1.702
RMS_EPS = 1e-6
N_MOD = 6

LANES = 128
SUBLANES = 8
HEAD_W = 256
ROPE_W = 128
FFT_N1 = 128
FFT_K1_TILE = 8
VMEM_LIMIT = 56 * 1024 * 1024

LOG2E = 1.4426950408889634
NORM_SLACK = 1.0 + 2.0 ** -6
MIN_ROW_SUM = 2.0 ** -60


def _cp(sem, vmem=None):
    return pltpu.CompilerParams(dimension_semantics=sem, vmem_limit_bytes=vmem or VMEM_LIMIT)


def _rms(x):
    return x * lax.rsqrt(jnp.mean(x * x, axis=-1, keepdims=True) + RMS_EPS)


def _dot(a, b):
    return jnp.dot(a, b, preferred_element_type=F32)


def _stack_rows(rows, height):
    w = rows[0].shape[1]
    sub = lax.broadcasted_iota(I32, (height, w), 0)
    out = jnp.zeros((height, w), rows[0].dtype)
    for r, v in enumerate(rows):
        out = jnp.where(sub == r, v, out)
    return out


def _dot_nt(a, b):
    return lax.dot_general(a, b, (((1,), (1,)), ((), ())), preferred_element_type=F32)


def _mod_kernel(c_ref, w_ref, b_ref, o_ref):
    c = c_ref[...]
    a = c / (1.0 + jnp.exp(-c))
    a_hi = a.astype(BF16)
    a_lo = (a - a_hi.astype(F32)).astype(BF16)
    w = w_ref[...]
    w_hi = w.astype(BF16)
    w_lo = (w - w_hi.astype(F32)).astype(BF16)
    acc = _dot(a_hi, w_hi) + _dot(a_lo, w_hi) + _dot(a_hi, w_lo)
    o_ref[...] = acc + b_ref[...]


def _modulation(c8, w_mod, b_mod):
    d, n = w_mod.shape
    tn = 1024
    return pl.pallas_call(
        _mod_kernel,
        grid=(n // tn,),
        in_specs=[pl.BlockSpec((SUBLANES, d), lambda j: (0, 0)),
                  pl.BlockSpec((d, tn), lambda j: (0, j)),
                  pl.BlockSpec((1, tn), lambda j: (0, j))],
        out_specs=pl.BlockSpec((SUBLANES, tn), lambda j: (0, j)),
        out_shape=jax.ShapeDtypeStruct((SUBLANES, n), F32),
        name="modulation",
        compiler_params=_cp(("arbitrary",)),
    )(c8, w_mod, b_mod.reshape(1, n))


def _pro_kernel(x_ref, sh_ref, sc_ref, g_ref, win_ref, bdcs_ref, nq_ref, nkv_ref, wqa_ref, wqb_ref,
                wk_ref, wvt_ref, t1_ref, t2_ref, q_ref, k_ref, vt_ref, za_ref, zb_ref, *, scale):
    x = x_ref[0]
    hx = _rms(x) * g_ref[...] * (1.0 + sc_ref[0]) + sh_ref[0]
    px = _dot(hx.astype(BF16), win_ref[...])
    u = px[:, 0:F_WIDTH].astype(BF16)
    z = _dot(u, bdcs_ref[...])
    za_ref[0] = z[:, :F_WIDTH].astype(BF16)
    zb_ref[0] = z[:, F_WIDTH:].astype(BF16)
    o_q = F_WIDTH
    o_kv = o_q + Q_LORA
    o_ra = o_kv + KV_LORA
    o_rb = o_ra + ROPE_W
    t1 = t1_ref[...]
    t2 = t2_ref[...]
    qn = (_rms(px[:, o_q:o_kv]) * nq_ref[...]).astype(BF16)
    qa = _dot(qn, wqa_ref[...])
    qb = _dot(qn, wqb_ref[...])
    for h in range(N_HEADS):
        nope = qa[:, h * HEAD_W:h * HEAD_W + QK_NOPE] * scale
        rope = (qa[:, h * HEAD_W + QK_NOPE:(h + 1) * HEAD_W] * t1
                + qb[:, h * ROPE_W:(h + 1) * ROPE_W] * t2) * scale
        q_ref[0, h, :, 0:QK_NOPE] = nope.astype(BF16)
        q_ref[0, h, :, QK_NOPE:HEAD_W] = rope.astype(BF16)
    kvn = (_rms(px[:, o_kv:o_ra]) * nkv_ref[...]).astype(BF16)
    kn = _dot(kvn, wk_ref[...])
    kr = (px[:, o_ra:o_rb] * t1 + px[:, o_rb:o_rb + ROPE_W] * t2).astype(BF16)
    for h in range(N_HEADS):
        k_ref[0, h, :, 0:QK_NOPE] = kn[:, h * QK_NOPE:(h + 1) * QK_NOPE].astype(BF16)
        k_ref[0, h, :, QK_NOPE:HEAD_W] = kr
    vt = _dot_nt(wvt_ref[...], kvn)
    for h in range(N_HEADS):
        vt_ref[0, h, 0] = vt[h * V_DIM:(h + 1) * V_DIM, :].astype(BF16)


def _prologue(x, m3, mod_row, g_pre, wts, t1, t2, tm, scale):
    b, n, d = x.shape
    nt = n // tm
    win, bdcs, nq, nkv, wqa, wqb, wk, wvt = wts
    full = lambda a: pl.BlockSpec(a.shape, lambda bi, i: (0,) * a.ndim)
    kern = functools.partial(_pro_kernel, scale=scale)
    return pl.pallas_call(
        kern,
        grid=(b, nt),
        in_specs=[pl.BlockSpec((1, tm, d), lambda bi, i: (bi, i, 0)),
                  pl.BlockSpec((1, 1, d), lambda bi, i: (mod_row(bi) * N_MOD + 0, 0, 0)),
                  pl.BlockSpec((1, 1, d), lambda bi, i: (mod_row(bi) * N_MOD + 1, 0, 0)),
                  full(g_pre), full(win), full(bdcs), full(nq), full(nkv), full(wqa), full(wqb),
                  full(wk), full(wvt),
                  pl.BlockSpec((tm, ROPE_W), lambda bi, i: (i, 0)),
                  pl.BlockSpec((tm, ROPE_W), lambda bi, i: (i, 0))],
        out_specs=[pl.BlockSpec((1, N_HEADS, tm, HEAD_W), lambda bi, i: (bi, 0, i, 0)),
                   pl.BlockSpec((1, N_HEADS, tm, HEAD_W), lambda bi, i: (bi, 0, i, 0)),
                   pl.BlockSpec((1, N_HEADS, 1, V_DIM, tm), lambda bi, i: (bi, 0, i, 0, 0)),
                   pl.BlockSpec((1, tm, F_WIDTH), lambda bi, i: (bi, i, 0)),
                   pl.BlockSpec((1, tm, F_WIDTH), lambda bi, i: (bi, i, 0))],
        out_shape=[jax.ShapeDtypeStruct((b, N_HEADS, n, HEAD_W), BF16),
                   jax.ShapeDtypeStruct((b, N_HEADS, n, HEAD_W), BF16),
                   jax.ShapeDtypeStruct((b, N_HEADS, nt, V_DIM, tm), BF16),
                   jax.ShapeDtypeStruct((b, n, F_WIDTH), BF16),
                   jax.ShapeDtypeStruct((b, n, F_WIDTH), BF16)],
        name="prologue",
        compiler_params=_cp(("arbitrary", "arbitrary")),
    )(x, m3, m3, g_pre, win, bdcs, nq, nkv, wqa, wqb, wk, wvt, t1, t2)


def _attn_kernel(q_ref, k_ref, vt_ref, kc_ref, vtc_ref, o_ref, kmax_sc, m_sc, l_sc, acc_sc,
                 *, nkv, tk, group_size, wide):
    q = q_ref[0, 0]
    ones_kw = jnp.ones((HEAD_W, LANES), BF16)

    def sq_norms(kb):
        kf = kb.astype(F32)
        return jnp.max(_dot((kf * kf).astype(BF16), ones_kw), axis=0, keepdims=True)

    @pl.when(pl.program_id(2) == 0)
    def _():
        def body(j, best):
            start = pl.multiple_of(j * tk, tk)
            return jnp.maximum(best, sq_norms(k_ref[0, 0, pl.ds(start, tk), :]))

        kmax_sc[...] = lax.fori_loop(0, nkv, body, sq_norms(kc_ref[0, 0]))

    qf = q.astype(F32)
    qn2 = _dot_nt(jnp.ones((SUBLANES, HEAD_W), BF16), (qf * qf).astype(BF16))[0:1]
    ref = jnp.sqrt(qn2 * kmax_sc[:, 0:1]) * NORM_SLACK

    def fast_chunk(kb, vtb):
        p = jnp.exp2(_dot_nt(kb, q) - ref)
        l_sc[...] += jnp.sum(p, axis=0, keepdims=True)
        acc_sc[...] += _dot(vtb, p.astype(BF16))

    l_sc[...] = jnp.zeros_like(l_sc)
    acc_sc[...] = jnp.zeros_like(acc_sc)
    fast_chunk(kc_ref[0, 0], vtc_ref[0, 0, 0])

    def fast_body(i, carry):
        for u in range(0, group_size, wide):
            j = group_size * i + u
            start = pl.multiple_of(j * tk, tk)
            vtb = jnp.concatenate([vt_ref[0, 0, j + w] for w in range(wide)], axis=1)
            fast_chunk(k_ref[0, 0, pl.ds(start, wide * tk), :], vtb)
        return carry

    lax.fori_loop(0, nkv // group_size, fast_body, 0)

    @pl.when(jnp.min(l_sc[...]) < MIN_ROW_SUM)
    def _():
        s = _dot_nt(kc_ref[0, 0], q)
        m0 = jnp.max(s, axis=0, keepdims=True)
        p = jnp.exp2(s - m0)
        m_sc[...] = m0
        l_sc[...] = jnp.sum(p, axis=0, keepdims=True)
        acc_sc[...] = _dot(vtc_ref[0, 0, 0], p.astype(BF16))

        def body(j, carry):
            start = pl.multiple_of(j * tk, tk)
            s = _dot_nt(k_ref[0, 0, pl.ds(start, tk), :], q)
            m_prev = m_sc[...]
            m_new = jnp.maximum(m_prev, jnp.max(s, axis=0, keepdims=True))
            alpha = jnp.exp2(m_prev - m_new)
            p = jnp.exp2(s - m_new)
            l_sc[...] = alpha * l_sc[...] + jnp.sum(p, axis=0, keepdims=True)
            acc_sc[...] = alpha * acc_sc[...] + _dot(vt_ref[0, 0, j], p.astype(BF16))
            m_sc[...] = m_new
            return carry

        lax.fori_loop(0, nkv, body, 0)

    o = acc_sc[...] / l_sc[...]
    o_ref[0] = o.T.astype(BF16)


def _attention(q, k, vt, kc, vtc, tq):
    b, h, n, _ = q.shape
    nkv, tk = vt.shape[2], vt.shape[4]
    nctx = kc.shape[2]
    group_size = next(g for g in (16, 8, 4, 2, 1) if nkv % g == 0)
    wide = min(4, group_size)
    kern = functools.partial(_attn_kernel, nkv=nkv, tk=tk, group_size=group_size, wide=wide)
    return pl.pallas_call(
        kern,
        grid=(b, h, n // tq),
        in_specs=[pl.BlockSpec((1, 1, tq, HEAD_W), lambda bi, hi, i: (bi, hi, i, 0)),
                  pl.BlockSpec((1, 1, n, HEAD_W), lambda bi, hi, i: (bi, hi, 0, 0)),
                  pl.BlockSpec((1, 1, nkv, V_DIM, tk), lambda bi, hi, i: (bi, hi, 0, 0, 0)),
                  pl.BlockSpec((1, 1, nctx, HEAD_W), lambda bi, hi, i: (bi, hi, 0, 0)),
                  pl.BlockSpec((1, 1, 1, V_DIM, nctx), lambda bi, hi, i: (bi, hi, 0, 0, 0))],
        out_specs=pl.BlockSpec((1, tq, V_DIM), lambda bi, hi, i: (bi, i, hi)),
        out_shape=jax.ShapeDtypeStruct((b, n, h * V_DIM), BF16),
        scratch_shapes=[pltpu.VMEM((1, LANES), F32),
                        pltpu.VMEM((1, tq), F32), pltpu.VMEM((1, tq), F32),
                        pltpu.VMEM((V_DIM, tq), F32)],
        name="attention",
        compiler_params=_cp(("arbitrary", "arbitrary", "arbitrary")),
    )(q, k, vt, kc, vtc)


def _fft_a_kernel(za_ref, zb_ref, w1_ref, tc_ref, ts_ref, yr_ref, yi_ref):
    ab = jnp.concatenate([za_ref[0], zb_ref[0]], axis=0)
    y = _dot(w1_ref[...], ab)
    re = y[:FFT_N1]
    im = y[FFT_N1:]
    tc = tc_ref[...]
    ts = ts_ref[...]
    yr_ref[0] = (re * tc + im * ts).astype(BF16)
    yi_ref[0] = (im * tc - re * ts).astype(BF16)


def _fft_b_kernel(yr_ref, yi_ref, m_ref, bdw_ref, o_ref):
    ri = jnp.concatenate([yr_ref[0], yi_ref[0]], axis=0)
    f = _dot(m_ref[...], ri)
    four = _dot(f.astype(BF16), bdw_ref[...])
    o_ref[0] = four.reshape(o_ref.shape[1:])


def _fourier(za, zb, w1, tc, ts, mmat, bdw):
    b, n, c = za.shape
    n2 = n // FFT_N1
    wide = n2 * c
    tn = min(2048, wide)
    za2 = za.reshape(b, FFT_N1, wide)
    zb2 = zb.reshape(b, FFT_N1, wide)
    yr, yi = pl.pallas_call(
        _fft_a_kernel,
        grid=(wide // tn, b),
        in_specs=[pl.BlockSpec((1, FFT_N1, tn), lambda i, bi: (bi, 0, i)),
                  pl.BlockSpec((1, FFT_N1, tn), lambda i, bi: (bi, 0, i)),
                  pl.BlockSpec(w1.shape, lambda i, bi: (0, 0)),
                  pl.BlockSpec((FFT_N1, tn), lambda i, bi: (0, i)),
                  pl.BlockSpec((FFT_N1, tn), lambda i, bi: (0, i))],
        out_specs=[pl.BlockSpec((1, FFT_N1, tn), lambda i, bi: (bi, 0, i)),
                   pl.BlockSpec((1, FFT_N1, tn), lambda i, bi: (bi, 0, i))],
        out_shape=[jax.ShapeDtypeStruct((b, FFT_N1, wide), BF16)] * 2,
        name="fourier_a",
        compiler_params=_cp(("arbitrary", "arbitrary")),
    )(za2, zb2, w1, tc, ts)
    rows = FFT_K1_TILE * n2
    yr = yr.reshape(b, n, c)
    yi = yi.reshape(b, n, c)
    out = pl.pallas_call(
        _fft_b_kernel,
        grid=(b, FFT_N1 // FFT_K1_TILE),
        in_specs=[pl.BlockSpec((1, rows, c), lambda bi, i: (bi, i, 0)),
                  pl.BlockSpec((1, rows, c), lambda bi, i: (bi, i, 0)),
                  pl.BlockSpec(mmat.shape, lambda bi, i: (0, 0)),
                  pl.BlockSpec(bdw.shape, lambda bi, i: (0, 0))],
        out_specs=pl.BlockSpec((1, n2, FFT_K1_TILE, c), lambda bi, i: (bi, 0, i, 0)),
        out_shape=jax.ShapeDtypeStruct((b, n2, FFT_N1, c), F32),
        name="fourier_b",
        compiler_params=_cp(("arbitrary", "arbitrary")),
    )(yr, yi, mmat, bdw)
    return out.reshape(b, n, c)


def _post_kernel(att_ref, four_ref, x_ref, g1_ref, sh2_ref, sc2_ref, gpost_ref, gpre_ref, woa_ref, wof_ref,
                 wrt_ref, br_ref, tri_ref, x1_ref, hx_ref, ridx_ref, gates_ref, cnt_ref, carry_sc):
    first = (pl.program_id(0) == 0) & (pl.program_id(1) == 0)

    @pl.when(first)
    def _():
        carry_sc[...] = jnp.zeros_like(carry_sc)

    mix = _dot(att_ref[0], woa_ref[...]) + _dot(four_ref[0].astype(BF16), wof_ref[...])
    x1 = x_ref[0] + g1_ref[0] * (_rms(mix) * gpost_ref[...])
    x1_ref[0] = x1
    hx = _rms(x1) * gpre_ref[...] * (1.0 + sc2_ref[0]) + sh2_ref[0]
    hx_ref[...] = hx
    lg =_dot_nt(wrt_ref[...], hx.astype(BF16)) + br_ref[...]
    iota_e = lax.broadcasted_iota(I32, lg.shape, 0)
    cur = lg
    tops, idxs, ohs = [], [], []
    for _ in range(TOP_K):
        mx = jnp.max(cur, axis=0, keepdims=True)
        idx = jnp.min(jnp.where(cur == mx, iota_e, N_EXPERTS), axis=0, keepdims=True)
        oh = iota_e == idx
        cur = jnp.where(oh, -jnp.inf, cur)
        tops.append(mx)
        idxs.append(idx)
        ohs.append(oh)
    es = [jnp.exp(t - tops[0]) for t in tops]
    den = es[0] + es[1] + es[2] + es[3]
    gates = [e / den for e in es]
    ohsum = ohs[0].astype(F32) + ohs[1].astype(F32) + ohs[2].astype(F32) + ohs[3].astype(F32)
    base = carry_sc[:, 0:1] + _dot(ohsum.astype(BF16), tri_ref[...])
    ranks = [jnp.sum(jnp.where(oh, base, 0.0), axis=0, keepdims=True).astype(I32) for oh in ohs]
    ridx_ref[0] = _stack_rows(idxs + ranks, 2 * TOP_K)
    gates_ref[...] = _stack_rows(gates, LANES).T
    carry_sc[...] = carry_sc[...] + jnp.sum(ohsum, axis=1, keepdims=True)
    cnt_ref[...] = carry_sc[...]


def _post_attention(att, four, x, m3, gpost, gpre, woa, wof, wrt, br, tri, tm):
    b, n, d = x.shape
    nt = n // tm
    t = b * n
    full = lambda a: pl.BlockSpec(a.shape, lambda bi, i: (0,) * a.ndim)
    mrow = lambda j: pl.BlockSpec((1, 1, d), lambda bi, i: (bi * N_MOD + j, 0, 0))
    return pl.pallas_call(
        _post_kernel,
        grid=(b, nt),
        in_specs=[pl.BlockSpec((1, tm, att.shape[2]), lambda bi, i: (bi, i, 0)),
                  pl.BlockSpec((1, tm, F_WIDTH), lambda bi, i: (bi, i, 0)),
                  pl.BlockSpec((1, tm, d), lambda bi, i: (bi, i, 0)),
                  mrow(2), mrow(3), mrow(4),
                  full(gpost), full(gpre), full(woa), full(wof), full(wrt), full(br), full(tri)],
        out_specs=[pl.BlockSpec((1, tm, d), lambda bi, i: (bi, i, 0)),
                   pl.BlockSpec((tm, d), lambda bi, i: (bi * nt + i, 0)),
                   pl.BlockSpec((1, 2 * TOP_K, tm), lambda bi, i: (bi * nt + i, 0, 0)),
                   pl.BlockSpec((tm, LANES), lambda bi, i: (bi * nt + i, 0)),
                   pl.BlockSpec((N_EXPERTS, LANES), lambda bi, i: (0, 0))],
        out_shape=[jax.ShapeDtypeStruct((b, n, d), F32),
                   jax.ShapeDtypeStruct((t, d), F32),
                   jax.ShapeDtypeStruct((t // tm, 2 * TOP_K, tm), I32),
                   jax.ShapeDtypeStruct((t, LANES), F32),
                   jax.ShapeDtypeStruct((N_EXPERTS, LANES), F32)],
        scratch_shapes=[pltpu.VMEM((N_EXPERTS, LANES), F32)],
        name="post_attention",
        compiler_params=_cp(("arbitrary", "arbitrary")),
    )(att, four, x, m3, m3, m3, gpost, gpre, woa, wof, wrt, br, tri)


def _dispatch_kernel(fill_ref, dest_ref, hx_ref, xs_hbm, zrow, sem, fsem, *, tm, bm):
    i = pl.program_id(0)

    @pl.when(i == 0)
    def _():
        zrow[...] = jnp.zeros_like(zrow)
        nblk = xs_hbm.shape[0] // bm

        def tail(blk, c):
            cp = pltpu.make_async_copy(zrow, xs_hbm.at[pl.ds(blk * bm, bm), :], fsem)
            cp.start()
            cp.wait()
            return c

        lax.fori_loop(fill_ref[2 * N_EXPERTS], nblk, tail, 0)

        def per_expert(e, c):
            s = fill_ref[e]
            n = fill_ref[N_EXPERTS + e]
            head = jnp.minimum(n, (-s) & (SUBLANES - 1))
            body = s + head
            nchunk = (n - head) // SUBLANES

            def row_copy(r):
                return pltpu.make_async_copy(zrow.at[pl.ds(0, 1), :], xs_hbm.at[pl.ds(s + r, 1), :], fsem)

            def chunk_copy(j):
                dst = pl.multiple_of(body + j * SUBLANES, SUBLANES)
                return pltpu.make_async_copy(zrow.at[pl.ds(0, SUBLANES), :], xs_hbm.at[pl.ds(dst, SUBLANES), :], fsem)

            def loop(n_it, copy, action):
                def step(r, cc):
                    action(copy(r))
                    return cc

                lax.fori_loop(0, n_it, step, 0)

            loop(head, row_copy, lambda cp: cp.start())
            loop(nchunk, chunk_copy, lambda cp: cp.start())
            loop(head, row_copy, lambda cp: cp.wait())
            loop(nchunk, chunk_copy, lambda cp: cp.wait())
            return c

        lax.fori_loop(0, N_EXPERTS, per_expert, 0)

    def row_copy(g, u, k):
        d = dest_ref[0, 0, k * tm + g * SUBLANES + u]
        return pltpu.make_async_copy(hx_ref.at[g, pl.ds(u, 1), :], xs_hbm.at[pl.ds(d, 1), :], sem)

    def start(g, c):
        for u in range(SUBLANES):
            for k in range(TOP_K):
                row_copy(g, u, k).start(priority=k % 2)
        return c

    def wait(g, c):
        for u in range(SUBLANES):
            for k in range(TOP_K):
                row_copy(g, u, k).wait()
        return c

    lax.fori_loop(0, tm // SUBLANES, start, 0)
    lax.fori_loop(0, tm // SUBLANES, wait, 0)


def _dispatch(fill, dest_tiles, hx, n_slots, tm, bm):
    t, dm = hx.shape
    kern = functools.partial(_dispatch_kernel, tm=tm, bm=bm)
    return pl.pallas_call(
        kern,
        grid_spec=pltpu.PrefetchScalarGridSpec(
            num_scalar_prefetch=1,
            grid=(t // tm,),
            in_specs=[pl.BlockSpec((1, 1, TOP_K * tm), lambda i, f: (i, 0, 0), memory_space=pltpu.SMEM),
                      pl.BlockSpec((tm // SUBLANES, SUBLANES, dm), lambda i, f: (i, 0, 0))],
            out_specs=pl.BlockSpec(memory_space=pl.ANY),
            scratch_shapes=[pltpu.VMEM((bm, dm), F32),
                            pltpu.SemaphoreType.DMA(()), pltpu.SemaphoreType.DMA(())]),
        out_shape=jax.ShapeDtypeStruct((n_slots, dm), F32),
        name="dispatch",
        compiler_params=_cp(("arbitrary",)),
    )(fill, dest_tiles, hx.reshape(t // SUBLANES, SUBLANES, dm))


def _moe_kernel(be_ref, nv_ref, x_ref, wg_ref, bg_ref, wu_ref, bu_ref, wd_ref, bd_ref, y_ref, wgb, wub, wdb):
    i = pl.program_id(0)

    @pl.when(i < nv_ref[0])
    def _():
        e = be_ref[i]
        prev = be_ref[jnp.maximum(i - 1, 0)]

        @pl.when((i == 0) | (e != prev))
        def _():
            wgb[...] = wg_ref[0].astype(BF16)
            wub[...] = wu_ref[0].astype(BF16)
            wdb[...] = wd_ref[0].astype(BF16)

        xb = x_ref[...].astype(BF16)
        g = jnp.minimum(_dot(xb, wgb[...]) + bg_ref[0], SWIGLU_LIMIT)
        u = jnp.clip(_dot(xb, wub[...]) + bu_ref[0], -SWIGLU_LIMIT, SWIGLU_LIMIT)
        act = g * (1.0 / (1.0 + jnp.exp(-SWIGLU_ALPHA * g))) * (u + 1.0)
        y_ref[...] = _dot(act.astype(BF16), wdb[...]) + bd_ref[0]

    @pl.when(i >= nv_ref[0])
    def _():
        y_ref[...] = jnp.zeros_like(y_ref)


def _experts(block_e, nvalid, xs, wg, bg, wu, bu, wd, bd, bm):
    n_slots = xs.shape[0]
    e, d, f = wg.shape
    nb = n_slots // bm
    xmap = lambda i, be, nv: (jnp.minimum(i, nv[0] - 1), 0)
    wmap = lambda i, be, nv: (be[i], 0, 0)
    return pl.pallas_call(
        _moe_kernel,
        grid_spec=pltpu.PrefetchScalarGridSpec(
            num_scalar_prefetch=2,
            grid=(nb,),
            in_specs=[pl.BlockSpec((bm, d), xmap),
                      pl.BlockSpec((1, d, f), wmap), pl.BlockSpec((1, 1, f), wmap),
                      pl.BlockSpec((1, d, f), wmap), pl.BlockSpec((1, 1, f), wmap),
                      pl.BlockSpec((1, f, d), wmap), pl.BlockSpec((1, 1, d), wmap)],
            out_specs=pl.BlockSpec((bm, d), lambda i, be, nv: (i, 0)),
            scratch_shapes=[pltpu.VMEM((d, f), BF16), pltpu.VMEM((d, f), BF16), pltpu.VMEM((f, d), BF16)]),
        out_shape=jax.ShapeDtypeStruct((n_slots, d), F32),
        name="experts",
        compiler_params=_cp(("arbitrary",)),
    )(block_e, nvalid, xs, wg, bg.reshape(e, 1, f), wu, bu.reshape(e, 1, f), wd, bd.reshape(e, 1, d))


def _combine_kernel(dcur_ref, dnxt_ref, ys_hbm, gates_ref, x1_ref, g2_ref, gpost_ref, o_ref, buf, sem,
                    *, tm, nsteps):
    i = pl.program_id(0)
    slot = i % 2

    def row_copy(dref, sl, g, u, k):
        d = dref[0, 0, k * tm + g * SUBLANES + u]
        return pltpu.make_async_copy(ys_hbm.at[pl.ds(d, 1), :], buf.at[sl, k, g, pl.ds(u, 1), :], sem.at[sl])

    def issue(dref, sl):
        def start(g, c):
            for u in range(SUBLANES):
                for k in range(TOP_K):
                    row_copy(dref, sl, g, u, k).start(priority=k % 2)
            return c

        lax.fori_loop(0, tm // SUBLANES, start, 0)

    @pl.when(i == 0)
    def _():
        issue(dcur_ref, 0)

    @pl.when(i + 1 < nsteps)
    def _():
        issue(dnxt_ref, 1 - slot)

    def wait(g, c):
        for u in range(SUBLANES):
            for k in range(TOP_K):
                row_copy(dcur_ref, slot, g, u, k).wait()
        return c

    lax.fori_loop(0, tm // SUBLANES, wait, 0)
    gt = gates_ref[...]
    rows = lambda k: buf[slot, k].reshape(tm, buf.shape[-1])
    y = gt[:, 0:1] * rows(0)
    for k in range(1, TOP_K):
        y = y + gt[:, k:k + 1] * rows(k)
    o_ref[...] = x1_ref[...] + g2_ref[0] * (_rms(y) * gpost_ref[...])


def _combine(dest_tiles, ys, gates_t, x1, m3, gpost, tm):
    b, n, d = x1.shape
    nt = n // tm
    nsteps = b * nt
    kern = functools.partial(_combine_kernel, tm=tm, nsteps=nsteps)
    dspec = lambda f: pl.BlockSpec((1, 1, TOP_K * tm), f, memory_space=pltpu.SMEM)
    out = pl.pallas_call(
        kern,
        grid=(nsteps,),
        in_specs=[dspec(lambda i: (i, 0, 0)),
                  dspec(lambda i: (jnp.minimum(i + 1, nsteps - 1), 0, 0)),
                  pl.BlockSpec(memory_space=pl.ANY),
                  pl.BlockSpec((tm, LANES), lambda i: (i, 0)),
                  pl.BlockSpec((tm, d), lambda i: (i, 0)),
                  pl.BlockSpec((1, 1, d), lambda i: (i // nt * N_MOD + 5, 0, 0)),
                  pl.BlockSpec(gpost.shape, lambda i: (0, 0))],
        out_specs=pl.BlockSpec((tm, d), lambda i: (i, 0)),
        out_shape=jax.ShapeDtypeStruct((b * n, d), F32),
        scratch_shapes=[pltpu.VMEM((2, TOP_K, tm // SUBLANES, SUBLANES, d), F32),
                        pltpu.SemaphoreType.DMA((2,))],
        name="combine",
        compiler_params=_cp(("arbitrary",)),
    )(dest_tiles, dest_tiles, ys, gates_t, x1.reshape(b * n, d), m3, gpost)
    return out.reshape(b, n, d)


def _rope_tables(n):
    f32 = np.float32
    rows = n // GRID_W
    row = np.repeat(np.arange(rows, dtype=f32), GRID_W)
    col = np.tile(np.arange(GRID_W, dtype=f32), rows)
    half = QK_ROPE // 2
    inv = (f32(1.0) / (f32(ROPE_BASE) ** (np.arange(0, half, 2, dtype=f32) / f32(half)))).astype(f32)
    ang = np.concatenate([row[:, None] * inv, col[:, None] * inv], axis=-1).astype(f32)
    cos, sin = np.cos(ang), np.sin(ang)
    zero = np.zeros((n, ROPE_W - QK_ROPE), f32)
    return (jnp.asarray(np.concatenate([cos, cos, zero], axis=1)),
            jnp.asarray(np.concatenate([sin, sin, zero], axis=1)))


def _ctx_tables(n):
    half = QK_ROPE // 2
    t1 = np.zeros((n, ROPE_W), np.float32)
    t1[:, :2 * half] = 1.0
    return jnp.asarray(t1), jnp.zeros((n, ROPE_W), F32)


def _prep_weights(w_in, norm_q, norm_kv, w_q_up, w_kv_up):
    d = w_in.shape[0]
    half = QK_ROPE // 2
    zr = lambda r, c: jnp.zeros((r, c), F32)
    o_rope = F_WIDTH + Q_LORA + KV_LORA
    kr = w_in[:, o_rope:o_rope + QK_ROPE]
    kre, kro = kr[:, 0::2], kr[:, 1::2]
    win = jnp.concatenate([w_in[:, :o_rope],
                           kre, kro, zr(d, ROPE_W - QK_ROPE),
                           -kro, kre, zr(d, ROPE_W - QK_ROPE)], axis=1).astype(BF16)
    hd = QK_NOPE + QK_ROPE
    qa, qb = [], []
    for h in range(N_HEADS):
        wn = w_q_up[:, h * hd:h * hd + QK_NOPE]
        wr = w_q_up[:, h * hd + QK_NOPE:(h + 1) * hd]
        we, wo = wr[:, 0::2], wr[:, 1::2]
        qa += [wn, we, wo, zr(Q_LORA, ROPE_W - QK_ROPE)]
        qb += [-wo, we, zr(Q_LORA, ROPE_W - QK_ROPE)]
    wqa = jnp.concatenate(qa, axis=1).astype(BF16)
    wqb = jnp.concatenate(qb, axis=1).astype(BF16)
    kvd = QK_NOPE + V_DIM
    wk = jnp.concatenate([w_kv_up[:, h * kvd:h * kvd + QK_NOPE] for h in range(N_HEADS)], axis=1).astype(BF16)
    wv = jnp.concatenate([w_kv_up[:, h * kvd + QK_NOPE:(h + 1) * kvd] for h in range(N_HEADS)], axis=1)
    wvt = wv.T.astype(BF16)
    return win, norm_q.reshape(1, -1), norm_kv.reshape(1, -1), wqa, wqb, wk, wvt


def _fourier_consts(n, w_fourier):
    c = F_GDIM
    j = np.arange(c)
    ang = 2.0 * np.pi * np.outer(j, j) / c
    eye = np.eye(F_GROUPS)
    bdc = np.kron(eye, np.cos(ang))
    bds = np.kron(eye, np.sin(ang))
    bdcs = jnp.asarray(np.concatenate([bdc, -bds], axis=1), dtype=BF16)
    n1 = FFT_N1
    n2 = n // n1
    k1 = np.arange(n1)
    a1 = 2.0 * np.pi * np.outer(k1, k1) / n1
    wc, ws = np.cos(a1), np.sin(a1)
    w1 = jnp.asarray(np.block([[wc, ws], [-ws, wc]]), dtype=BF16)
    tw = 2.0 * np.pi * np.outer(k1, np.arange(n2)) / n
    tc = jnp.asarray(np.repeat(np.cos(tw).astype(np.float32), F_WIDTH, axis=1))
    ts = jnp.asarray(np.repeat(np.sin(tw).astype(np.float32), F_WIDTH, axis=1))
    k2 = np.arange(n2)
    a2 = 2.0 * np.pi * np.outer(k2, k2) / n2
    norm = 1.0 / math.sqrt(n * c)
    kt = FFT_K1_TILE
    mc = np.zeros((n2, kt, kt, n2))
    ms = np.zeros((n2, kt, kt, n2))
    for r in range(kt):
        mc[:, r, r, :] = np.cos(a2) * norm
        ms[:, r, r, :] = np.sin(a2) * norm
    mmat = np.concatenate([mc.reshape(n2 * kt, kt * n2), ms.reshape(n2 * kt, kt * n2)], axis=1)
    mmat = jnp.asarray(mmat, dtype=BF16)
    zblk = jnp.zeros((c, c), F32)
    bdw = jnp.concatenate(
        [jnp.concatenate([w_fourier[g] if j == g else zblk for j in range(F_GROUPS)], axis=1)
         for g in range(F_GROUPS)], axis=0)
    return bdcs, w1, tc, ts, mmat, bdw.astype(BF16)


def kernel(x, c, ctx, c_ctx, w_mod, b_mod, norm_attn_pre, norm_attn_post, norm_ffn_pre, norm_ffn_post, w_in, norm_q_lat, norm_kv_lat, w_q_up, w_kv_up, w_fourier, w_out, w_router, b_router, w_gate, b_gate, w_up, b_up, w_down, b_down):
    b, n, d = x.shape
    nctx = ctx.shape[1]
    t = b * n
    assert w_mod.shape[0] == 1 and b + 1 <= SUBLANES and n % (FFT_N1 * SUBLANES) == 0
    row2 = lambda a: a[0].reshape(1, -1)

    c8 = jnp.concatenate([c, c_ctx[None, :], jnp.zeros((SUBLANES - b - 1, d), F32)], axis=0)
    m3 = _modulation(c8, w_mod[0], b_mod[0]).reshape(SUBLANES * N_MOD, 1, d)

    wts = _prep_weights(w_in[0], norm_q_lat[0], norm_kv_lat[0], w_q_up[0], w_kv_up[0])
    bdcs, w1, tc, ts, mmat, bdw = _fourier_consts(n, w_fourier[0])
    pro_w = (wts[0], bdcs) + wts[1:]
    scale = (QK_NOPE + QK_ROPE) ** -0.5 * LOG2E
    g_pre = row2(norm_attn_pre)

    tm = min(512, n)
    t1x, t2x = _rope_tables(n)
    q, k, vt, za, zb = _prologue(x, m3, lambda bi: bi, g_pre, pro_w, t1x, t2x, tm, scale)
    t1c, t2c = _ctx_tables(nctx)
    _, kc, vtc, _, _ = _prologue(ctx, m3, lambda bi: b, g_pre, pro_w, t1c, t2c, nctx, scale)

    att = _attention(q, k, vt, kc, vtc, min(1024, n))
    four = _fourier(za, zb, w1, tc, ts, mmat, bdw)

    tmp = min(512, n)
    tri = jnp.asarray(np.triu(np.ones((tmp, tmp), np.float32), 1), dtype=BF16)
    width = N_HEADS * V_DIM
    woa = w_out[0, :width].astype(BF16)
    wof = w_out[0, width:].astype(BF16)
    x1, hx2, ridx, gates_t, cnt = _post_attention(
        att, four, x, m3, row2(norm_attn_post), row2(norm_ffn_pre), woa, wof,
        w_router[0].T.astype(BF16), b_router[0].reshape(-1, 1), tri, tmp)

    bm = 512
    counts = cnt[:, 0].astype(I32)
    padded = (counts + bm - 1) // bm * bm
    pad_end = jnp.cumsum(padded)
    pad_start = pad_end - padded
    nb = t * TOP_K // bm + N_EXPERTS
    blk_start = jnp.arange(nb, dtype=I32)[:, None] * bm
    block_e = jnp.minimum(jnp.sum((pad_end[None, :] <= blk_start).astype(I32), axis=1), N_EXPERTS - 1)
    nvalid = (pad_end[-1:] // bm).astype(I32)
    eids = jnp.arange(N_EXPERTS, dtype=I32)[:, None, None, None]
    eidx, rank = ridx[:, :TOP_K], ridx[:, TOP_K:]
    dest = jnp.sum(jnp.where(eidx[None] == eids, pad_start[:, None, None, None], 0), axis=0) + rank
    dest_tiles = dest.reshape(t // tmp, 1, TOP_K * tmp)
    fill = jnp.concatenate([pad_start + counts, padded - counts, nvalid]).astype(I32)

    xs = _dispatch(fill, dest_tiles, hx2, nb * bm, tmp, bm)
    ys = _experts(block_e, nvalid, xs, w_gate[0], b_gate[0], w_up[0], b_up[0], w_down[0], b_down[0], bm)
    return _combine(dest_tiles, ys, gates_t, x1, m3, row2(norm_ffn_post), tmp)
```

```python
import functools
import math

import numpy as np
import jax
import jax.numpy as jnp
from jax import lax
from jax.experimental import pallas as pl
from jax.experimental.pallas import tpu as pltpu

F32 = jnp.float32
BF16 = jnp.bfloat16
I32 = jnp.int32
U32 = jnp.uint32

N_HEADS = 6
QK_NOPE = 128
QK_ROPE = 64
V_DIM = 128
Q_LORA = 384
KV_LORA = 256
F_GROUPS = 4
F_GDIM = 64
F_WIDTH = F_GROUPS * F_GDIM
GRID_W = 64
ROPE_BASE = 10000.0
N_EXPERTS = 32
TOP_K = 4
SWIGLU_LIMIT = 7.0
SWIGLU_ALPHA = 1.702
RMS_EPS = 1e-6
N_MOD = 6

LANES = 128
SUBLANES = 8
HEAD_W = 256
ROPE_W = 128
FFT_N1 = 128
FFT_K1_TILE = 8
VMEM_LIMIT = 56 * 1024 * 1024

LOG2E = 1.4426950408889634
NORM_SLACK = 1.0 + 2.0 ** -6
MIN_ROW_SUM = 2.0 ** -60


def _cp(sem, vmem=None):
    return pltpu.CompilerParams(dimension_semantics=sem, vmem_limit_bytes=vmem or VMEM_LIMIT)


def _rms(x):
    return x * lax.rsqrt(jnp.mean(x * x, axis=-1, keepdims=True) + RMS_EPS)


def _dot(a, b):
    return jnp.dot(a, b, preferred_element_type=F32)


def _stack_rows(rows, height):
    w = rows[0].shape[1]
    sub = lax.broadcasted_iota(I32, (height, w), 0)
    out = jnp.zeros((height, w), rows[0].dtype)
    for r, v in enumerate(rows):
        out = jnp.where(sub == r, v, out)
    return out


def _pack_bf16_pairs(xb):
    w = xb.shape[1] // 2
    lo = lax.bitcast_convert_type(xb[:, :w].astype(F32), U32)
    hi = lax.bitcast_convert_type(xb[:, w:].astype(F32), U32)
    return (hi & jnp.uint32(0xFFFF0000)) | (lo >> 16)


def _unpack_bf16_pairs(p):
    lo = lax.bitcast_convert_type(p << 16, F32)
    hi = lax.bitcast_convert_type(p & jnp.uint32(0xFFFF0000), F32)
    return jnp.concatenate([lo, hi], axis=1).astype(BF16)


def _dot_nt(a, b):
    return lax.dot_general(a, b, (((1,), (1,)), ((), ())), preferred_element_type=F32)


def _mod_kernel(c_ref, w_ref, b_ref, o_ref):
    c = c_ref[...]
    a = c / (1.0 + jnp.exp(-c))
    a_hi = a.astype(BF16)
    a_lo = (a - a_hi.astype(F32)).astype(BF16)
    w = w_ref[...]
    w_hi = w.astype(BF16)
    w_lo = (w - w_hi.astype(F32)).astype(BF16)
    acc = _dot(a_hi, w_hi) + _dot(a_lo, w_hi) + _dot(a_hi, w_lo)
    o_ref[...] = acc + b_ref[...]


def _modulation(c8, w_mod, b_mod):
    d, n = w_mod.shape
    tn = 1024
    return pl.pallas_call(
        _mod_kernel,
        grid=(n // tn,),
        in_specs=[pl.BlockSpec((SUBLANES, d), lambda j: (0, 0)),
                  pl.BlockSpec((d, tn), lambda j: (0, j)),
                  pl.BlockSpec((1, tn), lambda j: (0, j))],
        out_specs=pl.BlockSpec((SUBLANES, tn), lambda j: (0, j)),
        out_shape=jax.ShapeDtypeStruct((SUBLANES, n), F32),
        name="modulation",
        compiler_params=_cp(("arbitrary",)),
    )(c8, w_mod, b_mod.reshape(1, n))


def _pro_kernel(x_ref, sh_ref, sc_ref, g_ref, win_ref, bdcs_ref, nq_ref, nkv_ref, wqa_ref, wqb_ref,
                wk_ref, wvt_ref, t1_ref, t2_ref, q_ref, k_ref, vt_ref, za_ref, zb_ref, *, scale):
    x = x_ref[0]
    hx = _rms(x) * g_ref[...] * (1.0 + sc_ref[0]) + sh_ref[0]
    px = _dot(hx.astype(BF16), win_ref[...])
    u = px[:, 0:F_WIDTH].astype(BF16)
    z = _dot(u, bdcs_ref[...])
    za_ref[0] = z[:, :F_WIDTH].astype(BF16)
    zb_ref[0] = z[:, F_WIDTH:].astype(BF16)
    o_q = F_WIDTH
    o_kv = o_q + Q_LORA
    o_ra = o_kv + KV_LORA
    o_rb = o_ra + ROPE_W
    t1 = t1_ref[...]
    t2 = t2_ref[...]
    qn = (_rms(px[:, o_q:o_kv]) * nq_ref[...]).astype(BF16)
    qa = _dot(qn, wqa_ref[...])
    qb = _dot(qn, wqb_ref[...])
    for h in range(N_HEADS):
        nope = qa[:, h * HEAD_W:h * HEAD_W + QK_NOPE] * scale
        rope = (qa[:, h * HEAD_W + QK_NOPE:(h + 1) * HEAD_W] * t1
                + qb[:, h * ROPE_W:(h + 1) * ROPE_W] * t2) * scale
        q_ref[0, h, :, 0:QK_NOPE] = nope.astype(BF16)
        q_ref[0, h, :, QK_NOPE:HEAD_W] = rope.astype(BF16)
    kvn = (_rms(px[:, o_kv:o_ra]) * nkv_ref[...]).astype(BF16)
    kn = _dot(kvn, wk_ref[...])
    kr = (px[:, o_ra:o_rb] * t1 + px[:, o_rb:o_rb + ROPE_W] * t2).astype(BF16)
    for h in range(N_HEADS):
        k_ref[0, h, :, 0:QK_NOPE] = kn[:, h * QK_NOPE:(h + 1) * QK_NOPE].astype(BF16)
        k_ref[0, h, :, QK_NOPE:HEAD_W] = kr
    vt = _dot_nt(wvt_ref[...], kvn)
    for h in range(N_HEADS):
        vt_ref[0, h, 0] = vt[h * V_DIM:(h + 1) * V_DIM, :].astype(BF16)


def _prologue(x, m3, mod_row, g_pre, wts, t1, t2, tm, scale):
    b, n, d = x.shape
    nt = n // tm
    win, bdcs, nq, nkv, wqa, wqb, wk, wvt = wts
    full = lambda a: pl.BlockSpec(a.shape, lambda bi, i: (0,) * a.ndim)
    kern = functools.partial(_pro_kernel, scale=scale)
    return pl.pallas_call(
        kern,
        grid=(b, nt),
        in_specs=[pl.BlockSpec((1, tm, d), lambda bi, i: (bi, i, 0)),
                  pl.BlockSpec((1, 1, d), lambda bi, i: (mod_row(bi) * N_MOD + 0, 0, 0)),
                  pl.BlockSpec((1, 1, d), lambda bi, i: (mod_row(bi) * N_MOD + 1, 0, 0)),
                  full(g_pre), full(win), full(bdcs), full(nq), full(nkv), full(wqa), full(wqb),
                  full(wk), full(wvt),
                  pl.BlockSpec((tm, ROPE_W), lambda bi, i: (i, 0)),
                  pl.BlockSpec((tm, ROPE_W), lambda bi, i: (i, 0))],
        out_specs=[pl.BlockSpec((1, N_HEADS, tm, HEAD_W), lambda bi, i: (bi, 0, i, 0)),
                   pl.BlockSpec((1, N_HEADS, tm, HEAD_W), lambda bi, i: (bi, 0, i, 0)),
                   pl.BlockSpec((1, N_HEADS, 1, V_DIM, tm), lambda bi, i: (bi, 0, i, 0, 0)),
                   pl.BlockSpec((1, tm, F_WIDTH), lambda bi, i: (bi, i, 0)),
                   pl.BlockSpec((1, tm, F_WIDTH), lambda bi, i: (bi, i, 0))],
        out_shape=[jax.ShapeDtypeStruct((b, N_HEADS, n, HEAD_W), BF16),
                   jax.ShapeDtypeStruct((b, N_HEADS, n, HEAD_W), BF16),
                   jax.ShapeDtypeStruct((b, N_HEADS, nt, V_DIM, tm), BF16),
                   jax.ShapeDtypeStruct((b, n, F_WIDTH), BF16),
                   jax.ShapeDtypeStruct((b, n, F_WIDTH), BF16)],
        name="prologue",
        compiler_params=_cp(("arbitrary", "arbitrary")),
    )(x, m3, m3, g_pre, win, bdcs, nq, nkv, wqa, wqb, wk, wvt, t1, t2)


def _attn_kernel(q_ref, k_ref, vt_ref, kc_ref, vtc_ref, o_ref, kmax_sc, m_sc, l_sc, acc_sc,
                 *, nkv, tk, group_size, wide):
    q = q_ref[0, 0]
    ones_kw = jnp.ones((HEAD_W, LANES), BF16)

    def sq_norms(kb):
        kf = kb.astype(F32)
        return jnp.max(_dot((kf * kf).astype(BF16), ones_kw), axis=0, keepdims=True)

    @pl.when(pl.program_id(2) == 0)
    def _():
        def body(j, best):
            start = pl.multiple_of(j * tk, tk)
            return jnp.maximum(best, sq_norms(k_ref[0, 0, pl.ds(start, tk), :]))

        kmax_sc[...] = lax.fori_loop(0, nkv, body, sq_norms(kc_ref[0, 0]))

    qf = q.astype(F32)
    qn2 = _dot_nt(jnp.ones((SUBLANES, HEAD_W), BF16), (qf * qf).astype(BF16))[0:1]
    ref = jnp.sqrt(qn2 * kmax_sc[:, 0:1]) * NORM_SLACK

    def fast_chunk(kb, vtb):
        p = jnp.exp2(_dot_nt(kb, q) - ref)
        l_sc[...] += jnp.sum(p, axis=0, keepdims=True)
        acc_sc[...] += _dot(vtb, p.astype(BF16))

    l_sc[...] = jnp.zeros_like(l_sc)
    acc_sc[...] = jnp.zeros_like(acc_sc)
    fast_chunk(kc_ref[0, 0], vtc_ref[0, 0, 0])

    def fast_body(i, carry):
        for u in range(0, group_size, wide):
            j = group_size * i + u
            start = pl.multiple_of(j * tk, tk)
            vtb = jnp.concatenate([vt_ref[0, 0, j + w] for w in range(wide)], axis=1)
            fast_chunk(k_ref[0, 0, pl.ds(start, wide * tk), :], vtb)
        return carry

    lax.fori_loop(0, nkv // group_size, fast_body, 0)

    @pl.when(jnp.min(l_sc[...]) < MIN_ROW_SUM)
    def _():
        s = _dot_nt(kc_ref[0, 0], q)
        m0 = jnp.max(s, axis=0, keepdims=True)
        p = jnp.exp2(s - m0)
        m_sc[...] = m0
        l_sc[...] = jnp.sum(p, axis=0, keepdims=True)
        acc_sc[...] = _dot(vtc_ref[0, 0, 0], p.astype(BF16))

        def body(j, carry):
            start = pl.multiple_of(j * tk, tk)
            s = _dot_nt(k_ref[0, 0, pl.ds(start, tk), :], q)
            m_prev = m_sc[...]
            m_new = jnp.maximum(m_prev, jnp.max(s, axis=0, keepdims=True))
            alpha = jnp.exp2(m_prev - m_new)
            p = jnp.exp2(s - m_new)
            l_sc[...] = alpha * l_sc[...] + jnp.sum(p, axis=0, keepdims=True)
            acc_sc[...] = alpha * acc_sc[...] + _dot(vt_ref[0, 0, j], p.astype(BF16))
            m_sc[...] = m_new
            return carry

        lax.fori_loop(0, nkv, body, 0)

    o = acc_sc[...] / l_sc[...]
    o_ref[0] = o.T.astype(BF16)


def _attention(q, k, vt, kc, vtc, tq):
    b, h, n, _ = q.shape
    nkv, tk = vt.shape[2], vt.shape[4]
    nctx = kc.shape[2]
    group_size = next(g for g in (16, 8, 4, 2, 1) if nkv % g == 0)
    wide = min(4, group_size)
    kern = functools.partial(_attn_kernel, nkv=nkv, tk=tk, group_size=group_size, wide=wide)
    return pl.pallas_call(
        kern,
        grid=(b, h, n // tq),
        in_specs=[pl.BlockSpec((1, 1, tq, HEAD_W), lambda bi, hi, i: (bi, hi, i, 0)),
                  pl.BlockSpec((1, 1, n, HEAD_W), lambda bi, hi, i: (bi, hi, 0, 0)),
                  pl.BlockSpec((1, 1, nkv, V_DIM, tk), lambda bi, hi, i: (bi, hi, 0, 0, 0)),
                  pl.BlockSpec((1, 1, nctx, HEAD_W), lambda bi, hi, i: (bi, hi, 0, 0)),
                  pl.BlockSpec((1, 1, 1, V_DIM, nctx), lambda bi, hi, i: (bi, hi, 0, 0, 0))],
        out_specs=pl.BlockSpec((1, tq, V_DIM), lambda bi, hi, i: (bi, i, hi)),
        out_shape=jax.ShapeDtypeStruct((b, n, h * V_DIM), BF16),
        scratch_shapes=[pltpu.VMEM((1, LANES), F32),
                        pltpu.VMEM((1, tq), F32), pltpu.VMEM((1, tq), F32),
                        pltpu.VMEM((V_DIM, tq), F32)],
        name="attention",
        compiler_params=_cp(("arbitrary", "arbitrary", "arbitrary")),
    )(q, k, vt, kc, vtc)


def _fft_a_kernel(za_ref, zb_ref, w1_ref, tc_ref, ts_ref, yr_ref, yi_ref):
    ab = jnp.concatenate([za_ref[0], zb_ref[0]], axis=0)
    y = _dot(w1_ref[...], ab)
    re = y[:FFT_N1]
    im = y[FFT_N1:]
    tc = tc_ref[...]
    ts = ts_ref[...]
    yr_ref[0] = (re * tc + im * ts).astype(BF16)
    yi_ref[0] = (im * tc - re * ts).astype(BF16)


def _fft_b_kernel(yr_ref, yi_ref, m_ref, bdw_ref, o_ref):
    ri = jnp.concatenate([yr_ref[0], yi_ref[0]], axis=0)
    f = _dot(m_ref[...], ri)
    four = _dot(f.astype(BF16), bdw_ref[...])
    o_ref[0] = four.reshape(o_ref.shape[1:])


def _fourier(za, zb, w1, tc, ts, mmat, bdw):
    b, n, c = za.shape
    n2 = n // FFT_N1
    wide = n2 * c
    tn = min(2048, wide)
    za2 = za.reshape(b, FFT_N1, wide)
    zb2 = zb.reshape(b, FFT_N1, wide)
    yr, yi = pl.pallas_call(
        _fft_a_kernel,
        grid=(wide // tn, b),
        in_specs=[pl.BlockSpec((1, FFT_N1, tn), lambda i, bi: (bi, 0, i)),
                  pl.BlockSpec((1, FFT_N1, tn), lambda i, bi: (bi, 0, i)),
                  pl.BlockSpec(w1.shape, lambda i, bi: (0, 0)),
                  pl.BlockSpec((FFT_N1, tn), lambda i, bi: (0, i)),
                  pl.BlockSpec((FFT_N1, tn), lambda i, bi: (0, i))],
        out_specs=[pl.BlockSpec((1, FFT_N1, tn), lambda i, bi: (bi, 0, i)),
                   pl.BlockSpec((1, FFT_N1, tn), lambda i, bi: (bi, 0, i))],
        out_shape=[jax.ShapeDtypeStruct((b, FFT_N1, wide), BF16)] * 2,
        name="fourier_a",
        compiler_params=_cp(("arbitrary", "arbitrary")),
    )(za2, zb2, w1, tc, ts)
    rows = FFT_K1_TILE * n2
    yr = yr.reshape(b, n, c)
    yi = yi.reshape(b, n, c)
    out = pl.pallas_call(
        _fft_b_kernel,
        grid=(b, FFT_N1 // FFT_K1_TILE),
        in_specs=[pl.BlockSpec((1, rows, c), lambda bi, i: (bi, i, 0)),
                  pl.BlockSpec((1, rows, c), lambda bi, i: (bi, i, 0)),
                  pl.BlockSpec(mmat.shape, lambda bi, i: (0, 0)),
                  pl.BlockSpec(bdw.shape, lambda bi, i: (0, 0))],
        out_specs=pl.BlockSpec((1, n2, FFT_K1_TILE, c), lambda bi, i: (bi, 0, i, 0)),
        out_shape=jax.ShapeDtypeStruct((b, n2, FFT_N1, c), F32),
        name="fourier_b",
        compiler_params=_cp(("arbitrary", "arbitrary")),
    )(yr, yi, mmat, bdw)
    return out.reshape(b, n, c)


def _post_kernel(att_ref, four_ref, x_ref, g1_ref, sh2_ref, sc2_ref, gpost_ref, gpre_ref, woa_ref, wof_ref,
                 wrt_ref, br_ref, tri_ref, x1_ref, hx_ref, ridx_ref, gates_ref, cnt_ref, carry_sc):
    first = (pl.program_id(0) == 0) & (pl.program_id(1) == 0)

    @pl.when(first)
    def _():
        carry_sc[...] = jnp.zeros_like(carry_sc)

    mix = _dot(att_ref[0], woa_ref[...]) + _dot(four_ref[0].astype(BF16), wof_ref[...])
    x1 = x_ref[0] + g1_ref[0] * (_rms(mix) * gpost_ref[...])
    x1_ref[0] = x1
    hx = _rms(x1) * gpre_ref[...] * (1.0 + sc2_ref[0]) + sh2_ref[0]
    hb = hx.astype(BF16)
    hx_ref[...] = _pack_bf16_pairs(hb)
    lg = _dot_nt(wrt_ref[...], hb) + br_ref[...]
    iota_e = lax.broadcasted_iota(I32, lg.shape, 0)
    cur = lg
    tops, idxs, ohs = [], [], []
    for _ in range(TOP_K):
        mx = jnp.max(cur, axis=0, keepdims=True)
        idx = jnp.min(jnp.where(cur == mx, iota_e, N_EXPERTS), axis=0, keepdims=True)
        oh = iota_e == idx
        cur = jnp.where(oh, -jnp.inf, cur)
        tops.append(mx)
        idxs.append(idx)
        ohs.append(oh)
    es = [jnp.exp(t - tops[0]) for t in tops]
    den = es[0] + es[1] + es[2] + es[3]
    gates = [e / den for e in es]
    ohsum = ohs[0].astype(F32) + ohs[1].astype(F32) + ohs[2].astype(F32) + ohs[3].astype(F32)
    base = carry_sc[:, 0:1] + _dot(ohsum.astype(BF16), tri_ref[...])
    ranks = [jnp.sum(jnp.where(oh, base, 0.0), axis=0, keepdims=True).astype(I32) for oh in ohs]
    ridx_ref[0] = _stack_rows(idxs + ranks, 2 * TOP_K)
    gates_ref[...] = _stack_rows(gates, LANES).T
    carry_sc[...] = carry_sc[...] + jnp.sum(ohsum, axis=1, keepdims=True)
    cnt_ref[...] = carry_sc[...]


def _post_attention(att, four, x, m3, gpost, gpre, woa, wof, wrt, br, tri, tm):
    b, n, d = x.shape
    nt = n // tm
    t = b * n
    full = lambda a: pl.BlockSpec(a.shape, lambda bi, i: (0,) * a.ndim)
    mrow = lambda j: pl.BlockSpec((1, 1, d), lambda bi, i: (bi * N_MOD + j, 0, 0))
    return pl.pallas_call(
        _post_kernel,
        grid=(b, nt),
        in_specs=[pl.BlockSpec((1, tm, att.shape[2]), lambda bi, i: (bi, i, 0)),
                  pl.BlockSpec((1, tm, F_WIDTH), lambda bi, i: (bi, i, 0)),
                  pl.BlockSpec((1, tm, d), lambda bi, i: (bi, i, 0)),
                  mrow(2), mrow(3), mrow(4),
                  full(gpost), full(gpre), full(woa), full(wof), full(wrt), full(br), full(tri)],
        out_specs=[pl.BlockSpec((1, tm, d), lambda bi, i: (bi, i, 0)),
                   pl.BlockSpec((tm, d // 2), lambda bi, i: (bi * nt + i, 0)),
                   pl.BlockSpec((1, 2 * TOP_K, tm), lambda bi, i: (bi * nt + i, 0, 0)),
                   pl.BlockSpec((tm, LANES), lambda bi, i: (bi * nt + i, 0)),
                   pl.BlockSpec((N_EXPERTS, LANES), lambda bi, i: (0, 0))],
        out_shape=[jax.ShapeDtypeStruct((b, n, d), F32),
                   jax.ShapeDtypeStruct((t, d // 2), U32),
                   jax.ShapeDtypeStruct((t // tm, 2 * TOP_K, tm), I32),
                   jax.ShapeDtypeStruct((t, LANES), F32),
                   jax.ShapeDtypeStruct((N_EXPERTS, LANES), F32)],
        scratch_shapes=[pltpu.VMEM((N_EXPERTS, LANES), F32)],
        name="post_attention",
        compiler_params=_cp(("arbitrary", "arbitrary")),
    )(att, four, x, m3, m3, m3, gpost, gpre, woa, wof, wrt, br, tri)


def _dispatch_kernel(fill_ref, dest_ref, hx_ref, xs_hbm, zrow, sem, fsem, *, tm, bm):
    i = pl.program_id(0)

    @pl.when(i == 0)
    def _():
        zrow[...] = jnp.zeros_like(zrow)
        nblk = xs_hbm.shape[0] // bm

        def tail(blk, c):
            cp = pltpu.make_async_copy(zrow, xs_hbm.at[pl.ds(blk * bm, bm), :], fsem)
            cp.start()
            cp.wait()
            return c

        lax.fori_loop(fill_ref[2 * N_EXPERTS], nblk, tail, 0)

        def per_expert(e, c):
            s = fill_ref[e]
            n = fill_ref[N_EXPERTS + e]
            head = jnp.minimum(n, (-s) & (SUBLANES - 1))
            body = s + head
            nchunk = (n - head) // SUBLANES

            def row_copy(r):
                return pltpu.make_async_copy(zrow.at[pl.ds(0, 1), :], xs_hbm.at[pl.ds(s + r, 1), :], fsem)

            def chunk_copy(j):
                dst = pl.multiple_of(body + j * SUBLANES, SUBLANES)
                return pltpu.make_async_copy(zrow.at[pl.ds(0, SUBLANES), :], xs_hbm.at[pl.ds(dst, SUBLANES), :], fsem)

            def loop(n_it, copy, action):
                def step(r, cc):
                    action(copy(r))
                    return cc

                lax.fori_loop(0, n_it, step, 0)

            loop(head, row_copy, lambda cp: cp.start())
            loop(nchunk, chunk_copy, lambda cp: cp.start())
            loop(head, row_copy, lambda cp: cp.wait())
            loop(nchunk, chunk_copy, lambda cp: cp.wait())
            return c

        lax.fori_loop(0, N_EXPERTS, per_expert, 0)

    def row_copy(g, u, k):
        d = dest_ref[0, 0, k * tm + g * SUBLANES + u]
        return pltpu.make_async_copy(hx_ref.at[g, pl.ds(u, 1), :], xs_hbm.at[pl.ds(d, 1), :], sem)

    def start(g, c):
        for u in range(SUBLANES):
            for k in range(TOP_K):
                row_copy(g, u, k).start(priority=k % 2)
        return c

    def wait(g, c):
        for u in range(SUBLANES):
            for k in range(TOP_K):
                row_copy(g, u, k).wait()
        return c

    lax.fori_loop(0, tm // SUBLANES, start, 0)
    lax.fori_loop(0, tm // SUBLANES, wait, 0)


def _dispatch(fill, dest_tiles, hx, n_slots, tm, bm):
    t, dm = hx.shape
    kern = functools.partial(_dispatch_kernel, tm=tm, bm=bm)
    return pl.pallas_call(
        kern,
        grid_spec=pltpu.PrefetchScalarGridSpec(
            num_scalar_prefetch=1,
            grid=(t // tm,),
            in_specs=[pl.BlockSpec((1, 1, TOP_K * tm), lambda i, f: (i, 0, 0), memory_space=pltpu.SMEM),
                      pl.BlockSpec((tm // SUBLANES, SUBLANES, dm), lambda i, f: (i, 0, 0))],
            out_specs=pl.BlockSpec(memory_space=pl.ANY),
            scratch_shapes=[pltpu.VMEM((bm, dm), hx.dtype),
                            pltpu.SemaphoreType.DMA(()), pltpu.SemaphoreType.DMA(())]),
        out_shape=jax.ShapeDtypeStruct((n_slots, dm), hx.dtype),
        name="dispatch",
        compiler_params=_cp(("arbitrary",)),
    )(fill, dest_tiles, hx.reshape(t // SUBLANES, SUBLANES, dm))


def _moe_kernel(be_ref, nv_ref, x_ref, wg_ref, bg_ref, wu_ref, bu_ref, wd_ref, bd_ref, y_ref, wgb, wub, wdb):
    i = pl.program_id(0)

    @pl.when(i < nv_ref[0])
    def _():
        e = be_ref[i]
        prev = be_ref[jnp.maximum(i - 1, 0)]

        @pl.when((i == 0) | (e != prev))
        def _():
            wgb[...] = wg_ref[0].astype(BF16)
            wub[...] = wu_ref[0].astype(BF16)
            wdb[...] = wd_ref[0].astype(BF16)

        xb = _unpack_bf16_pairs(x_ref[...])
        g = jnp.minimum(_dot(xb, wgb[...]) + bg_ref[0], SWIGLU_LIMIT)
        u = jnp.clip(_dot(xb, wub[...]) + bu_ref[0], -SWIGLU_LIMIT, SWIGLU_LIMIT)
        act = g * (1.0 / (1.0 + jnp.exp(-SWIGLU_ALPHA * g))) * (u + 1.0)
        y_ref[...] = _dot(act.astype(BF16), wdb[...]) + bd_ref[0]

    @pl.when(i >= nv_ref[0])
    def _():
        y_ref[...] = jnp.zeros_like(y_ref)


def _experts(block_e, nvalid, xs, wg, bg, wu, bu, wd, bd, bm):
    n_slots = xs.shape[0]
    e, d, f = wg.shape
    nb = n_slots // bm
    xmap = lambda i, be, nv: (jnp.minimum(i, nv[0] - 1), 0)
    wmap = lambda i, be, nv: (be[i], 0, 0)
    return pl.pallas_call(
        _moe_kernel,
        grid_spec=pltpu.PrefetchScalarGridSpec(
            num_scalar_prefetch=2,
            grid=(nb,),
            in_specs=[pl.BlockSpec((bm, xs.shape[1]), xmap),
                      pl.BlockSpec((1, d, f), wmap), pl.BlockSpec((1, 1, f), wmap),
                      pl.BlockSpec((1, d, f), wmap), pl.BlockSpec((1, 1, f), wmap),
                      pl.BlockSpec((1, f, d), wmap), pl.BlockSpec((1, 1, d), wmap)],
            out_specs=pl.BlockSpec((bm, d), lambda i, be, nv: (i, 0)),
            scratch_shapes=[pltpu.VMEM((d, f), BF16), pltpu.VMEM((d, f), BF16), pltpu.VMEM((f, d), BF16)]),
        out_shape=jax.ShapeDtypeStruct((n_slots, d), F32),
        name="experts",
        compiler_params=_cp(("arbitrary",)),
    )(block_e, nvalid, xs, wg, bg.reshape(e, 1, f), wu, bu.reshape(e, 1, f), wd, bd.reshape(e, 1, d))


def _combine_kernel(dcur_ref, dnxt_ref, ys_hbm, gates_ref, x1_ref, g2_ref, gpost_ref, o_ref, buf, sem,
                    *, tm, nsteps):
    i = pl.program_id(0)
    slot = i % 2

    def row_copy(dref, sl, g, u, k):
        d = dref[0, 0, k * tm + g * SUBLANES + u]
        return pltpu.make_async_copy(ys_hbm.at[pl.ds(d, 1), :], buf.at[sl, k, g, pl.ds(u, 1), :], sem.at[sl])

    def issue(dref, sl):
        def start(g, c):
            for u in range(SUBLANES):
                for k in range(TOP_K):
                    row_copy(dref, sl, g, u, k).start(priority=k % 2)
            return c

        lax.fori_loop(0, tm // SUBLANES, start, 0)

    @pl.when(i == 0)
    def _():
        issue(dcur_ref, 0)

    @pl.when(i + 1 < nsteps)
    def _():
        issue(dnxt_ref, 1 - slot)

    def wait(g, c):
        for u in range(SUBLANES):
            for k in range(TOP_K):
                row_copy(dcur_ref, slot, g, u, k).wait()
        return c

    lax.fori_loop(0, tm // SUBLANES, wait, 0)
    gt = gates_ref[...]
    rows = lambda k: buf[slot, k].reshape(tm, buf.shape[-1])
    y = gt[:, 0:1] * rows(0)
    for k in range(1, TOP_K):
        y = y + gt[:, k:k + 1] * rows(k)
    o_ref[...] = x1_ref[...] + g2_ref[0] * (_rms(y) * gpost_ref[...])


def _combine(dest_tiles, ys, gates_t, x1, m3, gpost, tm):
    b, n, d = x1.shape
    nt = n // tm
    nsteps = b * nt
    kern = functools.partial(_combine_kernel, tm=tm, nsteps=nsteps)
    dspec = lambda f: pl.BlockSpec((1, 1, TOP_K * tm), f, memory_space=pltpu.SMEM)
    out = pl.pallas_call(
        kern,
        grid=(nsteps,),
        in_specs=[dspec(lambda i: (i, 0, 0)),
                  dspec(lambda i: (jnp.minimum(i + 1, nsteps - 1), 0, 0)),
                  pl.BlockSpec(memory_space=pl.ANY),
                  pl.BlockSpec((tm, LANES), lambda i: (i, 0)),
                  pl.BlockSpec((tm, d), lambda i: (i, 0)),
                  pl.BlockSpec((1, 1, d), lambda i: (i // nt * N_MOD + 5, 0, 0)),
                  pl.BlockSpec(gpost.shape, lambda i: (0, 0))],
        out_specs=pl.BlockSpec((tm, d), lambda i: (i, 0)),
        out_shape=jax.ShapeDtypeStruct((b * n, d), F32),
        scratch_shapes=[pltpu.VMEM((2, TOP_K, tm // SUBLANES, SUBLANES, d), F32),
                        pltpu.SemaphoreType.DMA((2,))],
        name="combine",
        compiler_params=_cp(("arbitrary",)),
    )(dest_tiles, dest_tiles, ys, gates_t, x1.reshape(b * n, d), m3, gpost)
    return out.reshape(b, n, d)


def _rope_tables(n):
    f32 = np.float32
    rows = n // GRID_W
    row = np.repeat(np.arange(rows, dtype=f32), GRID_W)
    col = np.tile(np.arange(GRID_W, dtype=f32), rows)
    half = QK_ROPE // 2
    inv = (f32(1.0) / (f32(ROPE_BASE) ** (np.arange(0, half, 2, dtype=f32) / f32(half)))).astype(f32)
    ang = np.concatenate([row[:, None] * inv, col[:, None] * inv], axis=-1).astype(f32)
    cos, sin = np.cos(ang), np.sin(ang)
    zero = np.zeros((n, ROPE_W - QK_ROPE), f32)
    return (jnp.asarray(np.concatenate([cos, cos, zero], axis=1)),
            jnp.asarray(np.concatenate([sin, sin, zero], axis=1)))


def _ctx_tables(n):
    half = QK_ROPE // 2
    t1 = np.zeros((n, ROPE_W), np.float32)
    t1[:, :2 * half] = 1.0
    return jnp.asarray(t1), jnp.zeros((n, ROPE_W), F32)


def _prep_weights(w_in, norm_q, norm_kv, w_q_up, w_kv_up):
    d = w_in.shape[0]
    half = QK_ROPE // 2
    zr = lambda r, c: jnp.zeros((r, c), F32)
    o_rope = F_WIDTH + Q_LORA + KV_LORA
    kr = w_in[:, o_rope:o_rope + QK_ROPE]
    kre, kro = kr[:, 0::2], kr[:, 1::2]
    win = jnp.concatenate([w_in[:, :o_rope],
                           kre, kro, zr(d, ROPE_W - QK_ROPE),
                           -kro, kre, zr(d, ROPE_W - QK_ROPE)], axis=1).astype(BF16)
    hd = QK_NOPE + QK_ROPE
    qa, qb = [], []
    for h in range(N_HEADS):
        wn = w_q_up[:, h * hd:h * hd + QK_NOPE]
        wr = w_q_up[:, h * hd + QK_NOPE:(h + 1) * hd]
        we, wo = wr[:, 0::2], wr[:, 1::2]
        qa += [wn, we, wo, zr(Q_LORA, ROPE_W - QK_ROPE)]
        qb += [-wo, we, zr(Q_LORA, ROPE_W - QK_ROPE)]
    wqa = jnp.concatenate(qa, axis=1).astype(BF16)
    wqb = jnp.concatenate(qb, axis=1).astype(BF16)
    kvd = QK_NOPE + V_DIM
    wk = jnp.concatenate([w_kv_up[:, h * kvd:h * kvd + QK_NOPE] for h in range(N_HEADS)], axis=1).astype(BF16)
    wv = jnp.concatenate([w_kv_up[:, h * kvd + QK_NOPE:(h + 1) * kvd] for h in range(N_HEADS)], axis=1)
    wvt = wv.T.astype(BF16)
    return win, norm_q.reshape(1, -1), norm_kv.reshape(1, -1), wqa, wqb, wk, wvt


def _fourier_consts(n, w_fourier):
    c = F_GDIM
    j = np.arange(c)
    ang = 2.0 * np.pi * np.outer(j, j) / c
    eye = np.eye(F_GROUPS)
    bdc = np.kron(eye, np.cos(ang))
    bds = np.kron(eye, np.sin(ang))
    bdcs = jnp.asarray(np.concatenate([bdc, -bds], axis=1), dtype=BF16)
    n1 = FFT_N1
    n2 = n // n1
    k1 = np.arange(n1)
    a1 = 2.0 * np.pi * np.outer(k1, k1) / n1
    wc, ws = np.cos(a1), np.sin(a1)
    w1 = jnp.asarray(np.block([[wc, ws], [-ws, wc]]), dtype=BF16)
    tw = 2.0 * np.pi * np.outer(k1, np.arange(n2)) / n
    tc = jnp.asarray(np.repeat(np.cos(tw).astype(np.float32), F_WIDTH, axis=1))
    ts = jnp.asarray(np.repeat(np.sin(tw).astype(np.float32), F_WIDTH, axis=1))
    k2 = np.arange(n2)
    a2 = 2.0 * np.pi * np.outer(k2, k2) / n2
    norm = 1.0 / math.sqrt(n * c)
    kt = FFT_K1_TILE
    mc = np.zeros((n2, kt, kt, n2))
    ms = np.zeros((n2, kt, kt, n2))
    for r in range(kt):
        mc[:, r, r, :] = np.cos(a2) * norm
        ms[:, r, r, :] = np.sin(a2) * norm
    mmat = np.concatenate([mc.reshape(n2 * kt, kt * n2), ms.reshape(n2 * kt, kt * n2)], axis=1)
    mmat = jnp.asarray(mmat, dtype=BF16)
    zblk = jnp.zeros((c, c), F32)
    bdw = jnp.concatenate(
        [jnp.concatenate([w_fourier[g] if j == g else zblk for j in range(F_GROUPS)], axis=1)
         for g in range(F_GROUPS)], axis=0)
    return bdcs, w1, tc, ts, mmat, bdw.astype(BF16)


def kernel(x, c, ctx, c_ctx, w_mod, b_mod, norm_attn_pre, norm_attn_post, norm_ffn_pre, norm_ffn_post, w_in, norm_q_lat, norm_kv_lat, w_q_up, w_kv_up, w_fourier, w_out, w_router, b_router, w_gate, b_gate, w_up, b_up, w_down, b_down):
    b, n, d = x.shape
    nctx = ctx.shape[1]
    t = b * n
    assert w_mod.shape[0] == 1 and b + 1 <= SUBLANES and n % (FFT_N1 * SUBLANES) == 0
    row2 = lambda a: a[0].reshape(1, -1)

    c8 = jnp.concatenate([c, c_ctx[None, :], jnp.zeros((SUBLANES - b - 1, d), F32)], axis=0)
    m3 = _modulation(c8, w_mod[0], b_mod[0]).reshape(SUBLANES * N_MOD, 1, d)

    wts = _prep_weights(w_in[0], norm_q_lat[0], norm_kv_lat[0], w_q_up[0], w_kv_up[0])
    bdcs, w1, tc, ts, mmat, bdw = _fourier_consts(n, w_fourier[0])
    pro_w = (wts[0], bdcs) + wts[1:]
    scale = (QK_NOPE + QK_ROPE) ** -0.5 * LOG2E
    g_pre = row2(norm_attn_pre)

    tm = min(512, n)
    t1x, t2x = _rope_tables(n)
    q, k, vt, za, zb = _prologue(x, m3, lambda bi: bi, g_pre, pro_w, t1x, t2x, tm, scale)
    t1c, t2c = _ctx_tables(nctx)
    _, kc, vtc, _, _ = _prologue(ctx, m3, lambda bi: b, g_pre, pro_w, t1c, t2c, nctx, scale)

    att = _attention(q, k, vt, kc, vtc, min(1024, n))
    four = _fourier(za, zb, w1, tc, ts, mmat, bdw)

    tmp = min(512, n)
    tri = jnp.asarray(np.triu(np.ones((tmp, tmp), np.float32), 1), dtype=BF16)
    width = N_HEADS * V_DIM
    woa = w_out[0, :width].astype(BF16)
    wof = w_out[0, width:].astype(BF16)
    x1, hx2, ridx, gates_t, cnt = _post_attention(
        att, four, x, m3, row2(norm_attn_post), row2(norm_ffn_pre), woa, wof,
        w_router[0].T.astype(BF16), b_router[0].reshape(-1, 1), tri, tmp)

    bm = 512
    counts = cnt[:, 0].astype(I32)
    padded = (counts + bm - 1) // bm * bm
    pad_end = jnp.cumsum(padded)
    pad_start = pad_end - padded
    nb = t * TOP_K // bm + N_EXPERTS
    blk_start = jnp.arange(nb, dtype=I32)[:, None] * bm
    block_e = jnp.minimum(jnp.sum((pad_end[None, :] <= blk_start).astype(I32), axis=1), N_EXPERTS - 1)
    nvalid = (pad_end[-1:] // bm).astype(I32)
    eids = jnp.arange(N_EXPERTS, dtype=I32)[:, None, None, None]
    eidx, rank = ridx[:, :TOP_K], ridx[:, TOP_K:]
    dest = jnp.sum(jnp.where(eidx[None] == eids, pad_start[:, None, None, None], 0), axis=0) + rank
    dest_tiles = dest.reshape(t // tmp, 1, TOP_K * tmp)
    fill = jnp.concatenate([pad_start + counts, padded - counts, nvalid]).astype(I32)

    xs = _dispatch(fill, dest_tiles, hx2, nb * bm, tmp, bm)
    ys = _experts(block_e, nvalid, xs, w_gate[0], b_gate[0], w_up[0], b_up[0], w_down[0], b_down[0], bm)
    return _combine(dest_tiles, ys, gates_t, x1, m3, row2(norm_ffn_post), tmp)
```

```python
import functools
import math

import numpy as np
import jax
import jax.numpy as jnp
from jax import lax
from jax.experimental import pallas as pl
from jax.experimental.pallas import tpu as pltpu

F32 = jnp.float32
BF16 = jnp.bfloat16
I32 = jnp.int32
U32 = jnp.uint32

N_HEADS = 6
QK_NOPE = 128
QK_ROPE = 64
V_DIM = 128
Q_LORA = 384
KV_LORA = 256
F_GROUPS = 4
F_GDIM = 64
F_WIDTH = F_GROUPS * F_GDIM
GRID_W = 64
ROPE_BASE = 10000.0
N_EXPERTS = 32
TOP_K = 4
SWIGLU_LIMIT = 7.0
SWIGLU_ALPHA = 1.702
RMS_EPS = 1e-6
N_MOD = 6

LANES = 128
SUBLANES = 8
HEAD_W = 256
ROPE_W = 128
FFT_N1 = 128
FFT_K1_TILE = 8
VMEM_LIMIT = 56 * 1024 * 1024

LOG2E = 1.4426950408889634
NORM_SLACK = 1.0 + 2.0 ** -6
MIN_ROW_SUM = 2.0 ** -60


def _cp(sem, vmem=None):
    return pltpu.CompilerParams(dimension_semantics=sem, vmem_limit_bytes=vmem or VMEM_LIMIT)


def _rms(x):
    return x * lax.rsqrt(jnp.mean(x * x, axis=-1, keepdims=True) + RMS_EPS)


def _dot(a, b):
    return jnp.dot(a, b, preferred_element_type=F32)


def _stack_rows(rows, height):
    w = rows[0].shape[1]
    sub = lax.broadcasted_iota(I32, (height, w), 0)
    out = jnp.zeros((height, w), rows[0].dtype)
    for r, v in enumerate(rows):
        out = jnp.where(sub == r, v, out)
    return out


def _pack_bf16_pairs(xb):
    w = xb.shape[1] // 2
    lo = lax.bitcast_convert_type(xb[:, :w].astype(F32), U32)
    hi = lax.bitcast_convert_type(xb[:, w:].astype(F32), U32)
    return (hi & jnp.uint32(0xFFFF0000)) | (lo >> 16)


def _unpack_bf16_pairs(p):
    lo = lax.bitcast_convert_type(p << 16, F32)
    hi = lax.bitcast_convert_type(p & jnp.uint32(0xFFFF0000), F32)
    return jnp.concatenate([lo, hi], axis=1).astype(BF16)


def _dot_nt(a, b):
    return lax.dot_general(a, b, (((1,), (1,)), ((), ())), preferred_element_type=F32)


def _mod_kernel(c_ref, w_ref, b_ref, o_ref):
    c = c_ref[...]
    a = c / (1.0 + jnp.exp(-c))
    a_hi = a.astype(BF16)
    a_lo = (a - a_hi.astype(F32)).astype(BF16)
    w = w_ref[...]
    w_hi = w.astype(BF16)
    w_lo = (w - w_hi.astype(F32)).astype(BF16)
    acc = _dot(a_hi, w_hi) + _dot(a_lo, w_hi) + _dot(a_hi, w_lo)
    o_ref[...] = acc + b_ref[...]


def _modulation(c8, w_mod, b_mod):
    d, n = w_mod.shape
    tn = 1024
    return pl.pallas_call(
        _mod_kernel,
        grid=(n // tn,),
        in_specs=[pl.BlockSpec((SUBLANES, d), lambda j: (0, 0)),
                  pl.BlockSpec((d, tn), lambda j: (0, j)),
                  pl.BlockSpec((1, tn), lambda j: (0, j))],
        out_specs=pl.BlockSpec((SUBLANES, tn), lambda j: (0, j)),
        out_shape=jax.ShapeDtypeStruct((SUBLANES, n), F32),
        name="modulation",
        compiler_params=_cp(("arbitrary",)),
    )(c8, w_mod, b_mod.reshape(1, n))


def _pro_kernel(x_ref, sh_ref, sc_ref, g_ref, win_ref, bdcs_ref, nq_ref, nkv_ref, wqa_ref, wqb_ref,
                wk_ref, wvt_ref, t1_ref, t2_ref, q_ref, k_ref, vt_ref, za_ref, zb_ref, *, scale):
    x = x_ref[0]
    hx = _rms(x) * g_ref[...] * (1.0 + sc_ref[0]) + sh_ref[0]
    px = _dot(hx.astype(BF16), win_ref[...])
    u = px[:, 0:F_WIDTH].astype(BF16)
    z = _dot(u, bdcs_ref[...])
    za_ref[0] = z[:, :F_WIDTH].astype(BF16)
    zb_ref[0] = z[:, F_WIDTH:].astype(BF16)
    o_q = F_WIDTH
    o_kv = o_q + Q_LORA
    o_ra = o_kv + KV_LORA
    o_rb = o_ra + ROPE_W
    t1 = t1_ref[...]
    t2 = t2_ref[...]
    qn = (_rms(px[:, o_q:o_kv]) * nq_ref[...]).astype(BF16)
    qa = _dot(qn, wqa_ref[...])
    qb = _dot(qn, wqb_ref[...])
    for h in range(N_HEADS):
        nope = qa[:, h * HEAD_W:h * HEAD_W + QK_NOPE] * scale
        rope = (qa[:, h * HEAD_W + QK_NOPE:(h + 1) * HEAD_W] * t1
                + qb[:, h * ROPE_W:(h + 1) * ROPE_W] * t2) * scale
        q_ref[0, h, :, 0:QK_NOPE] = nope.astype(BF16)
        q_ref[0, h, :, QK_NOPE:HEAD_W] = rope.astype(BF16)
    kvn = (_rms(px[:, o_kv:o_ra]) * nkv_ref[...]).astype(BF16)
    kn = _dot(kvn, wk_ref[...])
    kr = (px[:, o_ra:o_rb] * t1 + px[:, o_rb:o_rb + ROPE_W] * t2).astype(BF16)
    for h in range(N_HEADS):
        k_ref[0, h, :, 0:QK_NOPE] = kn[:, h * QK_NOPE:(h + 1) * QK_NOPE].astype(BF16)
        k_ref[0, h, :, QK_NOPE:HEAD_W] = kr
    vt = _dot_nt(wvt_ref[...], kvn)
    for h in range(N_HEADS):
        vt_ref[0, h, 0] = vt[h * V_DIM:(h + 1) * V_DIM, :].astype(BF16)


def _prologue(x, m3, mod_row, g_pre, wts, t1, t2, tm, scale):
    b, n, d = x.shape
    nt = n // tm
    win, bdcs, nq, nkv, wqa, wqb, wk, wvt = wts
    full = lambda a: pl.BlockSpec(a.shape, lambda bi, i: (0,) * a.ndim)
    kern = functools.partial(_pro_kernel, scale=scale)
    return pl.pallas_call(
        kern,
        grid=(b, nt),
        in_specs=[pl.BlockSpec((1, tm, d), lambda bi, i: (bi, i, 0)),
                  pl.BlockSpec((1, 1, d), lambda bi, i: (mod_row(bi) * N_MOD + 0, 0, 0)),
                  pl.BlockSpec((1, 1, d), lambda bi, i: (mod_row(bi) * N_MOD + 1, 0, 0)),
                  full(g_pre), full(win), full(bdcs), full(nq), full(nkv), full(wqa), full(wqb),
                  full(wk), full(wvt),
                  pl.BlockSpec((tm, ROPE_W), lambda bi, i: (i, 0)),
                  pl.BlockSpec((tm, ROPE_W), lambda bi, i: (i, 0))],
        out_specs=[pl.BlockSpec((1, N_HEADS, tm, HEAD_W), lambda bi, i: (bi, 0, i, 0)),
                   pl.BlockSpec((1, N_HEADS, tm, HEAD_W), lambda bi, i: (bi, 0, i, 0)),
                   pl.BlockSpec((1, N_HEADS, 1, V_DIM, tm), lambda bi, i: (bi, 0, i, 0, 0)),
                   pl.BlockSpec((1, tm, F_WIDTH), lambda bi, i: (bi, i, 0)),
                   pl.BlockSpec((1, tm, F_WIDTH), lambda bi, i: (bi, i, 0))],
        out_shape=[jax.ShapeDtypeStruct((b, N_HEADS, n, HEAD_W), BF16),
                   jax.ShapeDtypeStruct((b, N_HEADS, n, HEAD_W), BF16),
                   jax.ShapeDtypeStruct((b, N_HEADS, nt, V_DIM, tm), BF16),
                   jax.ShapeDtypeStruct((b, n, F_WIDTH), BF16),
                   jax.ShapeDtypeStruct((b, n, F_WIDTH), BF16)],
        name="prologue",
        compiler_params=_cp(("arbitrary", "arbitrary")),
    )(x, m3, m3, g_pre, win, bdcs, nq, nkv, wqa, wqb, wk, wvt, t1, t2)


def _attn_kernel(q_ref, k_ref, vt_ref, kc_ref, vtc_ref, o_ref, kmax_sc, m_sc, l_sc, acc_sc,
                 *, nkv, tk, group_size, wide):
    q = q_ref[0, 0]
    ones_kw = jnp.ones((HEAD_W, LANES), BF16)

    def sq_norms(kb):
        kf = kb.astype(F32)
        return jnp.max(_dot((kf * kf).astype(BF16), ones_kw), axis=0, keepdims=True)

    @pl.when(pl.program_id(2) == 0)
    def _():
        def body(j, best):
            start = pl.multiple_of(j * tk, tk)
            return jnp.maximum(best, sq_norms(k_ref[0, 0, pl.ds(start, tk), :]))

        kmax_sc[...] = lax.fori_loop(0, nkv, body, sq_norms(kc_ref[0, 0]))

    qf = q.astype(F32)
    qn2 = _dot_nt(jnp.ones((SUBLANES, HEAD_W), BF16), (qf * qf).astype(BF16))[0:1]
    ref = jnp.sqrt(qn2 * kmax_sc[:, 0:1]) * NORM_SLACK

    def fast_chunk(kb, vtb):
        p = jnp.exp2(_dot_nt(kb, q) - ref)
        l_sc[...] += jnp.sum(p, axis=0, keepdims=True)
        acc_sc[...] += _dot(vtb, p.astype(BF16))

    l_sc[...] = jnp.zeros_like(l_sc)
    acc_sc[...] = jnp.zeros_like(acc_sc)
    fast_chunk(kc_ref[0, 0], vtc_ref[0, 0, 0])

    def fast_body(i, carry):
        for u in range(0, group_size, wide):
            j = group_size * i + u
            start = pl.multiple_of(j * tk, tk)
            vtb = jnp.concatenate([vt_ref[0, 0, j + w] for w in range(wide)], axis=1)
            fast_chunk(k_ref[0, 0, pl.ds(start, wide * tk), :], vtb)
        return carry

    lax.fori_loop(0, nkv // group_size, fast_body, 0)

    @pl.when(jnp.min(l_sc[...]) < MIN_ROW_SUM)
    def _():
        s = _dot_nt(kc_ref[0, 0], q)
        m0 = jnp.max(s, axis=0, keepdims=True)
        p = jnp.exp2(s - m0)
        m_sc[...] = m0
        l_sc[...] = jnp.sum(p, axis=0, keepdims=True)
        acc_sc[...] = _dot(vtc_ref[0, 0, 0], p.astype(BF16))

        def body(j, carry):
            start = pl.multiple_of(j * tk, tk)
            s = _dot_nt(k_ref[0, 0, pl.ds(start, tk), :], q)
            m_prev = m_sc[...]
            m_new = jnp.maximum(m_prev, jnp.max(s, axis=0, keepdims=True))
            alpha = jnp.exp2(m_prev - m_new)
            p = jnp.exp2(s - m_new)
            l_sc[...] = alpha * l_sc[...] + jnp.sum(p, axis=0, keepdims=True)
            acc_sc[...] = alpha * acc_sc[...] + _dot(vt_ref[0, 0, j], p.astype(BF16))
            m_sc[...] = m_new
            return carry

        lax.fori_loop(0, nkv, body, 0)

    o = acc_sc[...] / l_sc[...]
    o_ref[0] = o.T.astype(BF16)


def _attention(q, k, vt, kc, vtc, tq):
    b, h, n, _ = q.shape
    nkv, tk = vt.shape[2], vt.shape[4]
    nctx = kc.shape[2]
    group_size = next(g for g in (16, 8, 4, 2, 1) if nkv % g == 0)
    wide = min(4, group_size)
    kern = functools.partial(_attn_kernel, nkv=nkv, tk=tk, group_size=group_size, wide=wide)
    return pl.pallas_call(
        kern,
        grid=(b, h, n // tq),
        in_specs=[pl.BlockSpec((1, 1, tq, HEAD_W), lambda bi, hi, i: (bi, hi, i, 0)),
                  pl.BlockSpec((1, 1, n, HEAD_W), lambda bi, hi, i: (bi, hi, 0, 0)),
                  pl.BlockSpec((1, 1, nkv, V_DIM, tk), lambda bi, hi, i: (bi, hi, 0, 0, 0)),
                  pl.BlockSpec((1, 1, nctx, HEAD_W), lambda bi, hi, i: (bi, hi, 0, 0)),
                  pl.BlockSpec((1, 1, 1, V_DIM, nctx), lambda bi, hi, i: (bi, hi, 0, 0, 0))],
        out_specs=pl.BlockSpec((1, tq, V_DIM), lambda bi, hi, i: (bi, i, hi)),
        out_shape=jax.ShapeDtypeStruct((b, n, h * V_DIM), BF16),
        scratch_shapes=[pltpu.VMEM((1, LANES), F32),
                        pltpu.VMEM((1, tq), F32), pltpu.VMEM((1, tq), F32),
                        pltpu.VMEM((V_DIM, tq), F32)],
        name="attention",
        compiler_params=_cp(("arbitrary", "arbitrary", "arbitrary")),
    )(q, k, vt, kc, vtc)


def _fft_a_kernel(za_ref, zb_ref, w1_ref, tc_ref, ts_ref, yr_ref, yi_ref):
    ab = jnp.concatenate([za_ref[0], zb_ref[0]], axis=0)
    y = _dot(w1_ref[...], ab)
    re = y[:FFT_N1]
    im = y[FFT_N1:]
    tc = tc_ref[...]
    ts = ts_ref[...]
    yr_ref[0] = (re * tc + im * ts).astype(BF16)
    yi_ref[0] = (im * tc - re * ts).astype(BF16)


def _fft_b_kernel(yr_ref, yi_ref, m_ref, bdw_ref, o_ref):
    ri = jnp.concatenate([yr_ref[0], yi_ref[0]], axis=0)
    f = _dot(m_ref[...], ri)
    four = _dot(f.astype(BF16), bdw_ref[...])
    o_ref[0] = four.reshape(o_ref.shape[1:])


def _fourier(za, zb, w1, tc, ts, mmat, bdw):
    b, n, c = za.shape
    n2 = n // FFT_N1
    wide = n2 * c
    tn = min(2048, wide)
    za2 = za.reshape(b, FFT_N1, wide)
    zb2 = zb.reshape(b, FFT_N1, wide)
    yr, yi = pl.pallas_call(
        _fft_a_kernel,
        grid=(wide // tn, b),
        in_specs=[pl.BlockSpec((1, FFT_N1, tn), lambda i, bi: (bi, 0, i)),
                  pl.BlockSpec((1, FFT_N1, tn), lambda i, bi: (bi, 0, i)),
                  pl.BlockSpec(w1.shape, lambda i, bi: (0, 0)),
                  pl.BlockSpec((FFT_N1, tn), lambda i, bi: (0, i)),
                  pl.BlockSpec((FFT_N1, tn), lambda i, bi: (0, i))],
        out_specs=[pl.BlockSpec((1, FFT_N1, tn), lambda i, bi: (bi, 0, i)),
                   pl.BlockSpec((1, FFT_N1, tn), lambda i, bi: (bi, 0, i))],
        out_shape=[jax.ShapeDtypeStruct((b, FFT_N1, wide), BF16)] * 2,
        name="fourier_a",
        compiler_params=_cp(("arbitrary", "arbitrary")),
    )(za2, zb2, w1, tc, ts)
    rows = FFT_K1_TILE * n2
    yr = yr.reshape(b, n, c)
    yi = yi.reshape(b, n, c)
    out = pl.pallas_call(
        _fft_b_kernel,
        grid=(b, FFT_N1 // FFT_K1_TILE),
        in_specs=[pl.BlockSpec((1, rows, c), lambda bi, i: (bi, i, 0)),
                  pl.BlockSpec((1, rows, c), lambda bi, i: (bi, i, 0)),
                  pl.BlockSpec(mmat.shape, lambda bi, i: (0, 0)),
                  pl.BlockSpec(bdw.shape, lambda bi, i: (0, 0))],
        out_specs=pl.BlockSpec((1, n2, FFT_K1_TILE, c), lambda bi, i: (bi, 0, i, 0)),
        out_shape=jax.ShapeDtypeStruct((b, n2, FFT_N1, c), F32),
        name="fourier_b",
        compiler_params=_cp(("arbitrary", "arbitrary")),
    )(yr, yi, mmat, bdw)
    return out.reshape(b, n, c)


def _post_kernel(att_ref, four_ref, x_ref, g1_ref, sh2_ref, sc2_ref, gpost_ref, gpre_ref, woa_ref, wof_ref,
                 wrt_ref, br_ref, tri_ref, x1_ref, hx_ref, ridx_ref, gates_ref, cnt_ref, carry_sc):
    first = (pl.program_id(0) == 0) & (pl.program_id(1) == 0)

    @pl.when(first)
    def _():
        carry_sc[...] = jnp.zeros_like(carry_sc)

    mix = _dot(att_ref[0], woa_ref[...]) + _dot(four_ref[0].astype(BF16), wof_ref[...])
    x1 = x_ref[0] + g1_ref[0] * (_rms(mix) * gpost_ref[...])
    x1_ref[0] = x1
    hx = _rms(x1) * gpre_ref[...] * (1.0 + sc2_ref[0]) + sh2_ref[0]
    hb = hx.astype(BF16)
    hx_ref[...] = _pack_bf16_pairs(hb)
    lg = _dot_nt(wrt_ref[...], hb) + br_ref[...]
    iota_e = lax.broadcasted_iota(I32, lg.shape, 0)
    cur = lg
    tops, idxs, ohs = [], [], []
    for _ in range(TOP_K):
        mx = jnp.max(cur, axis=0, keepdims=True)
        idx = jnp.min(jnp.where(cur == mx, iota_e, N_EXPERTS), axis=0, keepdims=True)
        oh = iota_e == idx
        cur = jnp.where(oh, -jnp.inf, cur)
        tops.append(mx)
        idxs.append(idx)
        ohs.append(oh)
    es = [jnp.exp(t - tops[0]) for t in tops]
    den = es[0] + es[1] + es[2] + es[3]
    gates = [e / den for e in es]
    ohsum = ohs[0].astype(F32) + ohs[1].astype(F32) + ohs[2].astype(F32) + ohs[3].astype(F32)
    base = carry_sc[:, 0:1] + _dot(ohsum.astype(BF16), tri_ref[...])
    ranks = [jnp.sum(jnp.where(oh, base, 0.0), axis=0, keepdims=True).astype(I32) for oh in ohs]
    ridx_ref[0] = _stack_rows(idxs + ranks, 2 * TOP_K)
    gates_ref[...] = _stack_rows(gates, LANES).T
    carry_sc[...] = carry_sc[...] + jnp.sum(ohsum, axis=1, keepdims=True)
    cnt_ref[...] = carry_sc[...]


def _post_attention(att, four, x, m3, gpost, gpre, woa, wof, wrt, br, tri, tm):
    b, n, d = x.shape
    nt = n // tm
    t = b * n
    full = lambda a: pl.BlockSpec(a.shape, lambda bi, i: (0,) * a.ndim)
    mrow = lambda j: pl.BlockSpec((1, 1, d), lambda bi, i: (bi * N_MOD + j, 0, 0))
    return pl.pallas_call(
        _post_kernel,
        grid=(b, nt),
        in_specs=[pl.BlockSpec((1, tm, att.shape[2]), lambda bi, i: (bi, i, 0)),
                  pl.BlockSpec((1, tm, F_WIDTH), lambda bi, i: (bi, i, 0)),
                  pl.BlockSpec((1, tm, d), lambda bi, i: (bi, i, 0)),
                  mrow(2), mrow(3), mrow(4),
                  full(gpost), full(gpre), full(woa), full(wof), full(wrt), full(br), full(tri)],
        out_specs=[pl.BlockSpec((1, tm, d), lambda bi, i: (bi, i, 0)),
                   pl.BlockSpec((tm, d // 2), lambda bi, i: (bi * nt + i, 0)),
                   pl.BlockSpec((1, 2 * TOP_K, tm), lambda bi, i: (bi * nt + i, 0, 0)),
                   pl.BlockSpec((tm, LANES), lambda bi, i: (bi * nt + i, 0)),
                   pl.BlockSpec((N_EXPERTS, LANES), lambda bi, i: (0, 0))],
        out_shape=[jax.ShapeDtypeStruct((b, n, d), F32),
                   jax.ShapeDtypeStruct((t, d // 2), U32),
                   jax.ShapeDtypeStruct((t // tm, 2 * TOP_K, tm), I32),
                   jax.ShapeDtypeStruct((t, LANES), F32),
                   jax.ShapeDtypeStruct((N_EXPERTS, LANES), F32)],
        scratch_shapes=[pltpu.VMEM((N_EXPERTS, LANES), F32)],
        name="post_attention",
        compiler_params=_cp(("arbitrary", "arbitrary")),
    )(att, four, x, m3, m3, m3, gpost, gpre, woa, wof, wrt, br, tri)


def _dispatch_kernel(fill_ref, dest_ref, hx_ref, xs_hbm, zrow, sem, fsem, *, tm, bm):
    i = pl.program_id(0)

    @pl.when(i == 0)
    def _():
        zrow[...] = jnp.zeros_like(zrow)
        nblk = xs_hbm.shape[0] // bm

        def tail(blk, c):
            cp = pltpu.make_async_copy(zrow, xs_hbm.at[pl.ds(blk * bm, bm), :], fsem)
            cp.start()
            cp.wait()
            return c

        lax.fori_loop(fill_ref[2 * N_EXPERTS], nblk, tail, 0)

        def per_expert(e, c):
            s = fill_ref[e]
            n = fill_ref[N_EXPERTS + e]
            head = jnp.minimum(n, (-s) & (SUBLANES - 1))
            body = s + head
            nchunk = (n - head) // SUBLANES

            def row_copy(r):
                return pltpu.make_async_copy(zrow.at[pl.ds(0, 1), :], xs_hbm.at[pl.ds(s + r, 1), :], fsem)

            def chunk_copy(j):
                dst = pl.multiple_of(body + j * SUBLANES, SUBLANES)
                return pltpu.make_async_copy(zrow.at[pl.ds(0, SUBLANES), :], xs_hbm.at[pl.ds(dst, SUBLANES), :], fsem)

            def loop(n_it, copy, action):
                def step(r, cc):
                    action(copy(r))
                    return cc

                lax.fori_loop(0, n_it, step, 0)

            loop(head, row_copy, lambda cp: cp.start())
            loop(nchunk, chunk_copy, lambda cp: cp.start())
            loop(head, row_copy, lambda cp: cp.wait())
            loop(nchunk, chunk_copy, lambda cp: cp.wait())
            return c

        lax.fori_loop(0, N_EXPERTS, per_expert, 0)

    def row_copy(g, u, k):
        d = dest_ref[0, 0, k * tm + g * SUBLANES + u]
        return pltpu.make_async_copy(hx_ref.at[g, pl.ds(u, 1), :], xs_hbm.at[pl.ds(d, 1), :], sem)

    def start(g, c):
        for u in range(SUBLANES):
            for k in range(TOP_K):
                row_copy(g, u, k).start(priority=k % 2)
        return c

    def wait(g, c):
        for u in range(SUBLANES):
            for k in range(TOP_K):
                row_copy(g, u, k).wait()
        return c

    lax.fori_loop(0, tm // SUBLANES, start, 0)
    lax.fori_loop(0, tm // SUBLANES, wait, 0)


def _dispatch(fill, dest_tiles, hx, n_slots, tm, bm):
    t, dm = hx.shape
    kern = functools.partial(_dispatch_kernel, tm=tm, bm=bm)
    return pl.pallas_call(
        kern,
        grid_spec=pltpu.PrefetchScalarGridSpec(
            num_scalar_prefetch=1,
            grid=(t // tm,),
            in_specs=[pl.BlockSpec((1, 1, TOP_K * tm), lambda i, f: (i, 0, 0), memory_space=pltpu.SMEM),
                      pl.BlockSpec((tm // SUBLANES, SUBLANES, dm), lambda i, f: (i, 0, 0))],
            out_specs=pl.BlockSpec(memory_space=pl.ANY),
            scratch_shapes=[pltpu.VMEM((bm, dm), hx.dtype),
                            pltpu.SemaphoreType.DMA(()), pltpu.SemaphoreType.DMA(())]),
        out_shape=jax.ShapeDtypeStruct((n_slots, dm), hx.dtype),
        name="dispatch",
        compiler_params=_cp(("arbitrary",)),
    )(fill, dest_tiles, hx.reshape(t // SUBLANES, SUBLANES, dm))


def _moe_kernel(be_ref, nv_ref, bv_ref, x_ref, wg_ref, bg_ref, wu_ref, bu_ref, wd_ref, bd_ref, y_ref,
                wgb, wub, wdb, *, row_step):
    i = pl.program_id(0)
    bm = x_ref.shape[0]

    @pl.when(i < nv_ref[0])
    def _():
        e = be_ref[i]
        prev = be_ref[jnp.maximum(i - 1, 0)]

        @pl.when((i == 0) | (e != prev))
        def _():
            wgb[...] = wg_ref[0].astype(BF16)
            wub[...] = wu_ref[0].astype(BF16)
            wdb[...] = wd_ref[0].astype(BF16)

        def compute(rows):
            xb = _unpack_bf16_pairs(x_ref[0:rows, :])
            g = jnp.minimum(_dot(xb, wgb[...]) + bg_ref[0], SWIGLU_LIMIT)
            u = jnp.clip(_dot(xb, wub[...]) + bu_ref[0], -SWIGLU_LIMIT, SWIGLU_LIMIT)
            act = g * (1.0 / (1.0 + jnp.exp(-SWIGLU_ALPHA * g))) * (u + 1.0)
            y_ref[0:rows, :] = _dot(act.astype(BF16), wdb[...]) + bd_ref[0]
            if rows < bm:
                y_ref[rows:bm, :] = jnp.zeros((bm - rows, y_ref.shape[1]), y_ref.dtype)

        valid = bv_ref[i]
        for rows in range(row_step, bm + 1, row_step):
            lo = rows - row_step if rows > row_step else -1
            pl.when((valid > lo) & (valid <= rows))(functools.partial(compute, rows))

    @pl.when(i >= nv_ref[0])
    def _():
        y_ref[...] = jnp.zeros_like(y_ref)


def _experts(block_e, nvalid, block_valid, xs, wg, bg, wu, bu, wd, bd, bm):
    n_slots = xs.shape[0]
    e, d, f = wg.shape
    nb = n_slots // bm
    xmap = lambda i, be, nv, bv: (jnp.minimum(i, nv[0] - 1), 0)
    wmap = lambda i, be, nv, bv: (be[i], 0, 0)
    kern = functools.partial(_moe_kernel, row_step=bm // 4)
    return pl.pallas_call(
        kern,
        grid_spec=pltpu.PrefetchScalarGridSpec(
            num_scalar_prefetch=3,
            grid=(nb,),
            in_specs=[pl.BlockSpec((bm, xs.shape[1]), xmap),
                      pl.BlockSpec((1, d, f), wmap), pl.BlockSpec((1, 1, f), wmap),
                      pl.BlockSpec((1, d, f), wmap), pl.BlockSpec((1, 1, f), wmap),
                      pl.BlockSpec((1, f, d), wmap), pl.BlockSpec((1, 1, d), wmap)],
            out_specs=pl.BlockSpec((bm, d), lambda i, be, nv, bv: (i, 0)),
            scratch_shapes=[pltpu.VMEM((d, f), BF16), pltpu.VMEM((d, f), BF16), pltpu.VMEM((f, d), BF16)]),
        out_shape=jax.ShapeDtypeStruct((n_slots, d), F32),
        name="experts",
        compiler_params=_cp(("arbitrary",)),
    )(block_e, nvalid, block_valid, xs, wg, bg.reshape(e, 1, f), wu, bu.reshape(e, 1, f), wd, bd.reshape(e, 1, d))


def _combine_kernel(dcur_ref, dnxt_ref, ys_hbm, gates_ref, x1_ref, g2_ref, gpost_ref, o_ref, buf, sem,
                    *, tm, nsteps):
    i = pl.program_id(0)
    slot = i % 2

    def row_copy(dref, sl, g, u, k):
        d = dref[0, 0, k * tm + g * SUBLANES + u]
        return pltpu.make_async_copy(ys_hbm.at[pl.ds(d, 1), :], buf.at[sl, k, g, pl.ds(u, 1), :], sem.at[sl])

    def issue(dref, sl):
        def start(g, c):
            for u in range(SUBLANES):
                for k in range(TOP_K):
                    row_copy(dref, sl, g, u, k).start(priority=k % 2)
            return c

        lax.fori_loop(0, tm // SUBLANES, start, 0)

    @pl.when(i == 0)
    def _():
        issue(dcur_ref, 0)

    @pl.when(i + 1 < nsteps)
    def _():
        issue(dnxt_ref, 1 - slot)

    def wait(g, c):
        for u in range(SUBLANES):
            for k in range(TOP_K):
                row_copy(dcur_ref, slot, g, u, k).wait()
        return c

    lax.fori_loop(0, tm // SUBLANES, wait, 0)
    gt = gates_ref[...]
    rows = lambda k: buf[slot, k].reshape(tm, buf.shape[-1])
    y = gt[:, 0:1] * rows(0)
    for k in range(1, TOP_K):
        y = y + gt[:, k:k + 1] * rows(k)
    o_ref[...] = x1_ref[...] + g2_ref[0] * (_rms(y) * gpost_ref[...])


def _combine(dest_tiles, ys, gates_t, x1, m3, gpost, tm):
    b, n, d = x1.shape
    nt = n // tm
    nsteps = b * nt
    kern = functools.partial(_combine_kernel, tm=tm, nsteps=nsteps)
    dspec = lambda f: pl.BlockSpec((1, 1, TOP_K * tm), f, memory_space=pltpu.SMEM)
    out = pl.pallas_call(
        kern,
        grid=(nsteps,),
        in_specs=[dspec(lambda i: (i, 0, 0)),
                  dspec(lambda i: (jnp.minimum(i + 1, nsteps - 1), 0, 0)),
                  pl.BlockSpec(memory_space=pl.ANY),
                  pl.BlockSpec((tm, LANES), lambda i: (i, 0)),
                  pl.BlockSpec((tm, d), lambda i: (i, 0)),
                  pl.BlockSpec((1, 1, d), lambda i: (i // nt * N_MOD + 5, 0, 0)),
                  pl.BlockSpec(gpost.shape, lambda i: (0, 0))],
        out_specs=pl.BlockSpec((tm, d), lambda i: (i, 0)),
        out_shape=jax.ShapeDtypeStruct((b * n, d), F32),
        scratch_shapes=[pltpu.VMEM((2, TOP_K, tm // SUBLANES, SUBLANES, d), F32),
                        pltpu.SemaphoreType.DMA((2,))],
        name="combine",
        compiler_params=_cp(("arbitrary",)),
    )(dest_tiles, dest_tiles, ys, gates_t, x1.reshape(b * n, d), m3, gpost)
    return out.reshape(b, n, d)


def _rope_tables(n):
    f32 = np.float32
    rows = n // GRID_W
    row = np.repeat(np.arange(rows, dtype=f32), GRID_W)
    col = np.tile(np.arange(GRID_W, dtype=f32), rows)
    half = QK_ROPE // 2
    inv = (f32(1.0) / (f32(ROPE_BASE) ** (np.arange(0, half, 2, dtype=f32) / f32(half)))).astype(f32)
    ang = np.concatenate([row[:, None] * inv, col[:, None] * inv], axis=-1).astype(f32)
    cos, sin = np.cos(ang), np.sin(ang)
    zero = np.zeros((n, ROPE_W - QK_ROPE), f32)
    return (jnp.asarray(np.concatenate([cos, cos, zero], axis=1)),
            jnp.asarray(np.concatenate([sin, sin, zero], axis=1)))


def _ctx_tables(n):
    half = QK_ROPE // 2
    t1 = np.zeros((n, ROPE_W), np.float32)
    t1[:, :2 * half] = 1.0
    return jnp.asarray(t1), jnp.zeros((n, ROPE_W), F32)


def _prep_weights(w_in, norm_q, norm_kv, w_q_up, w_kv_up):
    d = w_in.shape[0]
    half = QK_ROPE // 2
    zr = lambda r, c: jnp.zeros((r, c), F32)
    o_rope = F_WIDTH + Q_LORA + KV_LORA
    kr = w_in[:, o_rope:o_rope + QK_ROPE]
    kre, kro = kr[:, 0::2], kr[:, 1::2]
    win = jnp.concatenate([w_in[:, :o_rope],
                           kre, kro, zr(d, ROPE_W - QK_ROPE),
                           -kro, kre, zr(d, ROPE_W - QK_ROPE)], axis=1).astype(BF16)
    hd = QK_NOPE + QK_ROPE
    qa, qb = [], []
    for h in range(N_HEADS):
        wn = w_q_up[:, h * hd:h * hd + QK_NOPE]
        wr = w_q_up[:, h * hd + QK_NOPE:(h + 1) * hd]
        we, wo = wr[:, 0::2], wr[:, 1::2]
        qa += [wn, we, wo, zr(Q_LORA, ROPE_W - QK_ROPE)]
        qb += [-wo, we, zr(Q_LORA, ROPE_W - QK_ROPE)]
    wqa = jnp.concatenate(qa, axis=1).astype(BF16)
    wqb = jnp.concatenate(qb, axis=1).astype(BF16)
    kvd = QK_NOPE + V_DIM
    wk = jnp.concatenate([w_kv_up[:, h * kvd:h * kvd + QK_NOPE] for h in range(N_HEADS)], axis=1).astype(BF16)
    wv = jnp.concatenate([w_kv_up[:, h * kvd + QK_NOPE:(h + 1) * kvd] for h in range(N_HEADS)], axis=1)
    wvt = wv.T.astype(BF16)
    return win, norm_q.reshape(1, -1), norm_kv.reshape(1, -1), wqa, wqb, wk, wvt


def _fourier_consts(n, w_fourier):
    c = F_GDIM
    j = np.arange(c)
    ang = 2.0 * np.pi * np.outer(j, j) / c
    eye = np.eye(F_GROUPS)
    bdc = np.kron(eye, np.cos(ang))
    bds = np.kron(eye, np.sin(ang))
    bdcs = jnp.asarray(np.concatenate([bdc, -bds], axis=1), dtype=BF16)
    n1 = FFT_N1
    n2 = n // n1
    k1 = np.arange(n1)
    a1 = 2.0 * np.pi * np.outer(k1, k1) / n1
    wc, ws = np.cos(a1), np.sin(a1)
    w1 = jnp.asarray(np.block([[wc, ws], [-ws, wc]]), dtype=BF16)
    tw = 2.0 * np.pi * np.outer(k1, np.arange(n2)) / n
    tc = jnp.asarray(np.repeat(np.cos(tw).astype(np.float32), F_WIDTH, axis=1))
    ts = jnp.asarray(np.repeat(np.sin(tw).astype(np.float32), F_WIDTH, axis=1))
    k2 = np.arange(n2)
    a2 = 2.0 * np.pi * np.outer(k2, k2) / n2
    norm = 1.0 / math.sqrt(n * c)
    kt = FFT_K1_TILE
    mc = np.zeros((n2, kt, kt, n2))
    ms = np.zeros((n2, kt, kt, n2))
    for r in range(kt):
        mc[:, r, r, :] = np.cos(a2) * norm
        ms[:, r, r, :] = np.sin(a2) * norm
    mmat = np.concatenate([mc.reshape(n2 * kt, kt * n2), ms.reshape(n2 * kt, kt * n2)], axis=1)
    mmat = jnp.asarray(mmat, dtype=BF16)
    zblk = jnp.zeros((c, c), F32)
    bdw = jnp.concatenate(
        [jnp.concatenate([w_fourier[g] if j == g else zblk for j in range(F_GROUPS)], axis=1)
         for g in range(F_GROUPS)], axis=0)
    return bdcs, w1, tc, ts, mmat, bdw.astype(BF16)


def kernel(x, c, ctx, c_ctx, w_mod, b_mod, norm_attn_pre, norm_attn_post, norm_ffn_pre, norm_ffn_post, w_in, norm_q_lat, norm_kv_lat, w_q_up, w_kv_up, w_fourier, w_out, w_router, b_router, w_gate, b_gate, w_up, b_up, w_down, b_down):
    b, n, d = x.shape
    nctx = ctx.shape[1]
    t = b * n
    assert w_mod.shape[0] == 1 and b + 1 <= SUBLANES and n % (FFT_N1 * SUBLANES) == 0
    row2 = lambda a: a[0].reshape(1, -1)

    c8 = jnp.concatenate([c, c_ctx[None, :], jnp.zeros((SUBLANES - b - 1, d), F32)], axis=0)
    m3 = _modulation(c8, w_mod[0], b_mod[0]).reshape(SUBLANES * N_MOD, 1, d)

    wts = _prep_weights(w_in[0], norm_q_lat[0], norm_kv_lat[0], w_q_up[0], w_kv_up[0])
    bdcs, w1, tc, ts, mmat, bdw = _fourier_consts(n, w_fourier[0])
    pro_w = (wts[0], bdcs) + wts[1:]
    scale = (QK_NOPE + QK_ROPE) ** -0.5 * LOG2E
    g_pre = row2(norm_attn_pre)

    tm = min(512, n)
    t1x, t2x = _rope_tables(n)
    q, k, vt, za, zb = _prologue(x, m3, lambda bi: bi, g_pre, pro_w, t1x, t2x, tm, scale)
    t1c, t2c = _ctx_tables(nctx)
    _, kc, vtc, _, _ = _prologue(ctx, m3, lambda bi: b, g_pre, pro_w, t1c, t2c, nctx, scale)

    att = _attention(q, k, vt, kc, vtc, min(1024, n))
    four = _fourier(za, zb, w1, tc, ts, mmat, bdw)

    tmp = min(512, n)
    tri = jnp.asarray(np.triu(np.ones((tmp, tmp), np.float32), 1), dtype=BF16)
    width = N_HEADS * V_DIM
    woa = w_out[0, :width].astype(BF16)
    wof = w_out[0, width:].astype(BF16)
    x1, hx2, ridx, gates_t, cnt = _post_attention(
        att, four, x, m3, row2(norm_attn_post), row2(norm_ffn_pre), woa, wof,
        w_router[0].T.astype(BF16), b_router[0].reshape(-1, 1), tri, tmp)

    bm = 512
    counts = cnt[:, 0].astype(I32)
    padded = (counts + bm - 1) // bm * bm
    pad_end = jnp.cumsum(padded)
    pad_start = pad_end - padded
    nb = t * TOP_K // bm + N_EXPERTS
    blk_start = jnp.arange(nb, dtype=I32)[:, None] * bm
    block_e = jnp.minimum(jnp.sum((pad_end[None, :] <= blk_start).astype(I32), axis=1), N_EXPERTS - 1)
    nvalid = (pad_end[-1:] // bm).astype(I32)
    of_block = jnp.arange(N_EXPERTS, dtype=I32)[None, :] == block_e[:, None]
    used_end = jnp.sum(jnp.where(of_block, (pad_start + counts)[None, :], 0), axis=1)
    block_valid = jnp.clip(used_end - blk_start[:, 0], 0, bm).astype(I32)
    eids = jnp.arange(N_EXPERTS, dtype=I32)[:, None, None, None]
    eidx, rank = ridx[:, :TOP_K], ridx[:, TOP_K:]
    dest = jnp.sum(jnp.where(eidx[None] == eids, pad_start[:, None, None, None], 0), axis=0) + rank
    dest_tiles = dest.reshape(t // tmp, 1, TOP_K * tmp)
    fill = jnp.concatenate([pad_start + counts, padded - counts, nvalid]).astype(I32)

    xs = _dispatch(fill, dest_tiles, hx2, nb * bm, tmp, bm)
    ys = _experts(block_e, nvalid, block_valid, xs, w_gate[0], b_gate[0], w_up[0], b_up[0], w_down[0], b_down[0], bm)
    return _combine(dest_tiles, ys, gates_t, x1, m3, row2(norm_ffn_post), tmp)
```

```python
import functools
import math

import numpy as np
import jax
import jax.numpy as jnp
from jax import lax
from jax.experimental import pallas as pl
from jax.experimental.pallas import tpu as pltpu

F32 = jnp.float32
BF16 = jnp.bfloat16
I32 = jnp.int32
U32 = jnp.uint32

N_HEADS = 6
QK_NOPE = 128
QK_ROPE = 64
V_DIM = 128
Q_LORA = 384
KV_LORA = 256
F_GROUPS = 4
F_GDIM = 64
F_WIDTH = F_GROUPS * F_GDIM
GRID_W = 64
ROPE_BASE = 10000.0
N_EXPERTS = 32
TOP_K = 4
SWIGLU_LIMIT = 7.0
SWIGLU_ALPHA = 1.702
RMS_EPS = 1e-6
N_MOD = 6

LANES = 128
SUBLANES = 8
HEAD_W = 256
ROPE_W = 128
FFT_N1 = 128
FFT_K1_TILE = 8
VMEM_LIMIT = 56 * 1024 * 1024

LOG2E = 1.4426950408889634
NORM_SLACK = 1.0 + 2.0 ** -6
MIN_ROW_SUM = 2.0 ** -60


def _cp(sem, vmem=None):
    return pltpu.CompilerParams(dimension_semantics=sem, vmem_limit_bytes=vmem or VMEM_LIMIT)


def _rms(x):
    return x * lax.rsqrt(jnp.mean(x * x, axis=-1, keepdims=True) + RMS_EPS)


def _dot(a, b):
    return jnp.dot(a, b, preferred_element_type=F32)


def _stack_rows(rows, height):
    w = rows[0].shape[1]
    sub = lax.broadcasted_iota(I32, (height, w), 0)
    out = jnp.zeros((height, w), rows[0].dtype)
    for r, v in enumerate(rows):
        out = jnp.where(sub == r, v, out)
    return out


def _pack_bf16_pairs(xb):
    w = xb.shape[1] // 2
    lo = lax.bitcast_convert_type(xb[:, :w].astype(F32), U32)
    hi = lax.bitcast_convert_type(xb[:, w:].astype(F32), U32)
    return (hi & jnp.uint32(0xFFFF0000)) | (lo >> 16)


def _unpack_bf16_pairs(p):
    lo = lax.bitcast_convert_type(p << 16, F32)
    hi = lax.bitcast_convert_type(p & jnp.uint32(0xFFFF0000), F32)
    return jnp.concatenate([lo, hi], axis=1).astype(BF16)


def _dot_nt(a, b):
    return lax.dot_general(a, b, (((1,), (1,)), ((), ())), preferred_element_type=F32)


def _mod_kernel(c_ref, w_ref, b_ref, o_ref):
    c = c_ref[...]
    a = c / (1.0 + jnp.exp(-c))
    a_hi = a.astype(BF16)
    a_lo = (a - a_hi.astype(F32)).astype(BF16)
    w = w_ref[...]
    w_hi = w.astype(BF16)
    w_lo = (w - w_hi.astype(F32)).astype(BF16)
    acc = _dot(a_hi, w_hi) + _dot(a_lo, w_hi) + _dot(a_hi, w_lo)
    o_ref[...] = acc + b_ref[...]


def _modulation(c8, w_mod, b_mod):
    d, n = w_mod.shape
    tn = 1024
    return pl.pallas_call(
        _mod_kernel,
        grid=(n // tn,),
        in_specs=[pl.BlockSpec((SUBLANES, d), lambda j: (0, 0)),
                  pl.BlockSpec((d, tn), lambda j: (0, j)),
                  pl.BlockSpec((1, tn), lambda j: (0, j))],
        out_specs=pl.BlockSpec((SUBLANES, tn), lambda j: (0, j)),
        out_shape=jax.ShapeDtypeStruct((SUBLANES, n), F32),
        name="modulation",
        compiler_params=_cp(("arbitrary",)),
    )(c8, w_mod, b_mod.reshape(1, n))


def _pro_kernel(x_ref, sh_ref, sc_ref, g_ref, win_ref, bdcs_ref, nq_ref, nkv_ref, wqa_ref, wqb_ref,
                wk_ref, wvt_ref, t1_ref, t2_ref, q_ref, k_ref, vt_ref, za_ref, zb_ref, *, scale):
    x = x_ref[0]
    hx = _rms(x) * g_ref[...] * (1.0 + sc_ref[0]) + sh_ref[0]
    px = _dot(hx.astype(BF16), win_ref[...])
    u = px[:, 0:F_WIDTH].astype(BF16)
    z = _dot(u, bdcs_ref[...])
    za_ref[0] = z[:, :F_WIDTH].astype(BF16)
    zb_ref[0] = z[:, F_WIDTH:].astype(BF16)
    o_q = F_WIDTH
    o_kv = o_q + Q_LORA
    o_ra = o_kv + KV_LORA
    o_rb = o_ra + ROPE_W
    t1 = t1_ref[...]
    t2 = t2_ref[...]
    qn = (_rms(px[:, o_q:o_kv]) * nq_ref[...]).astype(BF16)
    qa = _dot(qn, wqa_ref[...])
    qb = _dot(qn, wqb_ref[...])
    for h in range(N_HEADS):
        nope = qa[:, h * HEAD_W:h * HEAD_W + QK_NOPE] * scale
        rope = (qa[:, h * HEAD_W + QK_NOPE:(h + 1) * HEAD_W] * t1
                + qb[:, h * ROPE_W:(h + 1) * ROPE_W] * t2) * scale
        q_ref[0, h, :, 0:QK_NOPE] = nope.astype(BF16)
        q_ref[0, h, :, QK_NOPE:HEAD_W] = rope.astype(BF16)
    kvn = (_rms(px[:, o_kv:o_ra]) * nkv_ref[...]).astype(BF16)
    kn = _dot(kvn, wk_ref[...])
    kr = (px[:, o_ra:o_rb] * t1 + px[:, o_rb:o_rb + ROPE_W] * t2).astype(BF16)
    for h in range(N_HEADS):
        k_ref[0, h, :, 0:QK_NOPE] = kn[:, h * QK_NOPE:(h + 1) * QK_NOPE].astype(BF16)
        k_ref[0, h, :, QK_NOPE:HEAD_W] = kr
    vt = _dot_nt(wvt_ref[...], kvn)
    for h in range(N_HEADS):
        vt_ref[0, h, 0] = vt[h * V_DIM:(h + 1) * V_DIM, :].astype(BF16)


def _prologue(x, m3, mod_row, g_pre, wts, t1, t2, tm, scale):
    b, n, d = x.shape
    nt = n // tm
    win, bdcs, nq, nkv, wqa, wqb, wk, wvt = wts
    full = lambda a: pl.BlockSpec(a.shape, lambda bi, i: (0,) * a.ndim)
    kern = functools.partial(_pro_kernel, scale=scale)
    return pl.pallas_call(
        kern,
        grid=(b, nt),
        in_specs=[pl.BlockSpec((1, tm, d), lambda bi, i: (bi, i, 0)),
                  pl.BlockSpec((1, 1, d), lambda bi, i: (mod_row(bi) * N_MOD + 0, 0, 0)),
                  pl.BlockSpec((1, 1, d), lambda bi, i: (mod_row(bi) * N_MOD + 1, 0, 0)),
                  full(g_pre), full(win), full(bdcs), full(nq), full(nkv), full(wqa), full(wqb),
                  full(wk), full(wvt),
                  pl.BlockSpec((tm, ROPE_W), lambda bi, i: (i, 0)),
                  pl.BlockSpec((tm, ROPE_W), lambda bi, i: (i, 0))],
        out_specs=[pl.BlockSpec((1, N_HEADS, tm, HEAD_W), lambda bi, i: (bi, 0, i, 0)),
                   pl.BlockSpec((1, N_HEADS, tm, HEAD_W), lambda bi, i: (bi, 0, i, 0)),
                   pl.BlockSpec((1, N_HEADS, 1, V_DIM, tm), lambda bi, i: (bi, 0, i, 0, 0)),
                   pl.BlockSpec((1, tm, F_WIDTH), lambda bi, i: (bi, i, 0)),
                   pl.BlockSpec((1, tm, F_WIDTH), lambda bi, i: (bi, i, 0))],
        out_shape=[jax.ShapeDtypeStruct((b, N_HEADS, n, HEAD_W), BF16),
                   jax.ShapeDtypeStruct((b, N_HEADS, n, HEAD_W), BF16),
                   jax.ShapeDtypeStruct((b, N_HEADS, nt, V_DIM, tm), BF16),
                   jax.ShapeDtypeStruct((b, n, F_WIDTH), BF16),
                   jax.ShapeDtypeStruct((b, n, F_WIDTH), BF16)],
        name="prologue",
        compiler_params=_cp(("arbitrary", "arbitrary")),
    )(x, m3, m3, g_pre, win, bdcs, nq, nkv, wqa, wqb, wk, wvt, t1, t2)


def _attn_kernel(q_ref, k_ref, vt_ref, kc_ref, vtc_ref, o_ref, kmax_sc, m_sc, l_sc, acc_sc,
                 *, nkv, tk, group_size, wide):
    q = q_ref[0, 0]
    ones_kw = jnp.ones((HEAD_W, LANES), BF16)

    def sq_norms(kb):
        kf = kb.astype(F32)
        return jnp.max(_dot((kf * kf).astype(BF16), ones_kw), axis=0, keepdims=True)

    @pl.when(pl.program_id(2) == 0)
    def _():
        def body(j, best):
            start = pl.multiple_of(j * tk, tk)
            return jnp.maximum(best, sq_norms(k_ref[0, 0, pl.ds(start, tk), :]))

        kmax_sc[...] = lax.fori_loop(0, nkv, body, sq_norms(kc_ref[0, 0]))

    qf = q.astype(F32)
    qn2 = _dot_nt(jnp.ones((SUBLANES, HEAD_W), BF16), (qf * qf).astype(BF16))[0:1]
    ref = jnp.sqrt(qn2 * kmax_sc[:, 0:1]) * NORM_SLACK

    def fast_chunk(kb, vtb):
        p = jnp.exp2(_dot_nt(kb, q) - ref)
        l_sc[...] += jnp.sum(p, axis=0, keepdims=True)
        acc_sc[...] += _dot(vtb, p.astype(BF16))

    l_sc[...] = jnp.zeros_like(l_sc)
    acc_sc[...] = jnp.zeros_like(acc_sc)
    fast_chunk(kc_ref[0, 0], vtc_ref[0, 0, 0])

    def fast_body(i, carry):
        for u in range(0, group_size, wide):
            j = group_size * i + u
            start = pl.multiple_of(j * tk, tk)
            vtb = jnp.concatenate([vt_ref[0, 0, j + w] for w in range(wide)], axis=1)
            fast_chunk(k_ref[0, 0, pl.ds(start, wide * tk), :], vtb)
        return carry

    lax.fori_loop(0, nkv // group_size, fast_body, 0)

    @pl.when(jnp.min(l_sc[...]) < MIN_ROW_SUM)
    def _():
        s = _dot_nt(kc_ref[0, 0], q)
        m0 = jnp.max(s, axis=0, keepdims=True)
        p = jnp.exp2(s - m0)
        m_sc[...] = m0
        l_sc[...] = jnp.sum(p, axis=0, keepdims=True)
        acc_sc[...] = _dot(vtc_ref[0, 0, 0], p.astype(BF16))

        def body(j, carry):
            start = pl.multiple_of(j * tk, tk)
            s = _dot_nt(k_ref[0, 0, pl.ds(start, tk), :], q)
            m_prev = m_sc[...]
            m_new = jnp.maximum(m_prev, jnp.max(s, axis=0, keepdims=True))
            alpha = jnp.exp2(m_prev - m_new)
            p = jnp.exp2(s - m_new)
            l_sc[...] = alpha * l_sc[...] + jnp.sum(p, axis=0, keepdims=True)
            acc_sc[...] = alpha * acc_sc[...] + _dot(vt_ref[0, 0, j], p.astype(BF16))
            m_sc[...] = m_new
            return carry

        lax.fori_loop(0, nkv, body, 0)

    o = acc_sc[...] / l_sc[...]
    o_ref[0] = o.T.astype(BF16)


def _attention(q, k, vt, kc, vtc, tq):
    b, h, n, _ = q.shape
    nkv, tk = vt.shape[2], vt.shape[4]
    nctx = kc.shape[2]
    group_size = next(g for g in (16, 8, 4, 2, 1) if nkv % g == 0)
    wide = min(4, group_size)
    kern = functools.partial(_attn_kernel, nkv=nkv, tk=tk, group_size=group_size, wide=wide)
    return pl.pallas_call(
        kern,
        grid=(b, h, n // tq),
        in_specs=[pl.BlockSpec((1, 1, tq, HEAD_W), lambda bi, hi, i: (bi, hi, i, 0)),
                  pl.BlockSpec((1, 1, n, HEAD_W), lambda bi, hi, i: (bi, hi, 0, 0)),
                  pl.BlockSpec((1, 1, nkv, V_DIM, tk), lambda bi, hi, i: (bi, hi, 0, 0, 0)),
                  pl.BlockSpec((1, 1, nctx, HEAD_W), lambda bi, hi, i: (bi, hi, 0, 0)),
                  pl.BlockSpec((1, 1, 1, V_DIM, nctx), lambda bi, hi, i: (bi, hi, 0, 0, 0))],
        out_specs=pl.BlockSpec((1, tq, V_DIM), lambda bi, hi, i: (bi, i, hi)),
        out_shape=jax.ShapeDtypeStruct((b, n, h * V_DIM), BF16),
        scratch_shapes=[pltpu.VMEM((1, LANES), F32),
                        pltpu.VMEM((1, tq), F32), pltpu.VMEM((1, tq), F32),
                        pltpu.VMEM((V_DIM, tq), F32)],
        name="attention",
        compiler_params=_cp(("arbitrary", "arbitrary", "arbitrary")),
    )(q, k, vt, kc, vtc)


def _fft_a_kernel(za_ref, zb_ref, w1_ref, tc_ref, ts_ref, yr_ref, yi_ref):
    ab = jnp.concatenate([za_ref[0], zb_ref[0]], axis=0)
    y = _dot(w1_ref[...], ab)
    re = y[:FFT_N1]
    im = y[FFT_N1:]
    tc = tc_ref[...]
    ts = ts_ref[...]
    yr_ref[0] = (re * tc + im * ts).astype(BF16)
    yi_ref[0] = (im * tc - re * ts).astype(BF16)


def _fft_b_kernel(yr_ref, yi_ref, m_ref, bdw_ref, o_ref):
    ri = jnp.concatenate([yr_ref[0], yi_ref[0]], axis=0)
    f = _dot(m_ref[...], ri)
    four = _dot(f.astype(BF16), bdw_ref[...])
    o_ref[0] = four.reshape(o_ref.shape[1:])


def _fourier(za, zb, w1, tc, ts, mmat, bdw):
    b, n, c = za.shape
    n2 = n // FFT_N1
    wide = n2 * c
    tn = min(2048, wide)
    za2 = za.reshape(b, FFT_N1, wide)
    zb2 = zb.reshape(b, FFT_N1, wide)
    yr, yi = pl.pallas_call(
        _fft_a_kernel,
        grid=(wide // tn, b),
        in_specs=[pl.BlockSpec((1, FFT_N1, tn), lambda i, bi: (bi, 0, i)),
                  pl.BlockSpec((1, FFT_N1, tn), lambda i, bi: (bi, 0, i)),
                  pl.BlockSpec(w1.shape, lambda i, bi: (0, 0)),
                  pl.BlockSpec((FFT_N1, tn), lambda i, bi: (0, i)),
                  pl.BlockSpec((FFT_N1, tn), lambda i, bi: (0, i))],
        out_specs=[pl.BlockSpec((1, FFT_N1, tn), lambda i, bi: (bi, 0, i)),
                   pl.BlockSpec((1, FFT_N1, tn), lambda i, bi: (bi, 0, i))],
        out_shape=[jax.ShapeDtypeStruct((b, FFT_N1, wide), BF16)] * 2,
        name="fourier_a",
        compiler_params=_cp(("arbitrary", "arbitrary")),
    )(za2, zb2, w1, tc, ts)
    rows = FFT_K1_TILE * n2
    yr = yr.reshape(b, n, c)
    yi = yi.reshape(b, n, c)
    out = pl.pallas_call(
        _fft_b_kernel,
        grid=(b, FFT_N1 // FFT_K1_TILE),
        in_specs=[pl.BlockSpec((1, rows, c), lambda bi, i: (bi, i, 0)),
                  pl.BlockSpec((1, rows, c), lambda bi, i: (bi, i, 0)),
                  pl.BlockSpec(mmat.shape, lambda bi, i: (0, 0)),
                  pl.BlockSpec(bdw.shape, lambda bi, i: (0, 0))],
        out_specs=pl.BlockSpec((1, n2, FFT_K1_TILE, c), lambda bi, i: (bi, 0, i, 0)),
        out_shape=jax.ShapeDtypeStruct((b, n2, FFT_N1, c), F32),
        name="fourier_b",
        compiler_params=_cp(("arbitrary", "arbitrary")),
    )(yr, yi, mmat, bdw)
    return out.reshape(b, n, c)


def _post_kernel(att_ref, four_ref, x_ref, g1_ref, sh2_ref, sc2_ref, gpost_ref, gpre_ref, woa_ref, wof_ref,
                 wrt_ref, br_ref, tri_ref, x1_ref, hx_ref, ridx_ref, gates_ref, cnt_ref, carry_sc):
    first = (pl.program_id(0) == 0) & (pl.program_id(1) == 0)

    @pl.when(first)
    def _():
        carry_sc[...] = jnp.zeros_like(carry_sc)

    mix = _dot(att_ref[0], woa_ref[...]) + _dot(four_ref[0].astype(BF16), wof_ref[...])
    x1 = x_ref[0] + g1_ref[0] * (_rms(mix) * gpost_ref[...])
    x1_ref[0] = x1
    hx = _rms(x1) * gpre_ref[...] * (1.0 + sc2_ref[0]) + sh2_ref[0]
    hb = hx.astype(BF16)
    hx_ref[...] = _pack_bf16_pairs(hb)
    lg = _dot_nt(wrt_ref[...], hb) + br_ref[...]
    iota_e = lax.broadcasted_iota(I32, lg.shape, 0)
    cur = lg
    tops, idxs, ohs = [], [], []
    for _ in range(TOP_K):
        mx = jnp.max(cur, axis=0, keepdims=True)
        idx = jnp.min(jnp.where(cur == mx, iota_e, N_EXPERTS), axis=0, keepdims=True)
        oh = iota_e == idx
        cur = jnp.where(oh, -jnp.inf, cur)
        tops.append(mx)
        idxs.append(idx)
        ohs.append(oh)
    es = [jnp.exp(t - tops[0]) for t in tops]
    den = es[0] + es[1] + es[2] + es[3]
    gates = [e / den for e in es]
    ohsum = ohs[0].astype(F32) + ohs[1].astype(F32) + ohs[2].astype(F32) + ohs[3].astype(F32)
    base = carry_sc[:, 0:1] + _dot(ohsum.astype(BF16), tri_ref[...])
    ranks = [jnp.sum(jnp.where(oh, base, 0.0), axis=0, keepdims=True).astype(I32) for oh in ohs]
    ridx_ref[0] = _stack_rows(idxs + ranks, 2 * TOP_K)
    gates_ref[...] = _stack_rows(gates, LANES).T
    carry_sc[...] = carry_sc[...] + jnp.sum(ohsum, axis=1, keepdims=True)
    cnt_ref[...] = carry_sc[...]


def _post_attention(att, four, x, m3, gpost, gpre, woa, wof, wrt, br, tri, tm):
    b, n, d = x.shape
    nt = n // tm
    t = b * n
    full = lambda a: pl.BlockSpec(a.shape, lambda bi, i: (0,) * a.ndim)
    mrow = lambda j: pl.BlockSpec((1, 1, d), lambda bi, i: (bi * N_MOD + j, 0, 0))
    return pl.pallas_call(
        _post_kernel,
        grid=(b, nt),
        in_specs=[pl.BlockSpec((1, tm, att.shape[2]), lambda bi, i: (bi, i, 0)),
                  pl.BlockSpec((1, tm, F_WIDTH), lambda bi, i: (bi, i, 0)),
                  pl.BlockSpec((1, tm, d), lambda bi, i: (bi, i, 0)),
                  mrow(2), mrow(3), mrow(4),
                  full(gpost), full(gpre), full(woa), full(wof), full(wrt), full(br), full(tri)],
        out_specs=[pl.BlockSpec((1, tm, d), lambda bi, i: (bi, i, 0)),
                   pl.BlockSpec((tm, d // 2), lambda bi, i: (bi * nt + i, 0)),
                   pl.BlockSpec((1, 2 * TOP_K, tm), lambda bi, i: (bi * nt + i, 0, 0)),
                   pl.BlockSpec((tm, LANES), lambda bi, i: (bi * nt + i, 0)),
                   pl.BlockSpec((N_EXPERTS, LANES), lambda bi, i: (0, 0))],
        out_shape=[jax.ShapeDtypeStruct((b, n, d), F32),
                   jax.ShapeDtypeStruct((t, d // 2), U32),
                   jax.ShapeDtypeStruct((t // tm, 2 * TOP_K, tm), I32),
                   jax.ShapeDtypeStruct((t, LANES), F32),
                   jax.ShapeDtypeStruct((N_EXPERTS, LANES), F32)],
        scratch_shapes=[pltpu.VMEM((N_EXPERTS, LANES), F32)],
        name="post_attention",
        compiler_params=_cp(("arbitrary", "arbitrary")),
    )(att, four, x, m3, m3, m3, gpost, gpre, woa, wof, wrt, br, tri)


def _dispatch_kernel(fill_ref, dest_ref, hx_ref, xs_hbm, zrow, sem, fsem, *, tm, bm):
    i = pl.program_id(0)

    @pl.when(i == 0)
    def _():
        zrow[...] = jnp.zeros_like(zrow)
        nblk = xs_hbm.shape[0] // bm

        def tail(blk, c):
            cp = pltpu.make_async_copy(zrow, xs_hbm.at[pl.ds(blk * bm, bm), :], fsem)
            cp.start()
            cp.wait()
            return c

        lax.fori_loop(fill_ref[2 * N_EXPERTS], nblk, tail, 0)

        def per_expert(e, c):
            s = fill_ref[e]
            n = fill_ref[N_EXPERTS + e]
            head = jnp.minimum(n, (-s) & (SUBLANES - 1))
            body = s + head
            nchunk = (n - head) // SUBLANES

            def row_copy(r):
                return pltpu.make_async_copy(zrow.at[pl.ds(0, 1), :], xs_hbm.at[pl.ds(s + r, 1), :], fsem)

            def chunk_copy(j):
                dst = pl.multiple_of(body + j * SUBLANES, SUBLANES)
                return pltpu.make_async_copy(zrow.at[pl.ds(0, SUBLANES), :], xs_hbm.at[pl.ds(dst, SUBLANES), :], fsem)

            def loop(n_it, copy, action):
                def step(r, cc):
                    action(copy(r))
                    return cc

                lax.fori_loop(0, n_it, step, 0)

            loop(head, row_copy, lambda cp: cp.start())
            loop(nchunk, chunk_copy, lambda cp: cp.start())
            loop(head, row_copy, lambda cp: cp.wait())
            loop(nchunk, chunk_copy, lambda cp: cp.wait())
            return c

        lax.fori_loop(0, N_EXPERTS, per_expert, 0)

    def row_copy(g, u, k):
        d = dest_ref[0, 0, k * tm + g * SUBLANES + u]
        return pltpu.make_async_copy(hx_ref.at[g, pl.ds(u, 1), :], xs_hbm.at[pl.ds(d, 1), :], sem)

    def start(g, c):
        for u in range(SUBLANES):
            for k in range(TOP_K):
                row_copy(g, u, k).start(priority=k % 2)
        return c

    def wait(g, c):
        for u in range(SUBLANES):
            for k in range(TOP_K):
                row_copy(g, u, k).wait()
        return c

    lax.fori_loop(0, tm // SUBLANES, start, 0)
    lax.fori_loop(0, tm // SUBLANES, wait, 0)


def _dispatch(fill, dest_tiles, hx, n_slots, tm, bm):
    t, dm = hx.shape
    kern = functools.partial(_dispatch_kernel, tm=tm, bm=bm)
    return pl.pallas_call(
        kern,
        grid_spec=pltpu.PrefetchScalarGridSpec(
            num_scalar_prefetch=1,
            grid=(t // tm,),
            in_specs=[pl.BlockSpec((1, 1, TOP_K * tm), lambda i, f: (i, 0, 0), memory_space=pltpu.SMEM),
                      pl.BlockSpec((tm // SUBLANES, SUBLANES, dm), lambda i, f: (i, 0, 0))],
            out_specs=pl.BlockSpec(memory_space=pl.ANY),
            scratch_shapes=[pltpu.VMEM((bm, dm), hx.dtype),
                            pltpu.SemaphoreType.DMA(()), pltpu.SemaphoreType.DMA(())]),
        out_shape=jax.ShapeDtypeStruct((n_slots, dm), hx.dtype),
        name="dispatch",
        compiler_params=_cp(("arbitrary",)),
    )(fill, dest_tiles, hx.reshape(t // SUBLANES, SUBLANES, dm))


def _moe_kernel(be_ref, nv_ref, ne_ref, x_ref, wg_hbm, bg_ref, wu_hbm, bu_ref, wd_hbm, bd_ref, y_ref,
                wgb, wub, wdb, wgs, wus, wds, sem):
    i = pl.program_id(0)

    def fetch(e):
        return (pltpu.make_async_copy(wg_hbm.at[e], wgs, sem.at[0]),
                pltpu.make_async_copy(wu_hbm.at[e], wus, sem.at[1]),
                pltpu.make_async_copy(wd_hbm.at[e], wds, sem.at[2]))

    @pl.when(i < nv_ref[0])
    def _():
        e = be_ref[i]
        prev = be_ref[jnp.maximum(i - 1, 0)]

        @pl.when(i == 0)
        def _():
            for cp in fetch(e):
                cp.start()

        @pl.when((i == 0) | (e != prev))
        def _():
            for cp in fetch(e):
                cp.wait()
            wgb[...] = wgs[...].astype(BF16)
            wub[...] = wus[...].astype(BF16)
            wdb[...] = wds[...].astype(BF16)
            nxt = ne_ref[i]

            @pl.when(nxt >= 0)
            def _():
                for cp in fetch(nxt):
                    cp.start()

        xb = _unpack_bf16_pairs(x_ref[...])
        g = jnp.minimum(_dot(xb, wgb[...]) + bg_ref[0], SWIGLU_LIMIT)
        u = jnp.clip(_dot(xb, wub[...]) + bu_ref[0], -SWIGLU_LIMIT, SWIGLU_LIMIT)
        act = g * (1.0 / (1.0 + jnp.exp(-SWIGLU_ALPHA * g))) * (u + 1.0)
        y_ref[...] = _dot(act.astype(BF16), wdb[...]) + bd_ref[0]

    @pl.when(i >= nv_ref[0])
    def _():
        y_ref[...] = jnp.zeros_like(y_ref)


def _experts(block_e, nvalid, next_e, xs, wg, bg, wu, bu, wd, bd, bm):
    n_slots = xs.shape[0]
    e, d, f = wg.shape
    nb = n_slots // bm
    xmap = lambda i, be, nv, ne: (jnp.minimum(i, nv[0] - 1), 0)
    bmap = lambda i, be, nv, ne: (be[i], 0, 0)
    hbm = pl.BlockSpec(memory_space=pl.ANY)
    return pl.pallas_call(
        _moe_kernel,
        grid_spec=pltpu.PrefetchScalarGridSpec(
            num_scalar_prefetch=3,
            grid=(nb,),
            in_specs=[pl.BlockSpec((bm, xs.shape[1]), xmap),
                      hbm, pl.BlockSpec((1, 1, f), bmap),
                      hbm, pl.BlockSpec((1, 1, f), bmap),
                      hbm, pl.BlockSpec((1, 1, d), bmap)],
            out_specs=pl.BlockSpec((bm, d), lambda i, be, nv, ne: (i, 0)),
            scratch_shapes=[pltpu.VMEM((d, f), BF16), pltpu.VMEM((d, f), BF16), pltpu.VMEM((f, d), BF16),
                            pltpu.VMEM((d, f), F32), pltpu.VMEM((d, f), F32), pltpu.VMEM((f, d), F32),
                            pltpu.SemaphoreType.DMA((3,))]),
        out_shape=jax.ShapeDtypeStruct((n_slots, d), F32),
        name="experts",
        compiler_params=_cp(("arbitrary",)),
    )(block_e, nvalid, next_e, xs, wg, bg.reshape(e, 1, f), wu, bu.reshape(e, 1, f), wd, bd.reshape(e, 1, d))


def _combine_kernel(dcur_ref, dnxt_ref, ys_hbm, gates_ref, x1_ref, g2_ref, gpost_ref, o_ref, buf, sem,
                    *, tm, nsteps):
    i = pl.program_id(0)
    slot = i % 2

    def row_copy(dref, sl, g, u, k):
        d = dref[0, 0, k * tm + g * SUBLANES + u]
        return pltpu.make_async_copy(ys_hbm.at[pl.ds(d, 1), :], buf.at[sl, k, g, pl.ds(u, 1), :], sem.at[sl])

    def issue(dref, sl):
        def start(g, c):
            for u in range(SUBLANES):
                for k in range(TOP_K):
                    row_copy(dref, sl, g, u, k).start(priority=k % 2)
            return c

        lax.fori_loop(0, tm // SUBLANES, start, 0)

    @pl.when(i == 0)
    def _():
        issue(dcur_ref, 0)

    @pl.when(i + 1 < nsteps)
    def _():
        issue(dnxt_ref, 1 - slot)

    def wait(g, c):
        for u in range(SUBLANES):
            for k in range(TOP_K):
                row_copy(dcur_ref, slot, g, u, k).wait()
        return c

    lax.fori_loop(0, tm // SUBLANES, wait, 0)
    gt = gates_ref[...]
    rows = lambda k: buf[slot, k].reshape(tm, buf.shape[-1])
    y = gt[:, 0:1] * rows(0)
    for k in range(1, TOP_K):
        y = y + gt[:, k:k + 1] * rows(k)
    o_ref[...] = x1_ref[...] + g2_ref[0] * (_rms(y) * gpost_ref[...])


def _combine(dest_tiles, ys, gates_t, x1, m3, gpost, tm):
    b, n, d = x1.shape
    nt = n // tm
    nsteps = b * nt
    kern = functools.partial(_combine_kernel, tm=tm, nsteps=nsteps)
    dspec = lambda f: pl.BlockSpec((1, 1, TOP_K * tm), f, memory_space=pltpu.SMEM)
    out = pl.pallas_call(
        kern,
        grid=(nsteps,),
        in_specs=[dspec(lambda i: (i, 0, 0)),
                  dspec(lambda i: (jnp.minimum(i + 1, nsteps - 1), 0, 0)),
                  pl.BlockSpec(memory_space=pl.ANY),
                  pl.BlockSpec((tm, LANES), lambda i: (i, 0)),
                  pl.BlockSpec((tm, d), lambda i: (i, 0)),
                  pl.BlockSpec((1, 1, d), lambda i: (i // nt * N_MOD + 5, 0, 0)),
                  pl.BlockSpec(gpost.shape, lambda i: (0, 0))],
        out_specs=pl.BlockSpec((tm, d), lambda i: (i, 0)),
        out_shape=jax.ShapeDtypeStruct((b * n, d), F32),
        scratch_shapes=[pltpu.VMEM((2, TOP_K, tm // SUBLANES, SUBLANES, d), F32),
                        pltpu.SemaphoreType.DMA((2,))],
        name="combine",
        compiler_params=_cp(("arbitrary",)),
    )(dest_tiles, dest_tiles, ys, gates_t, x1.reshape(b * n, d), m3, gpost)
    return out.reshape(b, n, d)


def _rope_tables(n):
    f32 = np.float32
    rows = n // GRID_W
    row = np.repeat(np.arange(rows, dtype=f32), GRID_W)
    col = np.tile(np.arange(GRID_W, dtype=f32), rows)
    half = QK_ROPE // 2
    inv = (f32(1.0) / (f32(ROPE_BASE) ** (np.arange(0, half, 2, dtype=f32) / f32(half)))).astype(f32)
    ang = np.concatenate([row[:, None] * inv, col[:, None] * inv], axis=-1).astype(f32)
    cos, sin = np.cos(ang), np.sin(ang)
    zero = np.zeros((n, ROPE_W - QK_ROPE), f32)
    return (jnp.asarray(np.concatenate([cos, cos, zero], axis=1)),
            jnp.asarray(np.concatenate([sin, sin, zero], axis=1)))


def _ctx_tables(n):
    half = QK_ROPE // 2
    t1 = np.zeros((n, ROPE_W), np.float32)
    t1[:, :2 * half] = 1.0
    return jnp.asarray(t1), jnp.zeros((n, ROPE_W), F32)


def _prep_weights(w_in, norm_q, norm_kv, w_q_up, w_kv_up):
    d = w_in.shape[0]
    half = QK_ROPE // 2
    zr = lambda r, c: jnp.zeros((r, c), F32)
    o_rope = F_WIDTH + Q_LORA + KV_LORA
    kr = w_in[:, o_rope:o_rope + QK_ROPE]
    kre, kro = kr[:, 0::2], kr[:, 1::2]
    win = jnp.concatenate([w_in[:, :o_rope],
                           kre, kro, zr(d, ROPE_W - QK_ROPE),
                           -kro, kre, zr(d, ROPE_W - QK_ROPE)], axis=1).astype(BF16)
    hd = QK_NOPE + QK_ROPE
    qa, qb = [], []
    for h in range(N_HEADS):
        wn = w_q_up[:, h * hd:h * hd + QK_NOPE]
        wr = w_q_up[:, h * hd + QK_NOPE:(h + 1) * hd]
        we, wo = wr[:, 0::2], wr[:, 1::2]
        qa += [wn, we, wo, zr(Q_LORA, ROPE_W - QK_ROPE)]
        qb += [-wo, we, zr(Q_LORA, ROPE_W - QK_ROPE)]
    wqa = jnp.concatenate(qa, axis=1).astype(BF16)
    wqb = jnp.concatenate(qb, axis=1).astype(BF16)
    kvd = QK_NOPE + V_DIM
    wk = jnp.concatenate([w_kv_up[:, h * kvd:h * kvd + QK_NOPE] for h in range(N_HEADS)], axis=1).astype(BF16)
    wv = jnp.concatenate([w_kv_up[:, h * kvd + QK_NOPE:(h + 1) * kvd] for h in range(N_HEADS)], axis=1)
    wvt = wv.T.astype(BF16)
    return win, norm_q.reshape(1, -1), norm_kv.reshape(1, -1), wqa, wqb, wk, wvt


def _fourier_consts(n, w_fourier):
    c = F_GDIM
    j = np.arange(c)
    ang = 2.0 * np.pi * np.outer(j, j) / c
    eye = np.eye(F_GROUPS)
    bdc = np.kron(eye, np.cos(ang))
    bds = np.kron(eye, np.sin(ang))
    bdcs = jnp.asarray(np.concatenate([bdc, -bds], axis=1), dtype=BF16)
    n1 = FFT_N1
    n2 = n // n1
    k1 = np.arange(n1)
    a1 = 2.0 * np.pi * np.outer(k1, k1) / n1
    wc, ws = np.cos(a1), np.sin(a1)
    w1 = jnp.asarray(np.block([[wc, ws], [-ws, wc]]), dtype=BF16)
    tw = 2.0 * np.pi * np.outer(k1, np.arange(n2)) / n
    tc = jnp.asarray(np.repeat(np.cos(tw).astype(np.float32), F_WIDTH, axis=1))
    ts = jnp.asarray(np.repeat(np.sin(tw).astype(np.float32), F_WIDTH, axis=1))
    k2 = np.arange(n2)
    a2 = 2.0 * np.pi * np.outer(k2, k2) / n2
    norm = 1.0 / math.sqrt(n * c)
    kt = FFT_K1_TILE
    mc = np.zeros((n2, kt, kt, n2))
    ms = np.zeros((n2, kt, kt, n2))
    for r in range(kt):
        mc[:, r, r, :] = np.cos(a2) * norm
        ms[:, r, r, :] = np.sin(a2) * norm
    mmat = np.concatenate([mc.reshape(n2 * kt, kt * n2), ms.reshape(n2 * kt, kt * n2)], axis=1)
    mmat = jnp.asarray(mmat, dtype=BF16)
    zblk = jnp.zeros((c, c), F32)
    bdw = jnp.concatenate(
        [jnp.concatenate([w_fourier[g] if j == g else zblk for j in range(F_GROUPS)], axis=1)
         for g in range(F_GROUPS)], axis=0)
    return bdcs, w1, tc, ts, mmat, bdw.astype(BF16)


def kernel(x, c, ctx, c_ctx, w_mod, b_mod, norm_attn_pre, norm_attn_post, norm_ffn_pre, norm_ffn_post, w_in, norm_q_lat, norm_kv_lat, w_q_up, w_kv_up, w_fourier, w_out, w_router, b_router, w_gate, b_gate, w_up, b_up, w_down, b_down):
    b, n, d = x.shape
    nctx = ctx.shape[1]
    t = b * n
    assert w_mod.shape[0] == 1 and b + 1 <= SUBLANES and n % (FFT_N1 * SUBLANES) == 0
    row2 = lambda a: a[0].reshape(1, -1)

    c8 = jnp.concatenate([c, c_ctx[None, :], jnp.zeros((SUBLANES - b - 1, d), F32)], axis=0)
    m3 = _modulation(c8, w_mod[0], b_mod[0]).reshape(SUBLANES * N_MOD, 1, d)

    wts = _prep_weights(w_in[0], norm_q_lat[0], norm_kv_lat[0], w_q_up[0], w_kv_up[0])
    bdcs, w1, tc, ts, mmat, bdw = _fourier_consts(n, w_fourier[0])
    pro_w = (wts[0], bdcs) + wts[1:]
    scale = (QK_NOPE + QK_ROPE) ** -0.5 * LOG2E
    g_pre = row2(norm_attn_pre)

    tm = min(512, n)
    t1x, t2x = _rope_tables(n)
    q, k, vt, za, zb = _prologue(x, m3, lambda bi: bi, g_pre, pro_w, t1x, t2x, tm, scale)
    t1c, t2c = _ctx_tables(nctx)
    _, kc, vtc, _, _ = _prologue(ctx, m3, lambda bi: b, g_pre, pro_w, t1c, t2c, nctx, scale)

    att = _attention(q, k, vt, kc, vtc, min(1024, n))
    four = _fourier(za, zb, w1, tc, ts, mmat, bdw)

    tmp = min(512, n)
    tri = jnp.asarray(np.triu(np.ones((tmp, tmp), np.float32), 1), dtype=BF16)
    width = N_HEADS * V_DIM
    woa = w_out[0, :width].astype(BF16)
    wof = w_out[0, width:].astype(BF16)
    x1, hx2, ridx, gates_t, cnt = _post_attention(
        att, four, x, m3, row2(norm_attn_post), row2(norm_ffn_pre), woa, wof,
        w_router[0].T.astype(BF16), b_router[0].reshape(-1, 1), tri, tmp)

    bm = 512
    counts = cnt[:, 0].astype(I32)
    padded = (counts + bm - 1) // bm * bm
    pad_end = jnp.cumsum(padded)
    pad_start = pad_end - padded
    nb = t * TOP_K // bm + N_EXPERTS
    blk_start = jnp.arange(nb, dtype=I32)[:, None] * bm
    block_e = jnp.minimum(jnp.sum((pad_end[None, :] <= blk_start).astype(I32), axis=1), N_EXPERTS - 1)
    nvalid = (pad_end[-1:] // bm).astype(I32)
    ids = jnp.arange(N_EXPERTS, dtype=I32)
    later = (ids[None, :] > ids[:, None]) & (counts[None, :] > 0)
    nxt = jnp.min(jnp.where(later, ids[None, :], N_EXPERTS), axis=1)
    nxt = jnp.where(nxt == N_EXPERTS, -1, nxt)
    next_e = jnp.sum(jnp.where(ids[None, :] == block_e[:, None], nxt[None, :], 0), axis=1).astype(I32)
    eids = jnp.arange(N_EXPERTS, dtype=I32)[:, None, None, None]
    eidx, rank = ridx[:, :TOP_K], ridx[:, TOP_K:]
    dest = jnp.sum(jnp.where(eidx[None] == eids, pad_start[:, None, None, None], 0), axis=0) + rank
    dest_tiles = dest.reshape(t // tmp, 1, TOP_K * tmp)
    fill = jnp.concatenate([pad_start + counts, padded - counts, nvalid]).astype(I32)

    xs = _dispatch(fill, dest_tiles, hx2, nb * bm, tmp, bm)
    ys = _experts(block_e, nvalid, next_e, xs, w_gate[0], b_gate[0], w_up[0], b_up[0], w_down[0], b_down[0], bm)
    return _combine(dest_tiles, ys, gates_t, x1, m3, row2(norm_ffn_post), tmp)
```

```python
import functools
import math

import numpy as np
import jax
import jax.numpy as jnp
from jax import lax
from jax.experimental import pallas as pl
from jax.experimental.pallas import tpu as pltpu

F32 = jnp.float32
BF16 = jnp.bfloat16
I32 = jnp.int32
U32 = jnp.uint32

N_HEADS = 6
QK_NOPE = 128
QK_ROPE = 64
V_DIM = 128
Q_LORA = 384
KV_LORA = 256
F_GROUPS = 4
F_GDIM = 64
F_WIDTH = F_GROUPS * F_GDIM
GRID_W = 64
ROPE_BASE = 10000.0
N_EXPERTS = 32
TOP_K = 4
SWIGLU_LIMIT = 7.0
SWIGLU_ALPHA = 1.702
RMS_EPS = 1e-6
N_MOD = 6

LANES = 128
SUBLANES = 8
HEAD_W = 256
ROPE_W = 128
FFT_N1 = 128
FFT_K1_TILE = 8
VMEM_LIMIT = 56 * 1024 * 1024

LOG2E = 1.4426950408889634
NORM_SLACK = 1.0 + 2.0 ** -6
MIN_ROW_SUM = 2.0 ** -60


def _cp(sem, vmem=None):
    return pltpu.CompilerParams(dimension_semantics=sem, vmem_limit_bytes=vmem or VMEM_LIMIT)


def _rms(x):
    return x * lax.rsqrt(jnp.mean(x * x, axis=-1, keepdims=True) + RMS_EPS)


def _dot(a, b):
    return jnp.dot(a, b, preferred_element_type=F32)


def _stack_rows(rows, height):
    w = rows[0].shape[1]
    sub = lax.broadcasted_iota(I32, (height, w), 0)
    out = jnp.zeros((height, w), rows[0].dtype)
    for r, v in enumerate(rows):
        out = jnp.where(sub == r, v, out)
    return out


def _pack_bf16_pairs(xb):
    w = xb.shape[1] // 2
    lo = lax.bitcast_convert_type(xb[:, :w].astype(F32), U32)
    hi = lax.bitcast_convert_type(xb[:, w:].astype(F32), U32)
    return (hi & jnp.uint32(0xFFFF0000)) | (lo >> 16)


def _unpack_bf16_pairs(p):
    lo = lax.bitcast_convert_type(p << 16, F32)
    hi = lax.bitcast_convert_type(p & jnp.uint32(0xFFFF0000), F32)
    return jnp.concatenate([lo, hi], axis=1).astype(BF16)


def _dot_nt(a, b):
    return lax.dot_general(a, b, (((1,), (1,)), ((), ())), preferred_element_type=F32)


def _mod_kernel(c_ref, w_ref, b_ref, o_ref):
    c = c_ref[...]
    a = c / (1.0 + jnp.exp(-c))
    a_hi = a.astype(BF16)
    a_lo = (a - a_hi.astype(F32)).astype(BF16)
    w = w_ref[...]
    w_hi = w.astype(BF16)
    w_lo = (w - w_hi.astype(F32)).astype(BF16)
    acc = _dot(a_hi, w_hi) + _dot(a_lo, w_hi) + _dot(a_hi, w_lo)
    o_ref[...] = acc + b_ref[...]


def _modulation(c8, w_mod, b_mod):
    d, n = w_mod.shape
    tn = 1024
    return pl.pallas_call(
        _mod_kernel,
        grid=(n // tn,),
        in_specs=[pl.BlockSpec((SUBLANES, d), lambda j: (0, 0)),
                  pl.BlockSpec((d, tn), lambda j: (0, j)),
                  pl.BlockSpec((1, tn), lambda j: (0, j))],
        out_specs=pl.BlockSpec((SUBLANES, tn), lambda j: (0, j)),
        out_shape=jax.ShapeDtypeStruct((SUBLANES, n), F32),
        name="modulation",
        compiler_params=_cp(("arbitrary",)),
    )(c8, w_mod, b_mod.reshape(1, n))


def _pro_kernel(x_ref, sh_ref, sc_ref, g_ref, win_ref, bdcs_ref, nq_ref, nkv_ref, wqa_ref, wqb_ref,
                wk_ref, wvt_ref, t1_ref, t2_ref, q_ref, k_ref, vt_ref, za_ref, zb_ref, *, scale):
    x = x_ref[0]
    hx = _rms(x) * g_ref[...] * (1.0 + sc_ref[0]) + sh_ref[0]
    px = _dot(hx.astype(BF16), win_ref[...])
    u = px[:, 0:F_WIDTH].astype(BF16)
    z = _dot(u, bdcs_ref[...])
    za_ref[0] = z[:, :F_WIDTH].astype(BF16)
    zb_ref[0] = z[:, F_WIDTH:].astype(BF16)
    o_q = F_WIDTH
    o_kv = o_q + Q_LORA
    o_ra = o_kv + KV_LORA
    o_rb = o_ra + ROPE_W
    t1 = t1_ref[...]
    t2 = t2_ref[...]
    qn = (_rms(px[:, o_q:o_kv]) * nq_ref[...]).astype(BF16)
    qa = _dot(qn, wqa_ref[...])
    qb = _dot(qn, wqb_ref[...])
    for h in range(N_HEADS):
        nope = qa[:, h * HEAD_W:h * HEAD_W + QK_NOPE] * scale
        rope = (qa[:, h * HEAD_W + QK_NOPE:(h + 1) * HEAD_W] * t1
                + qb[:, h * ROPE_W:(h + 1) * ROPE_W] * t2) * scale
        q_ref[0, h, :, 0:QK_NOPE] = nope.astype(BF16)
        q_ref[0, h, :, QK_NOPE:HEAD_W] = rope.astype(BF16)
    kvn = (_rms(px[:, o_kv:o_ra]) * nkv_ref[...]).astype(BF16)
    kn = _dot(kvn, wk_ref[...])
    kr = (px[:, o_ra:o_rb] * t1 + px[:, o_rb:o_rb + ROPE_W] * t2).astype(BF16)
    for h in range(N_HEADS):
        k_ref[0, h, :, 0:QK_NOPE] = kn[:, h * QK_NOPE:(h + 1) * QK_NOPE].astype(BF16)
        k_ref[0, h, :, QK_NOPE:HEAD_W] = kr
    vt = _dot_nt(wvt_ref[...], kvn)
    for h in range(N_HEADS):
        vt_ref[0, h, 0] = vt[h * V_DIM:(h + 1) * V_DIM, :].astype(BF16)


def _prologue(x, m3, mod_row, g_pre, wts, t1, t2, tm, scale):
    b, n, d = x.shape
    nt = n // tm
    win, bdcs, nq, nkv, wqa, wqb, wk, wvt = wts
    full = lambda a: pl.BlockSpec(a.shape, lambda bi, i: (0,) * a.ndim)
    kern = functools.partial(_pro_kernel, scale=scale)
    return pl.pallas_call(
        kern,
        grid=(b, nt),
        in_specs=[pl.BlockSpec((1, tm, d), lambda bi, i: (bi, i, 0)),
                  pl.BlockSpec((1, 1, d), lambda bi, i: (mod_row(bi) * N_MOD + 0, 0, 0)),
                  pl.BlockSpec((1, 1, d), lambda bi, i: (mod_row(bi) * N_MOD + 1, 0, 0)),
                  full(g_pre), full(win), full(bdcs), full(nq), full(nkv), full(wqa), full(wqb),
                  full(wk), full(wvt),
                  pl.BlockSpec((tm, ROPE_W), lambda bi, i: (i, 0)),
                  pl.BlockSpec((tm, ROPE_W), lambda bi, i: (i, 0))],
        out_specs=[pl.BlockSpec((1, N_HEADS, tm, HEAD_W), lambda bi, i: (bi, 0, i, 0)),
                   pl.BlockSpec((1, N_HEADS, tm, HEAD_W), lambda bi, i: (bi, 0, i, 0)),
                   pl.BlockSpec((1, N_HEADS, 1, V_DIM, tm), lambda bi, i: (bi, 0, i, 0, 0)),
                   pl.BlockSpec((1, tm, F_WIDTH), lambda bi, i: (bi, i, 0)),
                   pl.BlockSpec((1, tm, F_WIDTH), lambda bi, i: (bi, i, 0))],
        out_shape=[jax.ShapeDtypeStruct((b, N_HEADS, n, HEAD_W), BF16),
                   jax.ShapeDtypeStruct((b, N_HEADS, n, HEAD_W), BF16),
                   jax.ShapeDtypeStruct((b, N_HEADS, nt, V_DIM, tm), BF16),
                   jax.ShapeDtypeStruct((b, n, F_WIDTH), BF16),
                   jax.ShapeDtypeStruct((b, n, F_WIDTH), BF16)],
        name="prologue",
        compiler_params=_cp(("arbitrary", "arbitrary")),
    )(x, m3, m3, g_pre, win, bdcs, nq, nkv, wqa, wqb, wk, wvt, t1, t2)


def _attn_kernel(q_ref, k_ref, vt_ref, kc_ref, vtc_ref, o_ref, kmax_sc, m_sc, l_sc, acc_sc,
                 *, nkv, tk, group_size, wide):
    q = q_ref[0, 0]
    ones_kw = jnp.ones((HEAD_W, LANES), BF16)

    def sq_norms(kb):
        kf = kb.astype(F32)
        return jnp.max(_dot((kf * kf).astype(BF16), ones_kw), axis=0, keepdims=True)

    @pl.when(pl.program_id(2) == 0)
    def _():
        def body(j, best):
            start = pl.multiple_of(j * tk, tk)
            return jnp.maximum(best, sq_norms(k_ref[0, 0, pl.ds(start, tk), :]))

        kmax_sc[...] = lax.fori_loop(0, nkv, body, sq_norms(kc_ref[0, 0]))

    qf = q.astype(F32)
    qn2 = _dot_nt(jnp.ones((SUBLANES, HEAD_W), BF16), (qf * qf).astype(BF16))[0:1]
    ref = jnp.sqrt(qn2 * kmax_sc[:, 0:1]) * NORM_SLACK

    def fast_chunk(kb, vtb):
        p = jnp.exp2(_dot_nt(kb, q) - ref)
        l_sc[...] += jnp.sum(p, axis=0, keepdims=True)
        acc_sc[...] += _dot(vtb, p.astype(BF16))

    l_sc[...] = jnp.zeros_like(l_sc)
    acc_sc[...] = jnp.zeros_like(acc_sc)
    fast_chunk(kc_ref[0, 0], vtc_ref[0, 0, 0])

    def fast_body(i, carry):
        for u in range(0, group_size, wide):
            j = group_size * i + u
            start = pl.multiple_of(j * tk, tk)
            vtb = jnp.concatenate([vt_ref[0, 0, j + w] for w in range(wide)], axis=1)
            fast_chunk(k_ref[0, 0, pl.ds(start, wide * tk), :], vtb)
        return carry

    lax.fori_loop(0, nkv // group_size, fast_body, 0)

    @pl.when(jnp.min(l_sc[...]) < MIN_ROW_SUM)
    def _():
        s = _dot_nt(kc_ref[0, 0], q)
        m0 = jnp.max(s, axis=0, keepdims=True)
        p = jnp.exp2(s - m0)
        m_sc[...] = m0
        l_sc[...] = jnp.sum(p, axis=0, keepdims=True)
        acc_sc[...] = _dot(vtc_ref[0, 0, 0], p.astype(BF16))

        def body(j, carry):
            start = pl.multiple_of(j * tk, tk)
            s = _dot_nt(k_ref[0, 0, pl.ds(start, tk), :], q)
            m_prev = m_sc[...]
            m_new = jnp.maximum(m_prev, jnp.max(s, axis=0, keepdims=True))
            alpha = jnp.exp2(m_prev - m_new)
            p = jnp.exp2(s - m_new)
            l_sc[...] = alpha * l_sc[...] + jnp.sum(p, axis=0, keepdims=True)
            acc_sc[...] = alpha * acc_sc[...] + _dot(vt_ref[0, 0, j], p.astype(BF16))
            m_sc[...] = m_new
            return carry

        lax.fori_loop(0, nkv, body, 0)

    o = acc_sc[...] / l_sc[...]
    o_ref[0] = o.T.astype(BF16)


def _attention(q, k, vt, kc, vtc, tq):
    b, h, n, _ = q.shape
    nkv, tk = vt.shape[2], vt.shape[4]
    nctx = kc.shape[2]
    group_size = next(g for g in (16, 8, 4, 2, 1) if nkv % g == 0)
    wide = min(4, group_size)
    kern = functools.partial(_attn_kernel, nkv=nkv, tk=tk, group_size=group_size, wide=wide)
    return pl.pallas_call(
        kern,
        grid=(b, h, n // tq),
        in_specs=[pl.BlockSpec((1, 1, tq, HEAD_W), lambda bi, hi, i: (bi, hi, i, 0)),
                  pl.BlockSpec((1, 1, n, HEAD_W), lambda bi, hi, i: (bi, hi, 0, 0)),
                  pl.BlockSpec((1, 1, nkv, V_DIM, tk), lambda bi, hi, i: (bi, hi, 0, 0, 0)),
                  pl.BlockSpec((1, 1, nctx, HEAD_W), lambda bi, hi, i: (bi, hi, 0, 0)),
                  pl.BlockSpec((1, 1, 1, V_DIM, nctx), lambda bi, hi, i: (bi, hi, 0, 0, 0))],
        out_specs=pl.BlockSpec((1, tq, V_DIM), lambda bi, hi, i: (bi, i, hi)),
        out_shape=jax.ShapeDtypeStruct((b, n, h * V_DIM), BF16),
        scratch_shapes=[pltpu.VMEM((1, LANES), F32),
                        pltpu.VMEM((1, tq), F32), pltpu.VMEM((1, tq), F32),
                        pltpu.VMEM((V_DIM, tq), F32)],
        name="attention",
        compiler_params=_cp(("arbitrary", "arbitrary", "arbitrary")),
    )(q, k, vt, kc, vtc)


def _fft_a_kernel(za_ref, zb_ref, w1_ref, tc_ref, ts_ref, yr_ref, yi_ref):
    ab = jnp.concatenate([za_ref[0], zb_ref[0]], axis=0)
    y = _dot(w1_ref[...], ab)
    re = y[:FFT_N1]
    im = y[FFT_N1:]
    tc = tc_ref[...]
    ts = ts_ref[...]
    yr_ref[0] = (re * tc + im * ts).astype(BF16)
    yi_ref[0] = (im * tc - re * ts).astype(BF16)


def _fft_b_kernel(yr_ref, yi_ref, m_ref, bdw_ref, o_ref):
    ri = jnp.concatenate([yr_ref[0], yi_ref[0]], axis=0)
    f = _dot(m_ref[...], ri)
    four = _dot(f.astype(BF16), bdw_ref[...])
    o_ref[0] = four.reshape(o_ref.shape[1:])


def _fourier(za, zb, w1, tc, ts, mmat, bdw):
    b, n, c = za.shape
    n2 = n // FFT_N1
    wide = n2 * c
    tn = min(2048, wide)
    za2 = za.reshape(b, FFT_N1, wide)
    zb2 = zb.reshape(b, FFT_N1, wide)
    yr, yi = pl.pallas_call(
        _fft_a_kernel,
        grid=(wide // tn, b),
        in_specs=[pl.BlockSpec((1, FFT_N1, tn), lambda i, bi: (bi, 0, i)),
                  pl.BlockSpec((1, FFT_N1, tn), lambda i, bi: (bi, 0, i)),
                  pl.BlockSpec(w1.shape, lambda i, bi: (0, 0)),
                  pl.BlockSpec((FFT_N1, tn), lambda i, bi: (0, i)),
                  pl.BlockSpec((FFT_N1, tn), lambda i, bi: (0, i))],
        out_specs=[pl.BlockSpec((1, FFT_N1, tn), lambda i, bi: (bi, 0, i)),
                   pl.BlockSpec((1, FFT_N1, tn), lambda i, bi: (bi, 0, i))],
        out_shape=[jax.ShapeDtypeStruct((b, FFT_N1, wide), BF16)] * 2,
        name="fourier_a",
        compiler_params=_cp(("arbitrary", "arbitrary")),
    )(za2, zb2, w1, tc, ts)
    rows = FFT_K1_TILE * n2
    yr = yr.reshape(b, n, c)
    yi = yi.reshape(b, n, c)
    out = pl.pallas_call(
        _fft_b_kernel,
        grid=(b, FFT_N1 // FFT_K1_TILE),
        in_specs=[pl.BlockSpec((1, rows, c), lambda bi, i: (bi, i, 0)),
                  pl.BlockSpec((1, rows, c), lambda bi, i: (bi, i, 0)),
                  pl.BlockSpec(mmat.shape, lambda bi, i: (0, 0)),
                  pl.BlockSpec(bdw.shape, lambda bi, i: (0, 0))],
        out_specs=pl.BlockSpec((1, n2, FFT_K1_TILE, c), lambda bi, i: (bi, 0, i, 0)),
        out_shape=jax.ShapeDtypeStruct((b, n2, FFT_N1, c), F32),
        name="fourier_b",
        compiler_params=_cp(("arbitrary", "arbitrary")),
    )(yr, yi, mmat, bdw)
    return out.reshape(b, n, c)


def _post_kernel(att_ref, four_ref, x_ref, g1_ref, sh2_ref, sc2_ref, gpost_ref, gpre_ref, woa_ref, wof_ref,
                 wrt_ref, br_ref, tri_ref, x1_ref, hx_ref, ridx_ref, gates_ref, cnt_ref, carry_sc):
    first = (pl.program_id(0) == 0) & (pl.program_id(1) == 0)

    @pl.when(first)
    def _():
        carry_sc[...] = jnp.zeros_like(carry_sc)

    mix = _dot(att_ref[0], woa_ref[...]) + _dot(four_ref[0].astype(BF16), wof_ref[...])
    x1 = x_ref[0] + g1_ref[0] * (_rms(mix) * gpost_ref[...])
    x1_ref[0] = x1
    hx = _rms(x1) * gpre_ref[...] * (1.0 + sc2_ref[0]) + sh2_ref[0]
    hb = hx.astype(BF16)
    hx_ref[...] = _pack_bf16_pairs(hb)
    lg = _dot_nt(wrt_ref[...], hb) + br_ref[...]
    iota_e = lax.broadcasted_iota(I32, lg.shape, 0)
    cur = lg
    tops, idxs, ohs = [], [], []
    for _ in range(TOP_K):
        mx = jnp.max(cur, axis=0, keepdims=True)
        idx = jnp.min(jnp.where(cur == mx, iota_e, N_EXPERTS), axis=0, keepdims=True)
        oh = iota_e == idx
        cur = jnp.where(oh, -jnp.inf, cur)
        tops.append(mx)
        idxs.append(idx)
        ohs.append(oh)
    es = [jnp.exp(t - tops[0]) for t in tops]
    den = es[0] + es[1] + es[2] + es[3]
    gates = [e / den for e in es]
    ohsum = ohs[0].astype(F32) + ohs[1].astype(F32) + ohs[2].astype(F32) + ohs[3].astype(F32)
    base = carry_sc[:, 0:1] + _dot(ohsum.astype(BF16), tri_ref[...])
    ranks = [jnp.sum(jnp.where(oh, base, 0.0), axis=0, keepdims=True).astype(I32) for oh in ohs]
    ridx_ref[0] = _stack_rows(idxs + ranks, 2 * TOP_K)
    gates_ref[...] = _stack_rows(gates, LANES).T
    carry_sc[...] = carry_sc[...] + jnp.sum(ohsum, axis=1, keepdims=True)
    cnt_ref[...] = carry_sc[...]


def _post_attention(att, four, x, m3, gpost, gpre, woa, wof, wrt, br, tri, tm):
    b, n, d = x.shape
    nt = n // tm
    t = b * n
    full = lambda a: pl.BlockSpec(a.shape, lambda bi, i: (0,) * a.ndim)
    mrow = lambda j: pl.BlockSpec((1, 1, d), lambda bi, i: (bi * N_MOD + j, 0, 0))
    return pl.pallas_call(
        _post_kernel,
        grid=(b, nt),
        in_specs=[pl.BlockSpec((1, tm, att.shape[2]), lambda bi, i: (bi, i, 0)),
                  pl.BlockSpec((1, tm, F_WIDTH), lambda bi, i: (bi, i, 0)),
                  pl.BlockSpec((1, tm, d), lambda bi, i: (bi, i, 0)),
                  mrow(2), mrow(3), mrow(4),
                  full(gpost), full(gpre), full(woa), full(wof), full(wrt), full(br), full(tri)],
        out_specs=[pl.BlockSpec((1, tm, d), lambda bi, i: (bi, i, 0)),
                   pl.BlockSpec((tm, d // 2), lambda bi, i: (bi * nt + i, 0)),
                   pl.BlockSpec((1, 2 * TOP_K, tm), lambda bi, i: (bi * nt + i, 0, 0)),
                   pl.BlockSpec((tm, LANES), lambda bi, i: (bi * nt + i, 0)),
                   pl.BlockSpec((N_EXPERTS, LANES), lambda bi, i: (0, 0))],
        out_shape=[jax.ShapeDtypeStruct((b, n, d), F32),
                   jax.ShapeDtypeStruct((t, d // 2), U32),
                   jax.ShapeDtypeStruct((t // tm, 2 * TOP_K, tm), I32),
                   jax.ShapeDtypeStruct((t, LANES), F32),
                   jax.ShapeDtypeStruct((N_EXPERTS, LANES), F32)],
        scratch_shapes=[pltpu.VMEM((N_EXPERTS, LANES), F32)],
        name="post_attention",
        compiler_params=_cp(("arbitrary", "arbitrary")),
    )(att, four, x, m3, m3, m3, gpost, gpre, woa, wof, wrt, br, tri)


def _dispatch_kernel(fill_ref, dest_ref, hx_ref, xs_hbm, zrow, sem, fsem, *, tm, bm):
    i = pl.program_id(0)

    @pl.when(i == 0)
    def _():
        zrow[...] = jnp.zeros_like(zrow)
        nblk = xs_hbm.shape[0] // bm

        def tail(blk, c):
            cp = pltpu.make_async_copy(zrow, xs_hbm.at[pl.ds(blk * bm, bm), :], fsem)
            cp.start()
            cp.wait()
            return c

        lax.fori_loop(fill_ref[2 * N_EXPERTS], nblk, tail, 0)

        def per_expert(e, c):
            s = fill_ref[e]
            n = fill_ref[N_EXPERTS + e]
            head = jnp.minimum(n, (-s) & (SUBLANES - 1))
            body = s + head
            nchunk = (n - head) // SUBLANES

            def row_copy(r):
                return pltpu.make_async_copy(zrow.at[pl.ds(0, 1), :], xs_hbm.at[pl.ds(s + r, 1), :], fsem)

            def chunk_copy(j):
                dst = pl.multiple_of(body + j * SUBLANES, SUBLANES)
                return pltpu.make_async_copy(zrow.at[pl.ds(0, SUBLANES), :], xs_hbm.at[pl.ds(dst, SUBLANES), :], fsem)

            def loop(n_it, copy, action):
                def step(r, cc):
                    action(copy(r))
                    return cc

                lax.fori_loop(0, n_it, step, 0)

            loop(head, row_copy, lambda cp: cp.start())
            loop(nchunk, chunk_copy, lambda cp: cp.start())
            loop(head, row_copy, lambda cp: cp.wait())
            loop(nchunk, chunk_copy, lambda cp: cp.wait())
            return c

        lax.fori_loop(0, N_EXPERTS, per_expert, 0)

    def row_copy(g, u, k):
        d = dest_ref[0, 0, k * tm + g * SUBLANES + u]
        return pltpu.make_async_copy(hx_ref.at[g, pl.ds(u, 1), :], xs_hbm.at[pl.ds(d, 1), :], sem)

    def start(g, c):
        for u in range(SUBLANES):
            for k in range(TOP_K):
                row_copy(g, u, k).start(priority=k % 2)
        return c

    def wait(g, c):
        for u in range(SUBLANES):
            for k in range(TOP_K):
                row_copy(g, u, k).wait()
        return c

    lax.fori_loop(0, tm // SUBLANES, start, 0)
    lax.fori_loop(0, tm // SUBLANES, wait, 0)


def _dispatch(fill, dest_tiles, hx, n_slots, tm, bm):
    t, dm = hx.shape
    kern = functools.partial(_dispatch_kernel, tm=tm, bm=bm)
    return pl.pallas_call(
        kern,
        grid_spec=pltpu.PrefetchScalarGridSpec(
            num_scalar_prefetch=1,
            grid=(t // tm,),
            in_specs=[pl.BlockSpec((1, 1, TOP_K * tm), lambda i, f: (i, 0, 0), memory_space=pltpu.SMEM),
                      pl.BlockSpec((tm // SUBLANES, SUBLANES, dm), lambda i, f: (i, 0, 0))],
            out_specs=pl.BlockSpec(memory_space=pl.ANY),
            scratch_shapes=[pltpu.VMEM((bm, dm), hx.dtype),
                            pltpu.SemaphoreType.DMA(()), pltpu.SemaphoreType.DMA(())]),
        out_shape=jax.ShapeDtypeStruct((n_slots, dm), hx.dtype),
        name="dispatch",
        compiler_params=_cp(("arbitrary",)),
    )(fill, dest_tiles, hx.reshape(t // SUBLANES, SUBLANES, dm))


def _moe_kernel(be_ref, nv_ref, ne_ref, bv_ref, x_ref, wg_hbm, bg_ref, wu_hbm, bu_ref, wd_hbm, bd_ref, y_ref,
                wgb, wub, wdb, wgs, wus, wds, sem, *, row_step):
    i = pl.program_id(0)
    bm = x_ref.shape[0]

    def fetch(e):
        return (pltpu.make_async_copy(wg_hbm.at[e], wgs, sem.at[0]),
                pltpu.make_async_copy(wu_hbm.at[e], wus, sem.at[1]),
                pltpu.make_async_copy(wd_hbm.at[e], wds, sem.at[2]))

    @pl.when(i < nv_ref[0])
    def _():
        e = be_ref[i]
        prev = be_ref[jnp.maximum(i - 1, 0)]

        @pl.when(i == 0)
        def _():
            for cp in fetch(e):
                cp.start()

        @pl.when((i == 0) | (e != prev))
        def _():
            for cp in fetch(e):
                cp.wait()
            wgb[...] = wgs[...].astype(BF16)
            wub[...] = wus[...].astype(BF16)
            wdb[...] = wds[...].astype(BF16)
            nxt = ne_ref[i]

            @pl.when(nxt >= 0)
            def _():
                for cp in fetch(nxt):
                    cp.start()

        def compute(rows):
            xb = _unpack_bf16_pairs(x_ref[0:rows, :])
            g = jnp.minimum(_dot(xb, wgb[...]) + bg_ref[0], SWIGLU_LIMIT)
            u = jnp.clip(_dot(xb, wub[...]) + bu_ref[0], -SWIGLU_LIMIT, SWIGLU_LIMIT)
            act = g * (1.0 / (1.0 + jnp.exp(-SWIGLU_ALPHA * g))) * (u + 1.0)
            y_ref[0:rows, :] = _dot(act.astype(BF16), wdb[...]) + bd_ref[0]
            if rows < bm:
                y_ref[rows:bm, :] = jnp.zeros((bm - rows, y_ref.shape[1]), y_ref.dtype)

        valid = bv_ref[i]
        for rows in range(row_step, bm + 1, row_step):
            lo = rows - row_step if rows > row_step else -1
            pl.when((valid > lo) & (valid <= rows))(functools.partial(compute, rows))

    @pl.when(i >= nv_ref[0])
    def _():
        y_ref[...] = jnp.zeros_like(y_ref)


def _experts(block_e, nvalid, next_e, block_valid, xs, wg, bg, wu, bu, wd, bd, bm):
    n_slots = xs.shape[0]
    e, d, f = wg.shape
    nb = n_slots // bm
    xmap = lambda i, be, nv, ne, bv: (jnp.minimum(i, nv[0] - 1), 0)
    bmap = lambda i, be, nv, ne, bv: (be[i], 0, 0)
    hbm = pl.BlockSpec(memory_space=pl.ANY)
    return pl.pallas_call(
        functools.partial(_moe_kernel, row_step=bm // 4),
        grid_spec=pltpu.PrefetchScalarGridSpec(
            num_scalar_prefetch=4,
            grid=(nb,),
            in_specs=[pl.BlockSpec((bm, xs.shape[1]), xmap),
                      hbm, pl.BlockSpec((1, 1, f), bmap),
                      hbm, pl.BlockSpec((1, 1, f), bmap),
                      hbm, pl.BlockSpec((1, 1, d), bmap)],
            out_specs=pl.BlockSpec((bm, d), lambda i, be, nv, ne, bv: (i, 0)),
            scratch_shapes=[pltpu.VMEM((d, f), BF16), pltpu.VMEM((d, f), BF16), pltpu.VMEM((f, d), BF16),
                            pltpu.VMEM((d, f), F32), pltpu.VMEM((d, f), F32), pltpu.VMEM((f, d), F32),
                            pltpu.SemaphoreType.DMA((3,))]),
        out_shape=jax.ShapeDtypeStruct((n_slots, d), F32),
        name="experts",
        compiler_params=_cp(("arbitrary",)),
    )(block_e, nvalid, next_e, block_valid, xs, wg, bg.reshape(e, 1, f), wu, bu.reshape(e, 1, f), wd,
      bd.reshape(e, 1, d))


def _combine_kernel(dcur_ref, dnxt_ref, ys_hbm, gates_ref, x1_ref, g2_ref, gpost_ref, o_ref, buf, sem,
                    *, tm, nsteps):
    i = pl.program_id(0)
    slot = i % 2

    def row_copy(dref, sl, g, u, k):
        d = dref[0, 0, k * tm + g * SUBLANES + u]
        return pltpu.make_async_copy(ys_hbm.at[pl.ds(d, 1), :], buf.at[sl, k, g, pl.ds(u, 1), :], sem.at[sl])

    def issue(dref, sl):
        def start(g, c):
            for u in range(SUBLANES):
                for k in range(TOP_K):
                    row_copy(dref, sl, g, u, k).start(priority=k % 2)
            return c

        lax.fori_loop(0, tm // SUBLANES, start, 0)

    @pl.when(i == 0)
    def _():
        issue(dcur_ref, 0)

    @pl.when(i + 1 < nsteps)
    def _():
        issue(dnxt_ref, 1 - slot)

    def wait(g, c):
        for u in range(SUBLANES):
            for k in range(TOP_K):
                row_copy(dcur_ref, slot, g, u, k).wait()
        return c

    lax.fori_loop(0, tm // SUBLANES, wait, 0)
    gt = gates_ref[...]
    rows = lambda k: buf[slot, k].reshape(tm, buf.shape[-1])
    y = gt[:, 0:1] * rows(0)
    for k in range(1, TOP_K):
        y = y + gt[:, k:k + 1] * rows(k)
    o_ref[...] = x1_ref[...] + g2_ref[0] * (_rms(y) * gpost_ref[...])


def _combine(dest_tiles, ys, gates_t, x1, m3, gpost, tm):
    b, n, d = x1.shape
    nt = n // tm
    nsteps = b * nt
    kern = functools.partial(_combine_kernel, tm=tm, nsteps=nsteps)
    dspec = lambda f: pl.BlockSpec((1, 1, TOP_K * tm), f, memory_space=pltpu.SMEM)
    out = pl.pallas_call(
        kern,
        grid=(nsteps,),
        in_specs=[dspec(lambda i: (i, 0, 0)),
                  dspec(lambda i: (jnp.minimum(i + 1, nsteps - 1), 0, 0)),
                  pl.BlockSpec(memory_space=pl.ANY),
                  pl.BlockSpec((tm, LANES), lambda i: (i, 0)),
                  pl.BlockSpec((tm, d), lambda i: (i, 0)),
                  pl.BlockSpec((1, 1, d), lambda i: (i // nt * N_MOD + 5, 0, 0)),
                  pl.BlockSpec(gpost.shape, lambda i: (0, 0))],
        out_specs=pl.BlockSpec((tm, d), lambda i: (i, 0)),
        out_shape=jax.ShapeDtypeStruct((b * n, d), F32),
        scratch_shapes=[pltpu.VMEM((2, TOP_K, tm // SUBLANES, SUBLANES, d), F32),
                        pltpu.SemaphoreType.DMA((2,))],
        name="combine",
        compiler_params=_cp(("arbitrary",)),
    )(dest_tiles, dest_tiles, ys, gates_t, x1.reshape(b * n, d), m3, gpost)
    return out.reshape(b, n, d)


def _rope_tables(n):
    f32 = np.float32
    rows = n // GRID_W
    row = np.repeat(np.arange(rows, dtype=f32), GRID_W)
    col = np.tile(np.arange(GRID_W, dtype=f32), rows)
    half = QK_ROPE // 2
    inv = (f32(1.0) / (f32(ROPE_BASE) ** (np.arange(0, half, 2, dtype=f32) / f32(half)))).astype(f32)
    ang = np.concatenate([row[:, None] * inv, col[:, None] * inv], axis=-1).astype(f32)
    cos, sin = np.cos(ang), np.sin(ang)
    zero = np.zeros((n, ROPE_W - QK_ROPE), f32)
    return (jnp.asarray(np.concatenate([cos, cos, zero], axis=1)),
            jnp.asarray(np.concatenate([sin, sin, zero], axis=1)))


def _ctx_tables(n):
    half = QK_ROPE // 2
    t1 = np.zeros((n, ROPE_W), np.float32)
    t1[:, :2 * half] = 1.0
    return jnp.asarray(t1), jnp.zeros((n, ROPE_W), F32)


def _prep_weights(w_in, norm_q, norm_kv, w_q_up, w_kv_up):
    d = w_in.shape[0]
    half = QK_ROPE // 2
    zr = lambda r, c: jnp.zeros((r, c), F32)
    o_rope = F_WIDTH + Q_LORA + KV_LORA
    kr = w_in[:, o_rope:o_rope + QK_ROPE]
    kre, kro = kr[:, 0::2], kr[:, 1::2]
    win = jnp.concatenate([w_in[:, :o_rope],
                           kre, kro, zr(d, ROPE_W - QK_ROPE),
                           -kro, kre, zr(d, ROPE_W - QK_ROPE)], axis=1).astype(BF16)
    hd = QK_NOPE + QK_ROPE
    qa, qb = [], []
    for h in range(N_HEADS):
        wn = w_q_up[:, h * hd:h * hd + QK_NOPE]
        wr = w_q_up[:, h * hd + QK_NOPE:(h + 1) * hd]
        we, wo = wr[:, 0::2], wr[:, 1::2]
        qa += [wn, we, wo, zr(Q_LORA, ROPE_W - QK_ROPE)]
        qb += [-wo, we, zr(Q_LORA, ROPE_W - QK_ROPE)]
    wqa = jnp.concatenate(qa, axis=1).astype(BF16)
    wqb = jnp.concatenate(qb, axis=1).astype(BF16)
    kvd = QK_NOPE + V_DIM
    wk = jnp.concatenate([w_kv_up[:, h * kvd:h * kvd + QK_NOPE] for h in range(N_HEADS)], axis=1).astype(BF16)
    wv = jnp.concatenate([w_kv_up[:, h * kvd + QK_NOPE:(h + 1) * kvd] for h in range(N_HEADS)], axis=1)
    wvt = wv.T.astype(BF16)
    return win, norm_q.reshape(1, -1), norm_kv.reshape(1, -1), wqa, wqb, wk, wvt


def _fourier_consts(n, w_fourier):
    c = F_GDIM
    j = np.arange(c)
    ang = 2.0 * np.pi * np.outer(j, j) / c
    eye = np.eye(F_GROUPS)
    bdc = np.kron(eye, np.cos(ang))
    bds = np.kron(eye, np.sin(ang))
    bdcs = jnp.asarray(np.concatenate([bdc, -bds], axis=1), dtype=BF16)
    n1 = FFT_N1
    n2 = n // n1
    k1 = np.arange(n1)
    a1 = 2.0 * np.pi * np.outer(k1, k1) / n1
    wc, ws = np.cos(a1), np.sin(a1)
    w1 = jnp.asarray(np.block([[wc, ws], [-ws, wc]]), dtype=BF16)
    tw = 2.0 * np.pi * np.outer(k1, np.arange(n2)) / n
    tc = jnp.asarray(np.repeat(np.cos(tw).astype(np.float32), F_WIDTH, axis=1))
    ts = jnp.asarray(np.repeat(np.sin(tw).astype(np.float32), F_WIDTH, axis=1))
    k2 = np.arange(n2)
    a2 = 2.0 * np.pi * np.outer(k2, k2) / n2
    norm = 1.0 / math.sqrt(n * c)
    kt = FFT_K1_TILE
    mc = np.zeros((n2, kt, kt, n2))
    ms = np.zeros((n2, kt, kt, n2))
    for r in range(kt):
        mc[:, r, r, :] = np.cos(a2) * norm
        ms[:, r, r, :] = np.sin(a2) * norm
    mmat = np.concatenate([mc.reshape(n2 * kt, kt * n2), ms.reshape(n2 * kt, kt * n2)], axis=1)
    mmat = jnp.asarray(mmat, dtype=BF16)
    zblk = jnp.zeros((c, c), F32)
    bdw = jnp.concatenate(
        [jnp.concatenate([w_fourier[g] if j == g else zblk for j in range(F_GROUPS)], axis=1)
         for g in range(F_GROUPS)], axis=0)
    return bdcs, w1, tc, ts, mmat, bdw.astype(BF16)


def kernel(x, c, ctx, c_ctx, w_mod, b_mod, norm_attn_pre, norm_attn_post, norm_ffn_pre, norm_ffn_post, w_in, norm_q_lat, norm_kv_lat, w_q_up, w_kv_up, w_fourier, w_out, w_router, b_router, w_gate, b_gate, w_up, b_up, w_down, b_down):
    b, n, d = x.shape
    nctx = ctx.shape[1]
    t = b * n
    assert w_mod.shape[0] == 1 and b + 1 <= SUBLANES and n % (FFT_N1 * SUBLANES) == 0
    row2 = lambda a: a[0].reshape(1, -1)

    c8 = jnp.concatenate([c, c_ctx[None, :], jnp.zeros((SUBLANES - b - 1, d), F32)], axis=0)
    m3 = _modulation(c8, w_mod[0], b_mod[0]).reshape(SUBLANES * N_MOD, 1, d)

    wts = _prep_weights(w_in[0], norm_q_lat[0], norm_kv_lat[0], w_q_up[0], w_kv_up[0])
    bdcs, w1, tc, ts, mmat, bdw = _fourier_consts(n, w_fourier[0])
    pro_w = (wts[0], bdcs) + wts[1:]
    scale = (QK_NOPE + QK_ROPE) ** -0.5 * LOG2E
    g_pre = row2(norm_attn_pre)

    tm = min(512, n)
    t1x, t2x = _rope_tables(n)
    q, k, vt, za, zb = _prologue(x, m3, lambda bi: bi, g_pre, pro_w, t1x, t2x, tm, scale)
    t1c, t2c = _ctx_tables(nctx)
    _, kc, vtc, _, _ = _prologue(ctx, m3, lambda bi: b, g_pre, pro_w, t1c, t2c, nctx, scale)

    att = _attention(q, k, vt, kc, vtc, min(1024, n))
    four = _fourier(za, zb, w1, tc, ts, mmat, bdw)

    tmp = min(512, n)
    tri = jnp.asarray(np.triu(np.ones((tmp, tmp), np.float32), 1), dtype=BF16)
    width = N_HEADS * V_DIM
    woa = w_out[0, :width].astype(BF16)
    wof = w_out[0, width:].astype(BF16)
    x1, hx2, ridx, gates_t, cnt = _post_attention(
        att, four, x, m3, row2(norm_attn_post), row2(norm_ffn_pre), woa, wof,
        w_router[0].T.astype(BF16), b_router[0].reshape(-1, 1), tri, tmp)

    bm = 512
    counts = cnt[:, 0].astype(I32)
    padded = (counts + bm - 1) // bm * bm
    pad_end = jnp.cumsum(padded)
    pad_start = pad_end - padded
    nb = t * TOP_K // bm + N_EXPERTS
    blk_start = jnp.arange(nb, dtype=I32)[:, None] * bm
    block_e = jnp.minimum(jnp.sum((pad_end[None, :] <= blk_start).astype(I32), axis=1), N_EXPERTS - 1)
    nvalid = (pad_end[-1:] // bm).astype(I32)
    ids = jnp.arange(N_EXPERTS, dtype=I32)
    later = (ids[None, :] > ids[:, None]) & (counts[None, :] > 0)
    nxt = jnp.min(jnp.where(later, ids[None, :], N_EXPERTS), axis=1)
    nxt = jnp.where(nxt == N_EXPERTS, -1, nxt)
    of_block = ids[None, :] == block_e[:, None]
    next_e = jnp.sum(jnp.where(of_block, nxt[None, :], 0), axis=1).astype(I32)
    used_end = jnp.sum(jnp.where(of_block, (pad_start + counts)[None, :], 0), axis=1)
    block_valid = jnp.clip(used_end - blk_start[:, 0], 0, bm).astype(I32)
    eids = jnp.arange(N_EXPERTS, dtype=I32)[:, None, None, None]
    eidx, rank = ridx[:, :TOP_K], ridx[:, TOP_K:]
    dest = jnp.sum(jnp.where(eidx[None] == eids, pad_start[:, None, None, None], 0), axis=0) + rank
    dest_tiles = dest.reshape(t // tmp, 1, TOP_K * tmp)
    fill = jnp.concatenate([pad_start + counts, padded - counts, nvalid]).astype(I32)

    xs = _dispatch(fill, dest_tiles, hx2, nb * bm, tmp, bm)
    ys = _experts(block_e, nvalid, next_e, block_valid, xs, w_gate[0], b_gate[0], w_up[0], b_up[0], w_down[0], b_down[0], bm)
    return _combine(dest_tiles, ys, gates_t, x1, m3, row2(norm_ffn_post), tmp)
```

```python
import functools
import math

import numpy as np
import jax
import jax.numpy as jnp
from jax import lax
from jax.experimental import pallas as pl
from jax.experimental.pallas import tpu as pltpu

F32 = jnp.float32
BF16 = jnp.bfloat16
I32 = jnp.int32
U32 = jnp.uint32

N_HEADS = 6
QK_NOPE = 128
QK_ROPE = 64
V_DIM = 128
Q_LORA = 384
KV_LORA = 256
F_GROUPS = 4
F_GDIM = 64
F_WIDTH = F_GROUPS * F_GDIM
GRID_W = 64
ROPE_BASE = 10000.0
N_EXPERTS = 32
TOP_K = 4
SWIGLU_LIMIT = 7.0
SWIGLU_ALPHA = 1.702
RMS_EPS = 1e-6
N_MOD = 6

LANES = 128
SUBLANES = 8
HEAD_W = 256
ROPE_W = 128
FFT_N1 = 128
FFT_K1_TILE = 8
VMEM_LIMIT = 56 * 1024 * 1024

LOG2E = 1.4426950408889634
NORM_SLACK = 1.0 + 2.0 ** -6
MIN_ROW_SUM = 2.0 ** -60


def _cp(sem, vmem=None):
    return pltpu.CompilerParams(dimension_semantics=sem, vmem_limit_bytes=vmem or VMEM_LIMIT)


def _rms(x):
    return x * lax.rsqrt(jnp.mean(x * x, axis=-1, keepdims=True) + RMS_EPS)


def _dot(a, b):
    return jnp.dot(a, b, preferred_element_type=F32)


def _stack_rows(rows, height):
    w = rows[0].shape[1]
    sub = lax.broadcasted_iota(I32, (height, w), 0)
    out = jnp.zeros((height, w), rows[0].dtype)
    for r, v in enumerate(rows):
        out = jnp.where(sub == r, v, out)
    return out


def _pack_bf16_pairs(xb):
    w = xb.shape[1] // 2
    lo = lax.bitcast_convert_type(xb[:, :w].astype(F32), U32)
    hi = lax.bitcast_convert_type(xb[:, w:].astype(F32), U32)
    return (hi & jnp.uint32(0xFFFF0000)) | (lo >> 16)


def _unpack_bf16_pairs(p):
    lo = lax.bitcast_convert_type(p << 16, F32)
    hi = lax.bitcast_convert_type(p & jnp.uint32(0xFFFF0000), F32)
    return jnp.concatenate([lo, hi], axis=1).astype(BF16)


def _dot_nt(a, b):
    return lax.dot_general(a, b, (((1,), (1,)), ((), ())), preferred_element_type=F32)


def _mod_kernel(c_ref, w_ref, b_ref, o_ref):
    c = c_ref[...]
    a = c / (1.0 + jnp.exp(-c))
    a_hi = a.astype(BF16)
    a_lo = (a - a_hi.astype(F32)).astype(BF16)
    w = w_ref[...]
    w_hi = w.astype(BF16)
    w_lo = (w - w_hi.astype(F32)).astype(BF16)
    acc = _dot(a_hi, w_hi) + _dot(a_lo, w_hi) + _dot(a_hi, w_lo)
    o_ref[...] = acc + b_ref[...]


def _modulation(c8, w_mod, b_mod):
    d, n = w_mod.shape
    tn = 1024
    return pl.pallas_call(
        _mod_kernel,
        grid=(n // tn,),
        in_specs=[pl.BlockSpec((SUBLANES, d), lambda j: (0, 0)),
                  pl.BlockSpec((d, tn), lambda j: (0, j)),
                  pl.BlockSpec((1, tn), lambda j: (0, j))],
        out_specs=pl.BlockSpec((SUBLANES, tn), lambda j: (0, j)),
        out_shape=jax.ShapeDtypeStruct((SUBLANES, n), F32),
        name="modulation",
        compiler_params=_cp(("arbitrary",)),
    )(c8, w_mod, b_mod.reshape(1, n))


def _pro_kernel(x_ref, sh_ref, sc_ref, g_ref, win_ref, bdcs_ref, nq_ref, nkv_ref, wqa_ref, wqb_ref,
                wk_ref, wvt_ref, t1_ref, t2_ref, q_ref, k_ref, vt_ref, za_ref, zb_ref, *, scale):
    x = x_ref[0]
    hx = _rms(x) * g_ref[...] * (1.0 + sc_ref[0]) + sh_ref[0]
    px = _dot(hx.astype(BF16), win_ref[...])
    u = px[:, 0:F_WIDTH].astype(BF16)
    z = _dot(u, bdcs_ref[...])
    za_ref[0] = z[:, :F_WIDTH].astype(BF16)
    zb_ref[0] = z[:, F_WIDTH:].astype(BF16)
    o_q = F_WIDTH
    o_kv = o_q + Q_LORA
    o_ra = o_kv + KV_LORA
    o_rb = o_ra + ROPE_W
    t1 = t1_ref[...]
    t2 = t2_ref[...]
    qn = (_rms(px[:, o_q:o_kv]) * nq_ref[...]).astype(BF16)
    qa = _dot(qn, wqa_ref[...])
    qb = _dot(qn, wqb_ref[...])
    for h in range(N_HEADS):
        nope = qa[:, h * HEAD_W:h * HEAD_W + QK_NOPE] * scale
        rope = (qa[:, h * HEAD_W + QK_NOPE:(h + 1) * HEAD_W] * t1
                + qb[:, h * ROPE_W:(h + 1) * ROPE_W] * t2) * scale
        q_ref[0, h, :, 0:QK_NOPE] = nope.astype(BF16)
        q_ref[0, h, :, QK_NOPE:HEAD_W] = rope.astype(BF16)
    kvn = (_rms(px[:, o_kv:o_ra]) * nkv_ref[...]).astype(BF16)
    kn = _dot(kvn, wk_ref[...])
    kr = (px[:, o_ra:o_rb] * t1 + px[:, o_rb:o_rb + ROPE_W] * t2).astype(BF16)
    for h in range(N_HEADS):
        k_ref[0, h, :, 0:QK_NOPE] = kn[:, h * QK_NOPE:(h + 1) * QK_NOPE].astype(BF16)
        k_ref[0, h, :, QK_NOPE:HEAD_W] = kr
    vt = _dot_nt(wvt_ref[...], kvn)
    for h in range(N_HEADS):
        vt_ref[0, h, 0] = vt[h * V_DIM:(h + 1) * V_DIM, :].astype(BF16)


def _prologue(x, m3, mod_row, g_pre, wts, t1, t2, tm, scale):
    b, n, d = x.shape
    nt = n // tm
    win, bdcs, nq, nkv, wqa, wqb, wk, wvt = wts
    full = lambda a: pl.BlockSpec(a.shape, lambda bi, i: (0,) * a.ndim)
    kern = functools.partial(_pro_kernel, scale=scale)
    return pl.pallas_call(
        kern,
        grid=(b, nt),
        in_specs=[pl.BlockSpec((1, tm, d), lambda bi, i: (bi, i, 0)),
                  pl.BlockSpec((1, 1, d), lambda bi, i: (mod_row(bi) * N_MOD + 0, 0, 0)),
                  pl.BlockSpec((1, 1, d), lambda bi, i: (mod_row(bi) * N_MOD + 1, 0, 0)),
                  full(g_pre), full(win), full(bdcs), full(nq), full(nkv), full(wqa), full(wqb),
                  full(wk), full(wvt),
                  pl.BlockSpec((tm, ROPE_W), lambda bi, i: (i, 0)),
                  pl.BlockSpec((tm, ROPE_W), lambda bi, i: (i, 0))],
        out_specs=[pl.BlockSpec((1, N_HEADS, tm, HEAD_W), lambda bi, i: (bi, 0, i, 0)),
                   pl.BlockSpec((1, N_HEADS, tm, HEAD_W), lambda bi, i: (bi, 0, i, 0)),
                   pl.BlockSpec((1, N_HEADS, 1, V_DIM, tm), lambda bi, i: (bi, 0, i, 0, 0)),
                   pl.BlockSpec((1, tm, F_WIDTH), lambda bi, i: (bi, i, 0)),
                   pl.BlockSpec((1, tm, F_WIDTH), lambda bi, i: (bi, i, 0))],
        out_shape=[jax.ShapeDtypeStruct((b, N_HEADS, n, HEAD_W), BF16),
                   jax.ShapeDtypeStruct((b, N_HEADS, n, HEAD_W), BF16),
                   jax.ShapeDtypeStruct((b, N_HEADS, nt, V_DIM, tm), BF16),
                   jax.ShapeDtypeStruct((b, n, F_WIDTH), BF16),
                   jax.ShapeDtypeStruct((b, n, F_WIDTH), BF16)],
        name="prologue",
        compiler_params=_cp(("arbitrary", "arbitrary")),
    )(x, m3, m3, g_pre, win, bdcs, nq, nkv, wqa, wqb, wk, wvt, t1, t2)


def _attn_kernel(q_ref, k_ref, vt_ref, kc_ref, vtc_ref, o_ref, kmax_sc, m_sc, l_sc, acc_sc,
                 *, nkv, tk, group_size, wide):
    q = q_ref[0, 0]
    ones_kw = jnp.ones((HEAD_W, LANES), BF16)

    def sq_norms(kb):
        kf = kb.astype(F32)
        return jnp.max(_dot((kf * kf).astype(BF16), ones_kw), axis=0, keepdims=True)

    @pl.when(pl.program_id(2) == 0)
    def _():
        def body(j, best):
            start = pl.multiple_of(j * tk, tk)
            return jnp.maximum(best, sq_norms(k_ref[0, 0, pl.ds(start, tk), :]))

        kmax_sc[...] = lax.fori_loop(0, nkv, body, sq_norms(kc_ref[0, 0]))

    qf = q.astype(F32)
    qn2 = _dot_nt(jnp.ones((SUBLANES, HEAD_W), BF16), (qf * qf).astype(BF16))[0:1]
    ref = jnp.sqrt(qn2 * kmax_sc[:, 0:1]) * NORM_SLACK

    def fast_chunk(kb, vtb):
        p = jnp.exp2(_dot_nt(kb, q) - ref)
        l_sc[...] += jnp.sum(p, axis=0, keepdims=True)
        acc_sc[...] += _dot(vtb, p.astype(BF16))

    l_sc[...] = jnp.zeros_like(l_sc)
    acc_sc[...] = jnp.zeros_like(acc_sc)
    fast_chunk(kc_ref[0, 0], vtc_ref[0, 0, 0])

    def fast_body(i, carry):
        for u in range(0, group_size, wide):
            j = group_size * i + u
            start = pl.multiple_of(j * tk, tk)
            vtb = jnp.concatenate([vt_ref[0, 0, j + w] for w in range(wide)], axis=1)
            fast_chunk(k_ref[0, 0, pl.ds(start, wide * tk), :], vtb)
        return carry

    lax.fori_loop(0, nkv // group_size, fast_body, 0)

    @pl.when(jnp.min(l_sc[...]) < MIN_ROW_SUM)
    def _():
        s = _dot_nt(kc_ref[0, 0], q)
        m0 = jnp.max(s, axis=0, keepdims=True)
        p = jnp.exp2(s - m0)
        m_sc[...] = m0
        l_sc[...] = jnp.sum(p, axis=0, keepdims=True)
        acc_sc[...] = _dot(vtc_ref[0, 0, 0], p.astype(BF16))

        def body(j, carry):
            start = pl.multiple_of(j * tk, tk)
            s = _dot_nt(k_ref[0, 0, pl.ds(start, tk), :], q)
            m_prev = m_sc[...]
            m_new = jnp.maximum(m_prev, jnp.max(s, axis=0, keepdims=True))
            alpha = jnp.exp2(m_prev - m_new)
            p = jnp.exp2(s - m_new)
            l_sc[...] = alpha * l_sc[...] + jnp.sum(p, axis=0, keepdims=True)
            acc_sc[...] = alpha * acc_sc[...] + _dot(vt_ref[0, 0, j], p.astype(BF16))
            m_sc[...] = m_new
            return carry

        lax.fori_loop(0, nkv, body, 0)

    o = acc_sc[...] / l_sc[...]
    o_ref[0] = o.T.astype(BF16)


def _attention(q, k, vt, kc, vtc, tq):
    b, h, n, _ = q.shape
    nkv, tk = vt.shape[2], vt.shape[4]
    nctx = kc.shape[2]
    group_size = next(g for g in (16, 8, 4, 2, 1) if nkv % g == 0)
    wide = min(8, group_size)
    kern = functools.partial(_attn_kernel, nkv=nkv, tk=tk, group_size=group_size, wide=wide)
    return pl.pallas_call(
        kern,
        grid=(b, h, n // tq),
        in_specs=[pl.BlockSpec((1, 1, tq, HEAD_W), lambda bi, hi, i: (bi, hi, i, 0)),
                  pl.BlockSpec((1, 1, n, HEAD_W), lambda bi, hi, i: (bi, hi, 0, 0)),
                  pl.BlockSpec((1, 1, nkv, V_DIM, tk), lambda bi, hi, i: (bi, hi, 0, 0, 0)),
                  pl.BlockSpec((1, 1, nctx, HEAD_W), lambda bi, hi, i: (bi, hi, 0, 0)),
                  pl.BlockSpec((1, 1, 1, V_DIM, nctx), lambda bi, hi, i: (bi, hi, 0, 0, 0))],
        out_specs=pl.BlockSpec((1, tq, V_DIM), lambda bi, hi, i: (bi, i, hi)),
        out_shape=jax.ShapeDtypeStruct((b, n, h * V_DIM), BF16),
        scratch_shapes=[pltpu.VMEM((1, LANES), F32),
                        pltpu.VMEM((1, tq), F32), pltpu.VMEM((1, tq), F32),
                        pltpu.VMEM((V_DIM, tq), F32)],
        name="attention",
        compiler_params=_cp(("arbitrary", "arbitrary", "arbitrary")),
    )(q, k, vt, kc, vtc)


def _fft_a_kernel(za_ref, zb_ref, w1_ref, tc_ref, ts_ref, yr_ref, yi_ref):
    ab = jnp.concatenate([za_ref[0], zb_ref[0]], axis=0)
    y = _dot(w1_ref[...], ab)
    re = y[:FFT_N1]
    im = y[FFT_N1:]
    tc = tc_ref[...]
    ts = ts_ref[...]
    yr_ref[0] = (re * tc + im * ts).astype(BF16)
    yi_ref[0] = (im * tc - re * ts).astype(BF16)


def _fft_b_kernel(yr_ref, yi_ref, m_ref, bdw_ref, o_ref):
    ri = jnp.concatenate([yr_ref[0], yi_ref[0]], axis=0)
    f = _dot(m_ref[...], ri)
    four = _dot(f.astype(BF16), bdw_ref[...])
    o_ref[0] = four.reshape(o_ref.shape[1:])


def _fourier(za, zb, w1, tc, ts, mmat, bdw):
    b, n, c = za.shape
    n2 = n // FFT_N1
    wide = n2 * c
    tn = min(2048, wide)
    za2 = za.reshape(b, FFT_N1, wide)
    zb2 = zb.reshape(b, FFT_N1, wide)
    yr, yi = pl.pallas_call(
        _fft_a_kernel,
        grid=(wide // tn, b),
        in_specs=[pl.BlockSpec((1, FFT_N1, tn), lambda i, bi: (bi, 0, i)),
                  pl.BlockSpec((1, FFT_N1, tn), lambda i, bi: (bi, 0, i)),
                  pl.BlockSpec(w1.shape, lambda i, bi: (0, 0)),
                  pl.BlockSpec((FFT_N1, tn), lambda i, bi: (0, i)),
                  pl.BlockSpec((FFT_N1, tn), lambda i, bi: (0, i))],
        out_specs=[pl.BlockSpec((1, FFT_N1, tn), lambda i, bi: (bi, 0, i)),
                   pl.BlockSpec((1, FFT_N1, tn), lambda i, bi: (bi, 0, i))],
        out_shape=[jax.ShapeDtypeStruct((b, FFT_N1, wide), BF16)] * 2,
        name="fourier_a",
        compiler_params=_cp(("arbitrary", "arbitrary")),
    )(za2, zb2, w1, tc, ts)
    rows = FFT_K1_TILE * n2
    yr = yr.reshape(b, n, c)
    yi = yi.reshape(b, n, c)
    out = pl.pallas_call(
        _fft_b_kernel,
        grid=(b, FFT_N1 // FFT_K1_TILE),
        in_specs=[pl.BlockSpec((1, rows, c), lambda bi, i: (bi, i, 0)),
                  pl.BlockSpec((1, rows, c), lambda bi, i: (bi, i, 0)),
                  pl.BlockSpec(mmat.shape, lambda bi, i: (0, 0)),
                  pl.BlockSpec(bdw.shape, lambda bi, i: (0, 0))],
        out_specs=pl.BlockSpec((1, n2, FFT_K1_TILE, c), lambda bi, i: (bi, 0, i, 0)),
        out_shape=jax.ShapeDtypeStruct((b, n2, FFT_N1, c), F32),
        name="fourier_b",
        compiler_params=_cp(("arbitrary", "arbitrary")),
    )(yr, yi, mmat, bdw)
    return out.reshape(b, n, c)


def _post_kernel(att_ref, four_ref, x_ref, g1_ref, sh2_ref, sc2_ref, gpost_ref, gpre_ref, woa_ref, wof_ref,
                 wrt_ref, br_ref, tri_ref, x1_ref, hx_ref, ridx_ref, gates_ref, cnt_ref, carry_sc):
    first = (pl.program_id(0) == 0) & (pl.program_id(1) == 0)

    @pl.when(first)
    def _():
        carry_sc[...] = jnp.zeros_like(carry_sc)

    mix = _dot(att_ref[0], woa_ref[...]) + _dot(four_ref[0].astype(BF16), wof_ref[...])
    x1 = x_ref[0] + g1_ref[0] * (_rms(mix) * gpost_ref[...])
    x1_ref[0] = x1
    hx = _rms(x1) * gpre_ref[...] * (1.0 + sc2_ref[0]) + sh2_ref[0]
    hb = hx.astype(BF16)
    hx_ref[...] = _pack_bf16_pairs(hb)
    lg = _dot_nt(wrt_ref[...], hb) + br_ref[...]
    iota_e = lax.broadcasted_iota(I32, lg.shape, 0)
    cur = lg
    tops, idxs, ohs = [], [], []
    for _ in range(TOP_K):
        mx = jnp.max(cur, axis=0, keepdims=True)
        idx = jnp.min(jnp.where(cur == mx, iota_e, N_EXPERTS), axis=0, keepdims=True)
        oh = iota_e == idx
        cur = jnp.where(oh, -jnp.inf, cur)
        tops.append(mx)
        idxs.append(idx)
        ohs.append(oh)
    es = [jnp.exp(t - tops[0]) for t in tops]
    den = es[0] + es[1] + es[2] + es[3]
    gates = [e / den for e in es]
    ohsum = ohs[0].astype(F32) + ohs[1].astype(F32) + ohs[2].astype(F32) + ohs[3].astype(F32)
    base = carry_sc[:, 0:1] + _dot(ohsum.astype(BF16), tri_ref[...])
    ranks = [jnp.sum(jnp.where(oh, base, 0.0), axis=0, keepdims=True).astype(I32) for oh in ohs]
    ridx_ref[0] = _stack_rows(idxs + ranks, 2 * TOP_K)
    gates_ref[...] = _stack_rows(gates, LANES).T
    carry_sc[...] = carry_sc[...] + jnp.sum(ohsum, axis=1, keepdims=True)
    cnt_ref[...] = carry_sc[...]


def _post_attention(att, four, x, m3, gpost, gpre, woa, wof, wrt, br, tri, tm):
    b, n, d = x.shape
    nt = n // tm
    t = b * n
    full = lambda a: pl.BlockSpec(a.shape, lambda bi, i: (0,) * a.ndim)
    mrow = lambda j: pl.BlockSpec((1, 1, d), lambda bi, i: (bi * N_MOD + j, 0, 0))
    return pl.pallas_call(
        _post_kernel,
        grid=(b, nt),
        in_specs=[pl.BlockSpec((1, tm, att.shape[2]), lambda bi, i: (bi, i, 0)),
                  pl.BlockSpec((1, tm, F_WIDTH), lambda bi, i: (bi, i, 0)),
                  pl.BlockSpec((1, tm, d), lambda bi, i: (bi, i, 0)),
                  mrow(2), mrow(3), mrow(4),
                  full(gpost), full(gpre), full(woa), full(wof), full(wrt), full(br), full(tri)],
        out_specs=[pl.BlockSpec((1, tm, d), lambda bi, i: (bi, i, 0)),
                   pl.BlockSpec((tm, d // 2), lambda bi, i: (bi * nt + i, 0)),
                   pl.BlockSpec((1, 2 * TOP_K, tm), lambda bi, i: (bi * nt + i, 0, 0)),
                   pl.BlockSpec((tm, LANES), lambda bi, i: (bi * nt + i, 0)),
                   pl.BlockSpec((N_EXPERTS, LANES), lambda bi, i: (0, 0))],
        out_shape=[jax.ShapeDtypeStruct((b, n, d), F32),
                   jax.ShapeDtypeStruct((t, d // 2), U32),
                   jax.ShapeDtypeStruct((t // tm, 2 * TOP_K, tm), I32),
                   jax.ShapeDtypeStruct((t, LANES), F32),
                   jax.ShapeDtypeStruct((N_EXPERTS, LANES), F32)],
        scratch_shapes=[pltpu.VMEM((N_EXPERTS, LANES), F32)],
        name="post_attention",
        compiler_params=_cp(("arbitrary", "arbitrary")),
    )(att, four, x, m3, m3, m3, gpost, gpre, woa, wof, wrt, br, tri)


def _dispatch_kernel(fill_ref, dest_ref, hx_ref, xs_hbm, zrow, sem, fsem, *, tm, bm):
    i = pl.program_id(0)

    @pl.when(i == 0)
    def _():
        zrow[...] = jnp.zeros_like(zrow)
        nblk = xs_hbm.shape[0] // bm

        def tail(blk, c):
            cp = pltpu.make_async_copy(zrow, xs_hbm.at[pl.ds(blk * bm, bm), :], fsem)
            cp.start()
            cp.wait()
            return c

        lax.fori_loop(fill_ref[2 * N_EXPERTS], nblk, tail, 0)

        def per_expert(e, c):
            s = fill_ref[e]
            n = fill_ref[N_EXPERTS + e]
            head = jnp.minimum(n, (-s) & (SUBLANES - 1))
            body = s + head
            nchunk = (n - head) // SUBLANES

            def row_copy(r):
                return pltpu.make_async_copy(zrow.at[pl.ds(0, 1), :], xs_hbm.at[pl.ds(s + r, 1), :], fsem)

            def chunk_copy(j):
                dst = pl.multiple_of(body + j * SUBLANES, SUBLANES)
                return pltpu.make_async_copy(zrow.at[pl.ds(0, SUBLANES), :], xs_hbm.at[pl.ds(dst, SUBLANES), :], fsem)

            def loop(n_it, copy, action):
                def step(r, cc):
                    action(copy(r))
                    return cc

                lax.fori_loop(0, n_it, step, 0)

            loop(head, row_copy, lambda cp: cp.start())
            loop(nchunk, chunk_copy, lambda cp: cp.start())
            loop(head, row_copy, lambda cp: cp.wait())
            loop(nchunk, chunk_copy, lambda cp: cp.wait())
            return c

        lax.fori_loop(0, N_EXPERTS, per_expert, 0)

    def row_copy(g, u, k):
        d = dest_ref[0, 0, k * tm + g * SUBLANES + u]
        return pltpu.make_async_copy(hx_ref.at[g, pl.ds(u, 1), :], xs_hbm.at[pl.ds(d, 1), :], sem)

    def start(g, c):
        for u in range(SUBLANES):
            for k in range(TOP_K):
                row_copy(g, u, k).start(priority=k % 2)
        return c

    def wait(g, c):
        for u in range(SUBLANES):
            for k in range(TOP_K):
                row_copy(g, u, k).wait()
        return c

    lax.fori_loop(0, tm // SUBLANES, start, 0)
    lax.fori_loop(0, tm // SUBLANES, wait, 0)


def _dispatch(fill, dest_tiles, hx, n_slots, tm, bm):
    t, dm = hx.shape
    kern = functools.partial(_dispatch_kernel, tm=tm, bm=bm)
    return pl.pallas_call(
        kern,
        grid_spec=pltpu.PrefetchScalarGridSpec(
            num_scalar_prefetch=1,
            grid=(t // tm,),
            in_specs=[pl.BlockSpec((1, 1, TOP_K * tm), lambda i, f: (i, 0, 0), memory_space=pltpu.SMEM),
                      pl.BlockSpec((tm // SUBLANES, SUBLANES, dm), lambda i, f: (i, 0, 0))],
            out_specs=pl.BlockSpec(memory_space=pl.ANY),
            scratch_shapes=[pltpu.VMEM((bm, dm), hx.dtype),
                            pltpu.SemaphoreType.DMA(()), pltpu.SemaphoreType.DMA(())]),
        out_shape=jax.ShapeDtypeStruct((n_slots, dm), hx.dtype),
        name="dispatch",
        compiler_params=_cp(("arbitrary",)),
    )(fill, dest_tiles, hx.reshape(t // SUBLANES, SUBLANES, dm))


def _moe_kernel(be_ref, nv_ref, ne_ref, bv_ref, x_ref, wg_hbm, bg_ref, wu_hbm, bu_ref, wd_hbm, bd_ref, y_ref,
                wgb, wub, wdb, wgs, wus, wds, sem, *, row_step):
    i = pl.program_id(0)
    bm = x_ref.shape[0]

    def fetch(e):
        return (pltpu.make_async_copy(wg_hbm.at[e], wgs, sem.at[0]),
                pltpu.make_async_copy(wu_hbm.at[e], wus, sem.at[1]),
                pltpu.make_async_copy(wd_hbm.at[e], wds, sem.at[2]))

    @pl.when(i < nv_ref[0])
    def _():
        e = be_ref[i]
        prev = be_ref[jnp.maximum(i - 1, 0)]

        @pl.when(i == 0)
        def _():
            for cp in fetch(e):
                cp.start()

        @pl.when((i == 0) | (e != prev))
        def _():
            for cp in fetch(e):
                cp.wait()
            wgb[...] = wgs[...].astype(BF16)
            wub[...] = wus[...].astype(BF16)
            wdb[...] = wds[...].astype(BF16)
            nxt = ne_ref[i]

            @pl.when(nxt >= 0)
            def _():
                for cp in fetch(nxt):
                    cp.start()

        def compute(rows):
            xb = _unpack_bf16_pairs(x_ref[0:rows, :])
            g = jnp.minimum(_dot(xb, wgb[...]) + bg_ref[0], SWIGLU_LIMIT)
            u = jnp.clip(_dot(xb, wub[...]) + bu_ref[0], -SWIGLU_LIMIT, SWIGLU_LIMIT)
            act = g * (1.0 / (1.0 + jnp.exp(-SWIGLU_ALPHA * g))) * (u + 1.0)
            y_ref[0:rows, :] = _dot(act.astype(BF16), wdb[...]) + bd_ref[0]
            if rows < bm:
                y_ref[rows:bm, :] = jnp.zeros((bm - rows, y_ref.shape[1]), y_ref.dtype)

        valid = bv_ref[i]
        for rows in range(row_step, bm + 1, row_step):
            lo = rows - row_step if rows > row_step else -1
            pl.when((valid > lo) & (valid <= rows))(functools.partial(compute, rows))

    @pl.when(i >= nv_ref[0])
    def _():
        y_ref[...] = jnp.zeros_like(y_ref)


def _experts(block_e, nvalid, next_e, block_valid, xs, wg, bg, wu, bu, wd, bd, bm):
    n_slots = xs.shape[0]
    e, d, f = wg.shape
    nb = n_slots // bm
    xmap = lambda i, be, nv, ne, bv: (jnp.minimum(i, nv[0] - 1), 0)
    bmap = lambda i, be, nv, ne, bv: (be[i], 0, 0)
    hbm = pl.BlockSpec(memory_space=pl.ANY)
    return pl.pallas_call(
        functools.partial(_moe_kernel, row_step=bm // 4),
        grid_spec=pltpu.PrefetchScalarGridSpec(
            num_scalar_prefetch=4,
            grid=(nb,),
            in_specs=[pl.BlockSpec((bm, xs.shape[1]), xmap),
                      hbm, pl.BlockSpec((1, 1, f), bmap),
                      hbm, pl.BlockSpec((1, 1, f), bmap),
                      hbm, pl.BlockSpec((1, 1, d), bmap)],
            out_specs=pl.BlockSpec((bm, d), lambda i, be, nv, ne, bv: (i, 0)),
            scratch_shapes=[pltpu.VMEM((d, f), BF16), pltpu.VMEM((d, f), BF16), pltpu.VMEM((f, d), BF16),
                            pltpu.VMEM((d, f), F32), pltpu.VMEM((d, f), F32), pltpu.VMEM((f, d), F32),
                            pltpu.SemaphoreType.DMA((3,))]),
        out_shape=jax.ShapeDtypeStruct((n_slots, d), F32),
        name="experts",
        compiler_params=_cp(("arbitrary",)),
    )(block_e, nvalid, next_e, block_valid, xs, wg, bg.reshape(e, 1, f), wu, bu.reshape(e, 1, f), wd,
      bd.reshape(e, 1, d))


def _combine_kernel(dcur_ref, dnxt_ref, ys_hbm, gates_ref, x1_ref, g2_ref, gpost_ref, o_ref, buf, sem,
                    *, tm, nsteps):
    i = pl.program_id(0)
    slot = i % 2

    def row_copy(dref, sl, g, u, k):
        d = dref[0, 0, k * tm + g * SUBLANES + u]
        return pltpu.make_async_copy(ys_hbm.at[pl.ds(d, 1), :], buf.at[sl, k, g, pl.ds(u, 1), :], sem.at[sl])

    def issue(dref, sl):
        def start(g, c):
            for u in range(SUBLANES):
                for k in range(TOP_K):
                    row_copy(dref, sl, g, u, k).start(priority=k % 2)
            return c

        lax.fori_loop(0, tm // SUBLANES, start, 0)

    @pl.when(i == 0)
    def _():
        issue(dcur_ref, 0)

    @pl.when(i + 1 < nsteps)
    def _():
        issue(dnxt_ref, 1 - slot)

    def wait(g, c):
        for u in range(SUBLANES):
            for k in range(TOP_K):
                row_copy(dcur_ref, slot, g, u, k).wait()
        return c

    lax.fori_loop(0, tm // SUBLANES, wait, 0)
    gt = gates_ref[...]
    rows = lambda k: buf[slot, k].reshape(tm, buf.shape[-1])
    y = gt[:, 0:1] * rows(0)
    for k in range(1, TOP_K):
        y = y + gt[:, k:k + 1] * rows(k)
    o_ref[...] = x1_ref[...] + g2_ref[0] * (_rms(y) * gpost_ref[...])


def _combine(dest_tiles, ys, gates_t, x1, m3, gpost, tm):
    b, n, d = x1.shape
    nt = n // tm
    nsteps = b * nt
    kern = functools.partial(_combine_kernel, tm=tm, nsteps=nsteps)
    dspec = lambda f: pl.BlockSpec((1, 1, TOP_K * tm), f, memory_space=pltpu.SMEM)
    out = pl.pallas_call(
        kern,
        grid=(nsteps,),
        in_specs=[dspec(lambda i: (i, 0, 0)),
                  dspec(lambda i: (jnp.minimum(i + 1, nsteps - 1), 0, 0)),
                  pl.BlockSpec(memory_space=pl.ANY),
                  pl.BlockSpec((tm, LANES), lambda i: (i, 0)),
                  pl.BlockSpec((tm, d), lambda i: (i, 0)),
                  pl.BlockSpec((1, 1, d), lambda i: (i // nt * N_MOD + 5, 0, 0)),
                  pl.BlockSpec(gpost.shape, lambda i: (0, 0))],
        out_specs=pl.BlockSpec((tm, d), lambda i: (i, 0)),
        out_shape=jax.ShapeDtypeStruct((b * n, d), F32),
        scratch_shapes=[pltpu.VMEM((2, TOP_K, tm // SUBLANES, SUBLANES, d), F32),
                        pltpu.SemaphoreType.DMA((2,))],
        name="combine",
        compiler_params=_cp(("arbitrary",)),
    )(dest_tiles, dest_tiles, ys, gates_t, x1.reshape(b * n, d), m3, gpost)
    return out.reshape(b, n, d)


def _rope_tables(n):
    f32 = np.float32
    rows = n // GRID_W
    row = np.repeat(np.arange(rows, dtype=f32), GRID_W)
    col = np.tile(np.arange(GRID_W, dtype=f32), rows)
    half = QK_ROPE // 2
    inv = (f32(1.0) / (f32(ROPE_BASE) ** (np.arange(0, half, 2, dtype=f32) / f32(half)))).astype(f32)
    ang = np.concatenate([row[:, None] * inv, col[:, None] * inv], axis=-1).astype(f32)
    cos, sin = np.cos(ang), np.sin(ang)
    zero = np.zeros((n, ROPE_W - QK_ROPE), f32)
    return (jnp.asarray(np.concatenate([cos, cos, zero], axis=1)),
            jnp.asarray(np.concatenate([sin, sin, zero], axis=1)))


def _ctx_tables(n):
    half = QK_ROPE // 2
    t1 = np.zeros((n, ROPE_W), np.float32)
    t1[:, :2 * half] = 1.0
    return jnp.asarray(t1), jnp.zeros((n, ROPE_W), F32)


def _prep_weights(w_in, norm_q, norm_kv, w_q_up, w_kv_up):
    d = w_in.shape[0]
    half = QK_ROPE // 2
    zr = lambda r, c: jnp.zeros((r, c), F32)
    o_rope = F_WIDTH + Q_LORA + KV_LORA
    kr = w_in[:, o_rope:o_rope + QK_ROPE]
    kre, kro = kr[:, 0::2], kr[:, 1::2]
    win = jnp.concatenate([w_in[:, :o_rope],
                           kre, kro, zr(d, ROPE_W - QK_ROPE),
                           -kro, kre, zr(d, ROPE_W - QK_ROPE)], axis=1).astype(BF16)
    hd = QK_NOPE + QK_ROPE
    qa, qb = [], []
    for h in range(N_HEADS):
        wn = w_q_up[:, h * hd:h * hd + QK_NOPE]
        wr = w_q_up[:, h * hd + QK_NOPE:(h + 1) * hd]
        we, wo = wr[:, 0::2], wr[:, 1::2]
        qa += [wn, we, wo, zr(Q_LORA, ROPE_W - QK_ROPE)]
        qb += [-wo, we, zr(Q_LORA, ROPE_W - QK_ROPE)]
    wqa = jnp.concatenate(qa, axis=1).astype(BF16)
    wqb = jnp.concatenate(qb, axis=1).astype(BF16)
    kvd = QK_NOPE + V_DIM
    wk = jnp.concatenate([w_kv_up[:, h * kvd:h * kvd + QK_NOPE] for h in range(N_HEADS)], axis=1).astype(BF16)
    wv = jnp.concatenate([w_kv_up[:, h * kvd + QK_NOPE:(h + 1) * kvd] for h in range(N_HEADS)], axis=1)
    wvt = wv.T.astype(BF16)
    return win, norm_q.reshape(1, -1), norm_kv.reshape(1, -1), wqa, wqb, wk, wvt


def _fourier_consts(n, w_fourier):
    c = F_GDIM
    j = np.arange(c)
    ang = 2.0 * np.pi * np.outer(j, j) / c
    eye = np.eye(F_GROUPS)
    bdc = np.kron(eye, np.cos(ang))
    bds = np.kron(eye, np.sin(ang))
    bdcs = jnp.asarray(np.concatenate([bdc, -bds], axis=1), dtype=BF16)
    n1 = FFT_N1
    n2 = n // n1
    k1 = np.arange(n1)
    a1 = 2.0 * np.pi * np.outer(k1, k1) / n1
    wc, ws = np.cos(a1), np.sin(a1)
    w1 = jnp.asarray(np.block([[wc, ws], [-ws, wc]]), dtype=BF16)
    tw = 2.0 * np.pi * np.outer(k1, np.arange(n2)) / n
    tc = jnp.asarray(np.repeat(np.cos(tw).astype(np.float32), F_WIDTH, axis=1))
    ts = jnp.asarray(np.repeat(np.sin(tw).astype(np.float32), F_WIDTH, axis=1))
    k2 = np.arange(n2)
    a2 = 2.0 * np.pi * np.outer(k2, k2) / n2
    norm = 1.0 / math.sqrt(n * c)
    kt = FFT_K1_TILE
    mc = np.zeros((n2, kt, kt, n2))
    ms = np.zeros((n2, kt, kt, n2))
    for r in range(kt):
        mc[:, r, r, :] = np.cos(a2) * norm
        ms[:, r, r, :] = np.sin(a2) * norm
    mmat = np.concatenate([mc.reshape(n2 * kt, kt * n2), ms.reshape(n2 * kt, kt * n2)], axis=1)
    mmat = jnp.asarray(mmat, dtype=BF16)
    zblk = jnp.zeros((c, c), F32)
    bdw = jnp.concatenate(
        [jnp.concatenate([w_fourier[g] if j == g else zblk for j in range(F_GROUPS)], axis=1)
         for g in range(F_GROUPS)], axis=0)
    return bdcs, w1, tc, ts, mmat, bdw.astype(BF16)


def kernel(x, c, ctx, c_ctx, w_mod, b_mod, norm_attn_pre, norm_attn_post, norm_ffn_pre, norm_ffn_post, w_in, norm_q_lat, norm_kv_lat, w_q_up, w_kv_up, w_fourier, w_out, w_router, b_router, w_gate, b_gate, w_up, b_up, w_down, b_down):
    b, n, d = x.shape
    nctx = ctx.shape[1]
    t = b * n
    assert w_mod.shape[0] == 1 and b + 1 <= SUBLANES and n % (FFT_N1 * SUBLANES) == 0
    row2 = lambda a: a[0].reshape(1, -1)

    c8 = jnp.concatenate([c, c_ctx[None, :], jnp.zeros((SUBLANES - b - 1, d), F32)], axis=0)
    m3 = _modulation(c8, w_mod[0], b_mod[0]).reshape(SUBLANES * N_MOD, 1, d)

    wts = _prep_weights(w_in[0], norm_q_lat[0], norm_kv_lat[0], w_q_up[0], w_kv_up[0])
    bdcs, w1, tc, ts, mmat, bdw = _fourier_consts(n, w_fourier[0])
    pro_w = (wts[0], bdcs) + wts[1:]
    scale = (QK_NOPE + QK_ROPE) ** -0.5 * LOG2E
    g_pre = row2(norm_attn_pre)

    tm = min(512, n)
    t1x, t2x = _rope_tables(n)
    q, k, vt, za, zb = _prologue(x, m3, lambda bi: bi, g_pre, pro_w, t1x, t2x, tm, scale)
    t1c, t2c = _ctx_tables(nctx)
    _, kc, vtc, _, _ = _prologue(ctx, m3, lambda bi: b, g_pre, pro_w, t1c, t2c, nctx, scale)

    att = _attention(q, k, vt, kc, vtc, min(1024, n))
    four = _fourier(za, zb, w1, tc, ts, mmat, bdw)

    tmp = min(512, n)
    tri = jnp.asarray(np.triu(np.ones((tmp, tmp), np.float32), 1), dtype=BF16)
    width = N_HEADS * V_DIM
    woa = w_out[0, :width].astype(BF16)
    wof = w_out[0, width:].astype(BF16)
    x1, hx2, ridx, gates_t, cnt = _post_attention(
        att, four, x, m3, row2(norm_attn_post), row2(norm_ffn_pre), woa, wof,
        w_router[0].T.astype(BF16), b_router[0].reshape(-1, 1), tri, tmp)

    bm = 512
    counts = cnt[:, 0].astype(I32)
    padded = (counts + bm - 1) // bm * bm
    pad_end = jnp.cumsum(padded)
    pad_start = pad_end - padded
    nb = t * TOP_K // bm + N_EXPERTS
    blk_start = jnp.arange(nb, dtype=I32)[:, None] * bm
    block_e = jnp.minimum(jnp.sum((pad_end[None, :] <= blk_start).astype(I32), axis=1), N_EXPERTS - 1)
    nvalid = (pad_end[-1:] // bm).astype(I32)
    ids = jnp.arange(N_EXPERTS, dtype=I32)
    later = (ids[None, :] > ids[:, None]) & (counts[None, :] > 0)
    nxt = jnp.min(jnp.where(later, ids[None, :], N_EXPERTS), axis=1)
    nxt = jnp.where(nxt == N_EXPERTS, -1, nxt)
    of_block = ids[None, :] == block_e[:, None]
    next_e = jnp.sum(jnp.where(of_block, nxt[None, :], 0), axis=1).astype(I32)
    used_end = jnp.sum(jnp.where(of_block, (pad_start + counts)[None, :], 0), axis=1)
    block_valid = jnp.clip(used_end - blk_start[:, 0], 0, bm).astype(I32)
    eids = jnp.arange(N_EXPERTS, dtype=I32)[:, None, None, None]
    eidx, rank = ridx[:, :TOP_K], ridx[:, TOP_K:]
    dest = jnp.sum(jnp.where(eidx[None] == eids, pad_start[:, None, None, None], 0), axis=0) + rank
    dest_tiles = dest.reshape(t // tmp, 1, TOP_K * tmp)
    fill = jnp.concatenate([pad_start + counts, padded - counts, nvalid]).astype(I32)

    xs = _dispatch(fill, dest_tiles, hx2, nb * bm, tmp, bm)
    ys = _experts(block_e, nvalid, next_e, block_valid, xs, w_gate[0], b_gate[0], w_up[0], b_up[0], w_down[0], b_down[0], bm)
    return _combine(dest_tiles, ys, gates_t, x1, m3, row2(norm_ffn_post), tmp)
```
